```python
import jax, jax.numpy as jnp
from jax import lax
import numpy as np

D_MODEL = 1024
BATCH = 4
SEQ = 8192
DEPTH = 2

GRID_W = 64
CTX_LEN = 256
MIX_W = D_MODEL
HEAD_DIM = 64
GDN_W = MIX_W // 2
HGRN_W = MIX_W - GDN_W
GDN_HEADS = GDN_W // HEAD_DIM
HGRN_HEADS = HGRN_W // HEAD_DIM
CONV_W = 3
CHUNK = 64
D_FF = 4 * D_MODEL
N_MOD = 6
EPS = 1e-6
LB_FLOOR = 1e-30
IN_SIZES = (3 * GDN_W, GDN_W, 4 * GDN_HEADS, HGRN_W, HGRN_W, HGRN_W, HGRN_W, HGRN_W)
N_IN = 4 * GDN_W + 4 * GDN_HEADS + 5 * HGRN_W

kernel_name = 'hybrid_gdn_hgrn2_prefix_dit'


def rms_norm(x, g):
    xf = x.astype(jnp.float32)
    y = xf * lax.rsqrt(jnp.mean(xf * xf, axis=-1, keepdims=True) + EPS)
    return (y * g.astype(jnp.float32)).astype(x.dtype)


def head_rms(o, g):
    return o * lax.rsqrt(jnp.mean(o * o, axis=-1, keepdims=True) + EPS) * g.astype(jnp.float32)


def modulate(h, shift, scale):
    return h * (1 + scale) + shift


def split_proj(p):
    idx, acc = [], 0
    for s in IN_SIZES[:-1]:
        acc += s
        idx.append(acc)
    return jnp.split(p, idx, axis=-1)


def heads(z, n_heads):
    b, t, _ = z.shape
    return z.reshape(b, t, n_heads, -1).transpose(0, 2, 1, 3)


def merge_heads(z):
    b, h, t, d = z.shape
    return z.transpose(0, 2, 1, 3).reshape(b, t, h * d)


def flip_t(z):
    return jnp.flip(z, axis=2)


def to_col_major(z):
    b, l, ch = z.shape
    rows = l // GRID_W
    return z.reshape(b, rows, GRID_W, ch).transpose(0, 2, 1, 3).reshape(b, l, ch)


def from_col_major(z):
    b, l, ch = z.shape
    rows = l // GRID_W
    return z.reshape(b, GRID_W, rows, ch).transpose(0, 2, 1, 3).reshape(b, l, ch)


def short_conv(z, w):
    return lax.conv_general_dilated(
        z, w[:, None, :], window_strides=(1,), padding=((CONV_W // 2, CONV_W // 2),),
        dimension_numbers=('NWC', 'WIO', 'NWC'), feature_group_count=z.shape[-1])


def l2norm(z):
    return z * lax.rsqrt(jnp.sum(z * z, axis=-1, keepdims=True) + EPS)


def masked_exp(diff, mask):
    return jnp.where(mask, jnp.exp(jnp.where(mask, diff, 0.0)), 0.0)


def gated_delta_chunked(q, k, v, log_g, beta, s0):
    b, h, t, dk = q.shape
    dv = v.shape[-1]
    n = t // CHUNK
    rs = lambda z: z.reshape(b, h, n, CHUNK, *z.shape[3:])
    q, k, v, log_g, beta = rs(q), rs(k), rs(v), rs(log_g), rs(beta)
    gam = jnp.cumsum(log_g, axis=-1)
    incl = jnp.tril(jnp.ones((CHUNK, CHUNK), bool))
    strict = jnp.tril(jnp.ones((CHUNK, CHUNK), bool), -1)
    diff = gam[..., :, None] - gam[..., None, :]
    dec_incl = masked_exp(diff, incl)
    dec_strict = jnp.where(strict, dec_incl, 0.0)
    kb = k * beta[..., None]
    m = jnp.einsum('bhnid,bhnjd->bhnij', kb, k) * dec_strict
    a = m + jnp.eye(CHUNK, dtype=q.dtype)
    u = lax.linalg.triangular_solve(a, v * beta[..., None], left_side=True, lower=True, unit_diagonal=True)
    w = lax.linalg.triangular_solve(a, kb * jnp.exp(gam)[..., None], left_side=True, lower=True, unit_diagonal=True)
    qk = jnp.einsum('bhnid,bhnjd->bhnij', q, k) * dec_incl
    q_dec = q * jnp.exp(gam)[..., None]
    k_dec = k * jnp.exp(gam[..., -1:] - gam)[..., None]
    g_last = jnp.exp(gam[..., -1])

    def step(s, xs):
        u_c, w_c, qk_c, qd_c, kd_c, gl_c = xs
        v_new = u_c - jnp.einsum('bhck,bhkv->bhcv', w_c, s)
        o = jnp.einsum('bhck,bhkv->bhcv', qd_c, s) + jnp.einsum('bhij,bhjv->bhiv', qk_c, v_new)
        s = gl_c[..., None, None] * s + jnp.einsum('bhck,bhcv->bhkv', kd_c, v_new)
        return s, o

    xs = tuple(jnp.moveaxis(z, 2, 0) for z in (u, w, qk, q_dec, k_dec, g_last))
    s_fin, o = lax.scan(step, s0, xs)
    return jnp.moveaxis(o, 0, 2).reshape(b, h, t, dv), s_fin


def hgrn2_chunked(q, k, v, log_f, s0):
    b, h, t, dk = q.shape
    dv = v.shape[-1]
    n = t // CHUNK
    rs = lambda z: z.reshape(b, h, n, CHUNK, *z.shape[3:])
    q, k, v, log_f = rs(q), rs(k), rs(v), rs(log_f)
    bc = jnp.cumsum(log_f, axis=3)
    q_dec = q * jnp.exp(bc)
    k_dec = k * jnp.exp(bc[..., -1:, :] - bc)
    f_last = jnp.exp(bc[..., -1, :])
    incl = jnp.tril(jnp.ones((CHUNK, CHUNK), bool))[..., None]

    def step(s, xs):
        q_c, k_c, v_c, b_c, qd_c, kd_c, fl_c = xs
        diff = b_c[:, :, :, None, :] - b_c[:, :, None, :, :]
        dec = masked_exp(diff, incl)
        a = jnp.einsum('bhid,bhjd,bhijd->bhij', q_c, k_c, dec)
        o = jnp.einsum('bhck,bhkv->bhcv', qd_c, s) + jnp.einsum('bhij,bhjv->bhiv', a, v_c)
        s = fl_c[..., None] * s + jnp.einsum('bhck,bhcv->bhkv', kd_c, v_c)
        return s, o

    xs = tuple(jnp.moveaxis(z, 2, 0) for z in (q, k, v, bc, q_dec, k_dec, f_last))
    s_fin, o = lax.scan(step, s0, xs)
    return jnp.moveaxis(o, 0, 2).reshape(b, h, t, dv), s_fin


def two_way(scan_fn, ctx_f, ctx_b, lat_f, lat_b, s0):
    oc_f, sc_f = scan_fn(*ctx_f, s0)
    oc_b, sc_b = scan_fn(*tuple(flip_t(z) for z in ctx_b), s0)
    ol_f, _ = scan_fn(*lat_f, sc_f)
    ol_b, _ = scan_fn(*tuple(flip_t(z) for z in lat_b), sc_b)
    return oc_f + flip_t(oc_b), ol_f + flip_t(ol_b)


def gdn_inputs(qkv, ab, conv_w, a_log, dt_bias):
    qkv = jax.nn.silu(short_conv(qkv.astype(jnp.float32), conv_w.astype(jnp.float32)))
    q, k, v = jnp.split(qkv, 3, axis=-1)
    q = l2norm(heads(q, GDN_HEADS)) * HEAD_DIM ** -0.5
    k = l2norm(heads(k, GDN_HEADS))
    v = heads(v, GDN_HEADS)
    b, t, _ = ab.shape
    ab = ab.astype(jnp.float32).reshape(b, t, 4, GDN_HEADS)
    log_g = -jnp.exp(a_log.astype(jnp.float32)) * jax.nn.softplus(ab[:, :, 0:2] + dt_bias.astype(jnp.float32))
    beta = jax.nn.sigmoid(ab[:, :, 2:4])
    return q, k, v, log_g.transpose(2, 0, 3, 1), beta.transpose(2, 0, 3, 1)


def hgrn_inputs(q, f_f, f_b, i, lb):
    q = heads(jax.nn.silu(q.astype(jnp.float32)), HGRN_HEADS) * HEAD_DIM ** -0.5
    v = heads(i.astype(jnp.float32), HGRN_HEADS)
    ks, lfs = [], []
    for d, fz in enumerate((f_f, f_b)):
        z = heads(fz.astype(jnp.float32), HGRN_HEADS)
        lbd = lb[d].astype(jnp.float32).reshape(HGRN_HEADS, 1, HEAD_DIM)
        log_lb = jnp.log(jnp.maximum(lbd, LB_FLOOR))
        lfs.append(jnp.logaddexp(log_lb, jnp.log1p(-lbd) + jax.nn.log_sigmoid(z)))
        ks.append((1 - lbd) * jax.nn.sigmoid(-z))
    return q, v, ks, lfs


def mixer(h_ctx, h_lat, w_in, conv_w, a_log, dt_bias, gdn_g, hgrn_g, lb, w_out, need_ctx):
    dt = h_lat.dtype
    b = h_lat.shape[0]
    pc = split_proj(h_ctx @ w_in)
    pl = split_proj(h_lat @ w_in)
    qc, kc, vc, gc, bc = gdn_inputs(pc[0], pc[2], conv_w, a_log, dt_bias)
    ql, kl, vl, gl, bl = gdn_inputs(pl[0], pl[2], conv_w, a_log, dt_bias)
    s0a = jnp.zeros((b, GDN_HEADS, HEAD_DIM, HEAD_DIM), jnp.float32)
    oa_c, oa_l = two_way(gated_delta_chunked,
                         (qc, kc, vc, gc[0], bc[0]), (qc, kc, vc, gc[1], bc[1]),
                         (ql, kl, vl, gl[0], bl[0]), (ql, kl, vl, gl[1], bl[1]), s0a)
    hq_c, hv_c, hk_c, hf_c = hgrn_inputs(pc[3], pc[4], pc[5], pc[6], lb)
    hq_l, hv_l, hk_l, hf_l = hgrn_inputs(*tuple(to_col_major(z) for z in pl[3:7]), lb)
    s0b = jnp.zeros((b, HGRN_HEADS, HEAD_DIM, HEAD_DIM), jnp.float32)
    ob_c, ob_l = two_way(hgrn2_chunked,
                         (hq_c, hk_c[0], hv_c, hf_c[0]), (hq_c, hk_c[1], hv_c, hf_c[1]),
                         (hq_l, hk_l[0], hv_l, hf_l[0]), (hq_l, hk_l[1], hv_l, hf_l[1]), s0b)

    def merge(oa, ob, ga, gb, col_major):
        ya = merge_heads(head_rms(oa, gdn_g)) * jax.nn.silu(ga.astype(jnp.float32))
        yb = merge_heads(head_rms(ob, hgrn_g))
        if col_major:
            yb = from_col_major(yb)
        yb = yb * jax.nn.sigmoid(gb.astype(jnp.float32))
        return jnp.concatenate([ya, yb], axis=-1).astype(dt) @ w_out

    y_lat = merge(oa_l, ob_l, pl[1], pl[7], True)
    y_ctx = merge(oa_c, ob_c, pc[1], pc[7], False) if need_ctx else None
    return y_ctx, y_lat


def sq_relu_mlp(h, w1, w2):
    return jnp.square(jax.nn.relu(h @ w1)) @ w2


def setup_inputs(seed: int = 0) -> dict:
    key = jax.random.key(seed)
    ks = jax.random.split(key, 20)
    nrm = lambda k, shape, s: jax.random.normal(k, shape, jnp.float32) * s
    x = nrm(ks[0], (BATCH, SEQ, D_MODEL), 1.0)
    c = nrm(ks[1], (BATCH, D_MODEL), 1.0)
    ctx = nrm(ks[2], (BATCH, CTX_LEN, D_MODEL), 1.0)
    c_ctx = nrm(ks[3], (D_MODEL,), 1.0)
    w_mod = nrm(ks[4], (DEPTH, D_MODEL, N_MOD * D_MODEL), D_MODEL ** -0.5)
    b_mod = nrm(ks[5], (DEPTH, N_MOD * D_MODEL), 0.02)
    norm1_g = 1.0 + nrm(ks[6], (DEPTH, D_MODEL), 0.05)
    norm2_g = 1.0 + nrm(ks[7], (DEPTH, D_MODEL), 0.05)
    w_in = nrm(ks[8], (DEPTH, D_MODEL, N_IN), D_MODEL ** -0.5)
    conv_w = nrm(ks[9], (DEPTH, CONV_W, 3 * GDN_W), CONV_W ** -0.5)
    a_log = jnp.log(jax.random.uniform(ks[10], (DEPTH, 2, GDN_HEADS), jnp.float32, 1.0, 16.0))
    dt_bias = jnp.log(jnp.expm1(jax.random.uniform(ks[11], (DEPTH, 2, GDN_HEADS), jnp.float32, 1e-3, 1e-1)))
    gdn_norm_g = 1.0 + nrm(ks[12], (DEPTH, HEAD_DIM), 0.05)
    hgrn_norm_g = 1.0 + nrm(ks[13], (DEPTH, HEAD_DIM), 0.05)
    lb_logits = nrm(ks[14], (DEPTH, 2, HGRN_W), 0.5)
    w_out = nrm(ks[15], (DEPTH, MIX_W, D_MODEL), MIX_W ** -0.5)
    w_mlp1 = nrm(ks[16], (DEPTH, D_MODEL, D_FF), D_MODEL ** -0.5)
    w_mlp2 = nrm(ks[17], (DEPTH, D_FF, D_MODEL), D_FF ** -0.5)
    final_g = 1.0 + nrm(ks[18], (D_MODEL,), 0.05)
    return {'x': x, 'c': c, 'ctx': ctx, 'c_ctx': c_ctx, 'w_mod': w_mod, 'b_mod': b_mod,
            'norm1_g': norm1_g, 'norm2_g': norm2_g, 'w_in': w_in, 'conv_w': conv_w,
            'a_log': a_log, 'dt_bias': dt_bias, 'gdn_norm_g': gdn_norm_g, 'hgrn_norm_g': hgrn_norm_g,
            'lb_logits': lb_logits, 'w_out': w_out, 'w_mlp1': w_mlp1, 'w_mlp2': w_mlp2, 'final_g': final_g}


def reference(x, c, ctx, c_ctx, w_mod, b_mod, norm1_g, norm2_g, w_in, conv_w, a_log, dt_bias,
              gdn_norm_g, hgrn_norm_g, lb_logits, w_out, w_mlp1, w_mlp2, final_g):
    p = jax.nn.softmax(lb_logits.astype(jnp.float32), axis=0)
    lower_bounds = jnp.maximum(jnp.cumsum(p, axis=0) - p[0:1], 0.0)
    sc_lat = jax.nn.silu(c)
    sc_ctx = jax.nn.silu(c_ctx)
    x_lat, x_ctx = x, ctx
    for l in range(DEPTH):
        need_ctx = l < DEPTH - 1
        m_lat = jnp.split((sc_lat @ w_mod[l] + b_mod[l])[:, None, :], N_MOD, axis=-1)
        m_ctx = jnp.split(sc_ctx @ w_mod[l] + b_mod[l], N_MOD, axis=-1)
        h_lat = modulate(rms_norm(x_lat, norm1_g[l]), m_lat[0], m_lat[1])
        h_ctx = modulate(rms_norm(x_ctx, norm1_g[l]), m_ctx[0], m_ctx[1])
        y_ctx, y_lat = mixer(h_ctx, h_lat, w_in[l], conv_w[l], a_log[l], dt_bias[l],
                             gdn_norm_g[l], hgrn_norm_g[l], lower_bounds[l], w_out[l], need_ctx)
        x_lat = x_lat + m_lat[2] * y_lat
        h_lat = modulate(rms_norm(x_lat, norm2_g[l]), m_lat[3], m_lat[4])
        x_lat = x_lat + m_lat[5] * sq_relu_mlp(h_lat, w_mlp1[l], w_mlp2[l])
        if need_ctx:
            x_ctx = x_ctx + m_ctx[2] * y_ctx
            h_ctx = modulate(rms_norm(x_ctx, norm2_g[l]), m_ctx[3], m_ctx[4])
            x_ctx = x_ctx + m_ctx[5] * sq_relu_mlp(h_ctx, w_mlp1[l], w_mlp2[l])
    return rms_norm(x_lat, final_g)
```

```python
import functools

import numpy as np
import jax
import jax.numpy as jnp
from jax import lax
from jax.experimental import pallas as pl
from jax.experimental.pallas import tpu as pltpu

F32 = jnp.float32
BF16 = jnp.bfloat16

HEAD_DIM = 64
N_HEADS = 8
GROUP_W = N_HEADS * HEAD_DIM
N_PAIRS = N_HEADS // 2
PAIR_W = 2 * HEAD_DIM
CHUNK = 64
GRID_W = 64
CONV_W = 3
N_MOD = 6
EPS = 1e-6
LB_FLOOR = 1e-30
AB_PAD = 128
VMEM_LIMIT = 56 * 1024 * 1024

PROJ_TM = 256
MLP_TM = 256
GDN_TT = 256
HGRN_TT = 128


def _dot(a, b):
    return jnp.dot(a, b, preferred_element_type=F32)


def _dot_nt(a, b):
    return lax.dot_general(a, b, (((1,), (1,)), ((), ())), preferred_element_type=F32)


def _dot_sel(sel, x):
    hi = x.astype(BF16)
    lo = (x - hi.astype(F32)).astype(BF16)
    return _dot(sel, hi) + _dot(sel, lo)


def _dot_sel_rhs(x, sel):
    hi = x.astype(BF16)
    lo = (x - hi.astype(F32)).astype(BF16)
    return _dot(hi, sel) + _dot(lo, sel)


def _bd(x, bdmask):
    return jnp.concatenate([x, x], axis=0) * bdmask.astype(x.dtype)


def _sigmoid(x):
    return 1.0 / (1.0 + jnp.exp(-x))


def _silu(x):
    return x * _sigmoid(x)


def _softplus(x):
    return jnp.maximum(x, 0.0) + jnp.log1p(jnp.exp(-jnp.abs(x)))


def _cparams(sem):
    return pltpu.CompilerParams(dimension_semantics=sem, vmem_limit_bytes=VMEM_LIMIT)


def _packed_ij():
    i = np.arange(CHUNK)[:, None]
    j = (np.arange(PAIR_W) % HEAD_DIM)[None, :]
    return i, j


def _bd_mask():
    r = np.arange(PAIR_W)[:, None] // HEAD_DIM
    c = np.arange(PAIR_W)[None, :] // HEAD_DIM
    return (r == c).astype(np.float32)


def _seg_ones(width):
    r = np.arange(width)[:, None] // HEAD_DIM
    c = np.arange(width)[None, :] // HEAD_DIM
    return (r == c).astype(np.float32)


_G_SUP, _G_INCL, _G_STRICT, _G_EYE, _G_B8, _G_C16, _G_C32, _G_C64 = range(8)


def _gdn_consts(reverse):
    i, j = _packed_ij()
    t = np.arange(CHUNK)
    if reverse:
        tri = (t[None, :] >= t[:, None])
        sup = i < j
        incl = j >= i
        strict = j > i
    else:
        tri = (t[None, :] <= t[:, None])
        sup = i > j
        incl = j <= i
        strict = j < i
    eye = i == j
    b8 = (i // 8) == (j // 8)
    c16 = ((i // 16) == (j // 16)) & ~b8
    c32 = ((i // 32) == (j // 32)) & ((i // 16) != (j // 16))
    c64 = (i // 32) != (j // 32)
    stack = np.stack([sup, incl, strict, eye, b8, c16, c32, c64]).astype(np.float32)
    return tri.astype(np.float32), stack


def _gdn_expand(reverse):
    d = 1 if reverse else 0
    col = np.arange(AB_PAD)[:, None]
    head = (np.arange(GROUP_W) // HEAD_DIM)[None, :]
    e_lg = (col == d * N_HEADS + head)
    e_bt = (col == 2 * N_HEADS + d * N_HEADS + head)
    return e_lg.astype(np.float32), e_bt.astype(np.float32)


_H_LEVELS = (1, 2, 4, 8, 16, 32)


def _hgrn_consts(reverse):
    r = np.arange(CHUNK)[:, None]
    t = np.arange(CHUNK)[None, :]
    i, j = _packed_ij()
    sels, masks = [], []
    for s in _H_LEVELS:
        start = (r // s) * s
        end = start + s - 1
        odd = ((r // s) % 2) == 1
        if reverse:
            sel = np.where(~odd, (t >= r) & (t <= end), (t >= start) & (t < r))
            m = ((i // (2 * s)) == (j // (2 * s))) & (((i // s) % 2) == 0) & (((j // s) % 2) == 1)
        else:
            sel = np.where(odd, (t >= start) & (t <= r), (t > r) & (t <= end))
            m = ((i // (2 * s)) == (j // (2 * s))) & (((i // s) % 2) == 1) & (((j // s) % 2) == 0)
        sels.append(sel)
        masks.append(m)
    if reverse:
        sels.append(t >= r)
        sels.append(t < r)
    else:
        sels.append(t <= r)
        sels.append(t > r)
    masks.append(i == j)
    sel = np.concatenate(sels, axis=0).astype(np.float32)
    return sel, np.stack(masks).astype(np.float32)


def _mod_kernel(c_ref, w_ref, b_ref, o_ref):
    sc = _silu(c_ref[...]).astype(BF16)
    o_ref[0] = _dot(sc, w_ref[0].astype(BF16)) + b_ref[0]


def _modulation(cvec, w_mod, b_mod):
    depth, d, n = w_mod.shape
    tn = 1536
    return pl.pallas_call(
        _mod_kernel,
        grid=(depth, n // tn),
        in_specs=[pl.BlockSpec((8, d), lambda l, j: (0, 0)),
                  pl.BlockSpec((1, d, tn), lambda l, j: (l, 0, j)),
                  pl.BlockSpec((1, 1, tn), lambda l, j: (l, 0, j))],
        out_specs=pl.BlockSpec((1, 8, tn), lambda l, j: (l, 0, j)),
        out_shape=jax.ShapeDtypeStruct((depth, 8, n), F32),
        compiler_params=_cparams(("arbitrary", "arbitrary")),
        name="modulation",
    )(cvec, w_mod, b_mod.reshape(depth, 1, n))


_PROJ_WIDTHS = (3 * GROUP_W, GROUP_W, 4 * GROUP_W, GROUP_W, AB_PAD)


def _norm_mod(x, g, shift, scale):
    y = x * lax.rsqrt(jnp.mean(x * x, axis=-1, keepdims=True) + EPS) * g
    return y * (1.0 + scale) + shift


def _proj_kernel(x_ref, mod_ref, g_ref, w_ref, qkv_ref, ga_ref, ph_ref, gb_ref, ab_ref):
    h = _norm_mod(x_ref[0], g_ref[...], mod_ref[0, 0:1, :], mod_ref[0, 1:2, :]).astype(BF16)
    off = 0
    for ref, w in zip((qkv_ref, ga_ref, ph_ref, gb_ref, ab_ref), _PROJ_WIDTHS):
        ref[0] = _dot(h, w_ref[:, off:off + w])
        off += w


def _projection(x, mod, g, w_cat, shared_mod):
    b, t, d = x.shape
    tm = min(PROJ_TM, t)
    mod_map = (lambda bi, i: (0, 0, 0)) if shared_mod else (lambda bi, i: (bi, 0, 0))
    tok = lambda w: pl.BlockSpec((1, tm, w), lambda bi, i: (bi, i, 0))
    return pl.pallas_call(
        _proj_kernel,
        grid=(b, t // tm),
        in_specs=[tok(d),
                  pl.BlockSpec((1, N_MOD, d), mod_map),
                  pl.BlockSpec((1, d), lambda bi, i: (0, 0)),
                  pl.BlockSpec(w_cat.shape, lambda bi, i: (0, 0))],
        out_specs=[tok(w) for w in _PROJ_WIDTHS],
        out_shape=[jax.ShapeDtypeStruct((b, t, w), F32) for w in _PROJ_WIDTHS],
        compiler_params=_cparams(("arbitrary", "arbitrary")),
        name="projection",
    )(x, mod, g, w_cat)


def _gdn_kernel(reverse, n_tiles, qkv_ref, prev_ref, next_ref, ab_ref, convw_ref, alog_ref, dtb_ref,
                s0_ref, tri_ref, cm_ref, bdm_ref, elg_ref, ebt_ref, ones_ref,
                o_ref, sfin_ref, q_sc, k_sc, v_sc, lg_sc, bt_sc, s_sc):
    step = pl.program_id(1)
    tile = (n_tiles - 1 - step) if reverse else step
    tt = qkv_ref.shape[1]
    n_chunks = tt // CHUNK

    @pl.when(step == 0)
    def _():
        s_sc[...] = s0_ref[0]

    x = qkv_ref[0]
    row = lax.broadcasted_iota(jnp.int32, (tt, 1), 0)
    pv = jnp.where(tile > 0, prev_ref[0, 7:8, :], 0.0)
    nx = jnp.where(tile < n_tiles - 1, next_ref[0, 0:1, :], 0.0)
    x_prev = jnp.where(row == 0, pv, pltpu.roll(x, 1, axis=0))
    x_next = jnp.where(row == tt - 1, nx, pltpu.roll(x, tt - 1, axis=0))
    y = _silu(convw_ref[0:1, :] * x_prev + convw_ref[1:2, :] * x + convw_ref[2:3, :] * x_next)

    ones = ones_ref[...]
    q = y[:, 0:GROUP_W]
    k = y[:, GROUP_W:2 * GROUP_W]
    q_sc[...] = q * lax.rsqrt(_dot((q * q).astype(BF16), ones) + EPS) * (HEAD_DIM ** -0.5)
    k_sc[...] = k * lax.rsqrt(_dot((k * k).astype(BF16), ones) + EPS)
    v_sc[...] = y[:, 2 * GROUP_W:3 * GROUP_W]

    ab = ab_ref[0]
    lg_c = -jnp.exp(alog_ref[...]) * _softplus(ab + dtb_ref[...])
    lg_sc[...] = _dot_sel_rhs(lg_c, elg_ref[...])
    bt_sc[...] = _dot_sel_rhs(_sigmoid(ab), ebt_ref[...])

    tri = tri_ref[...]
    bdm = bdm_ref[...]
    sup, incl, strict, eye = cm_ref[_G_SUP], cm_ref[_G_INCL], cm_ref[_G_STRICT], cm_ref[_G_EYE]
    last = 0 if reverse else CHUNK - 1

    def pmul(a, b):
        return _dot(a.astype(BF16), _bd(b.astype(BF16), bdm))

    def chunk_body(ci, carry):
        c = (n_chunks - 1 - ci) if reverse else ci
        rows = pl.ds(pl.multiple_of(c * CHUNK, CHUNK), CHUNK)
        for p in range(N_PAIRS):
            lanes = slice(p * PAIR_W, (p + 1) * PAIR_W)
            qp, kp, vp = q_sc[rows, lanes], k_sc[rows, lanes], v_sc[rows, lanes]
            lg, bt = lg_sc[rows, lanes], bt_sc[rows, lanes]
            gam = _dot_sel(tri, lg)
            dincl = jnp.exp(_dot_sel(tri, lg * sup)) * incl
            kb = kp.astype(BF16)
            k2 = _bd(kb, bdm)
            kk = _dot_nt(kb, k2)
            qk = _dot_nt(qp.astype(BF16), k2)
            m = kk * bt * dincl * strict
            n1 = -(m * cm_ref[_G_B8])
            n2 = pmul(n1, n1)
            n4 = pmul(n2, n2)
            t_inv = eye + n1
            t_inv = t_inv + pmul(t_inv, n2)
            t_inv = t_inv + pmul(t_inv, n4)
            for lvl in (_G_C16, _G_C32, _G_C64):
                t_inv = t_inv - pmul(pmul(t_inv, m * cm_ref[lvl]), t_inv)
            eg = jnp.exp(gam)
            rhs = jnp.concatenate([_bd((vp * bt).astype(BF16), bdm),
                                   _bd((kp * bt * eg).astype(BF16), bdm)], axis=1)
            uw = _dot(t_inv.astype(BF16), rhs)
            u, w = uw[:, 0:PAIR_W], uw[:, PAIR_W:2 * PAIR_W]
            s = s_sc[p]
            wq = jnp.concatenate([w, qp * eg], axis=0).astype(BF16)
            sq = _dot(wq, s.astype(BF16))
            v_new = u - sq[0:CHUNK]
            o = sq[CHUNK:2 * CHUNK] + _dot((qk * dincl).astype(BF16), _bd(v_new.astype(BF16), bdm))
            o_ref[0, rows, lanes] = o
            g_last = gam[last:last + 1, :]
            kd = kp * jnp.exp(g_last - gam)
            s_sc[p] = s * jnp.exp(g_last) + _dot(kd.T.astype(BF16), v_new.astype(BF16)) * bdm
        return carry

    lax.fori_loop(0, n_chunks, chunk_body, 0)

    @pl.when(step == n_tiles - 1)
    def _():
        sfin_ref[0] = s_sc[...]


def _gdn_scan(qkv, ab, conv_w, a_log_pad, dt_bias_pad, s0, reverse):
    b, t, _ = qkv.shape
    tt = min(GDN_TT, t)
    n_tiles = t // tt
    hb = tt // 8
    n_hb = t // 8
    tri, cm = _gdn_consts(reverse)
    e_lg, e_bt = _gdn_expand(reverse)
    tile_of = (lambda i: n_tiles - 1 - i) if reverse else (lambda i: i)
    const = lambda shape: pl.BlockSpec(shape, lambda bi, i: (0,) * len(shape))
    o, s_fin = pl.pallas_call(
        functools.partial(_gdn_kernel, reverse, n_tiles),
        grid=(b, n_tiles),
        in_specs=[pl.BlockSpec((1, tt, 3 * GROUP_W), lambda bi, i: (bi, tile_of(i), 0)),
                  pl.BlockSpec((1, 8, 3 * GROUP_W), lambda bi, i: (bi, jnp.maximum(tile_of(i) * hb - 1, 0), 0)),
                  pl.BlockSpec((1, 8, 3 * GROUP_W), lambda bi, i: (bi, jnp.minimum((tile_of(i) + 1) * hb, n_hb - 1), 0)),
                  pl.BlockSpec((1, tt, AB_PAD), lambda bi, i: (bi, tile_of(i), 0)),
                  const((CONV_W, 3 * GROUP_W)),
                  const((1, AB_PAD)),
                  const((1, AB_PAD)),
                  pl.BlockSpec((1, N_PAIRS, PAIR_W, PAIR_W), lambda bi, i: (bi, 0, 0, 0)),
                  const((CHUNK, CHUNK)),
                  const(cm.shape),
                  const((PAIR_W, PAIR_W)),
                  const((AB_PAD, GROUP_W)),
                  const((AB_PAD, GROUP_W)),
                  const((GROUP_W, GROUP_W))],
        out_specs=[pl.BlockSpec((1, tt, GROUP_W), lambda bi, i: (bi, tile_of(i), 0)),
                   pl.BlockSpec((1, N_PAIRS, PAIR_W, PAIR_W), lambda bi, i: (bi, 0, 0, 0))],
        out_shape=[jax.ShapeDtypeStruct((b, t, GROUP_W), F32),
                   jax.ShapeDtypeStruct((b, N_PAIRS, PAIR_W, PAIR_W), F32)],
        scratch_shapes=[pltpu.VMEM((tt, GROUP_W), F32)] * 5 + [pltpu.VMEM((N_PAIRS, PAIR_W, PAIR_W), F32)],
        compiler_params=_cparams(("arbitrary", "arbitrary")),
        name="gdn_bwd" if reverse else "gdn_fwd",
    )(qkv, qkv, qkv, ab, conv_w, a_log_pad, dt_bias_pad, s0,
      jnp.asarray(tri, BF16), jnp.asarray(cm), jnp.asarray(_bd_mask()),
      jnp.asarray(e_lg, BF16), jnp.asarray(e_bt, BF16), jnp.asarray(_seg_ones(GROUP_W), BF16))
    return o, s_fin


def _hgrn_kernel(reverse, layer, n_steps, ph_ref, lbl_ref, s0_ref, sel_ref, hm_ref, bdm_ref,
                 o_ref, sfin_ref, q_sc, k_sc, lf_sc, ez_sc, s_sc):
    step = pl.program_id(1)
    tt = ph_ref.shape[1]
    n_chunks = tt // CHUNK
    n_lvl = len(_H_LEVELS)

    @pl.when(step == 0)
    def _():
        s_sc[...] = s0_ref[0]

    logits = lbl_ref[...]
    e = jnp.exp(logits - jnp.max(logits, axis=0, keepdims=True))
    prob = e / jnp.sum(e, axis=0, keepdims=True)
    lb = jnp.maximum(jnp.sum(prob[0:layer + 1], axis=0, keepdims=True) - prob[0:1], 0.0)

    z_off = 2 * GROUP_W if reverse else GROUP_W
    z = ph_ref[0, :, z_off:z_off + GROUP_W]
    q_sc[...] = _silu(ph_ref[0, :, 0:GROUP_W]) * (HEAD_DIM ** -0.5)
    sig = _sigmoid(z)
    lf_sc[...] = jnp.log(jnp.maximum(lb, LB_FLOOR) + (1.0 - lb) * sig)
    k_sc[...] = (1.0 - lb) * (1.0 - sig)

    bdm = bdm_ref[...]
    sel = sel_ref[...]
    last = 0 if reverse else CHUNK - 1
    q_row0 = n_lvl * CHUNK
    k_row0 = (n_lvl + 1) * CHUNK

    for ci in range(n_chunks):
        c = (n_chunks - 1 - ci) if reverse else ci
        rows = slice(c * CHUNK, (c + 1) * CHUNK)
        ez_sc[...] = jnp.exp(_dot_sel(sel, lf_sc[rows, :]))
        for p in range(N_PAIRS):
            lanes = slice(p * PAIR_W, (p + 1) * PAIR_W)
            qp, kp = q_sc[rows, lanes], k_sc[rows, lanes]
            vp = ph_ref[0, rows, 3 * GROUP_W + p * PAIR_W:3 * GROUP_W + (p + 1) * PAIR_W]
            a = _dot_nt(qp.astype(BF16), _bd(kp.astype(BF16), bdm)) * hm_ref[n_lvl]
            for li in range(n_lvl):
                ezl = ez_sc[li * CHUNK:(li + 1) * CHUNK, lanes]
                a = a + _dot_nt((qp * ezl).astype(BF16), _bd((kp * ezl).astype(BF16), bdm)) * hm_ref[li]
            qd = qp * ez_sc[q_row0:q_row0 + CHUNK, lanes]
            kd = kp * ez_sc[k_row0:k_row0 + CHUNK, lanes]
            st = s_sc[p]
            o = _dot(a.astype(BF16), _bd(vp.astype(BF16), bdm)) + _dot_nt(qd.astype(BF16), st.astype(BF16))
            o_ref[0, rows, lanes] = o
            f_last = ez_sc[q_row0 + last:q_row0 + last + 1, lanes]
            s_sc[p] = st * f_last + _dot(vp.T.astype(BF16), kd.astype(BF16)) * bdm

    @pl.when(step == n_steps - 1)
    def _():
        sfin_ref[0] = s_sc[...]


def _hgrn_scan(ph, lb_logits_dir, s0, layer, reverse, col_major):
    b, t, _ = ph.shape
    depth = lb_logits_dir.shape[0]
    sel, hm = _hgrn_consts(reverse)
    if col_major:
        rows = t // GRID_W
        tt = rows
        n_steps = GRID_W
        ph_v = ph.reshape(b, rows, GRID_W * 4 * GROUP_W)
        step_of = (lambda i: n_steps - 1 - i) if reverse else (lambda i: i)
        in_map = lambda bi, i: (bi, 0, step_of(i))
        out_arr = (b, rows, GRID_W * GROUP_W)
    else:
        tt = min(HGRN_TT, t)
        n_steps = t // tt
        ph_v = ph
        step_of = (lambda i: n_steps - 1 - i) if reverse else (lambda i: i)
        in_map = lambda bi, i: (bi, step_of(i), 0)
        out_arr = (b, t, GROUP_W)
    const = lambda shape: pl.BlockSpec(shape, lambda bi, i: (0,) * len(shape))
    o, s_fin = pl.pallas_call(
        functools.partial(_hgrn_kernel, reverse, layer, n_steps),
        grid=(b, n_steps),
        in_specs=[pl.BlockSpec((1, tt, 4 * GROUP_W), in_map),
                  const((depth, GROUP_W)),
                  pl.BlockSpec((1, N_PAIRS, PAIR_W, PAIR_W), lambda bi, i: (bi, 0, 0, 0)),
                  const(sel.shape),
                  const(hm.shape),
                  const((PAIR_W, PAIR_W))],
        out_specs=[pl.BlockSpec((1, tt, GROUP_W), in_map),
                   pl.BlockSpec((1, N_PAIRS, PAIR_W, PAIR_W), lambda bi, i: (bi, 0, 0, 0))],
        out_shape=[jax.ShapeDtypeStruct(out_arr, F32),
                   jax.ShapeDtypeStruct((b, N_PAIRS, PAIR_W, PAIR_W), F32)],
        scratch_shapes=[pltpu.VMEM((tt, GROUP_W), F32)] * 3
                       + [pltpu.VMEM((sel.shape[0], GROUP_W), F32),
                          pltpu.VMEM((N_PAIRS, PAIR_W, PAIR_W), F32)],
        compiler_params=_cparams(("arbitrary", "arbitrary")),
        name="hgrn_bwd" if reverse else "hgrn_fwd",
    )(ph_v, lb_logits_dir, s0, jnp.asarray(sel, BF16), jnp.asarray(hm), jnp.asarray(_bd_mask()))
    return o.reshape(b, t, GROUP_W), s_fin


def _mlp_kernel(final, x_ref, oaf_ref, oab_ref, obf_ref, obb_ref, ga_ref, gb_ref, mod_ref,
                gg_ref, hg_ref, n2g_ref, fg_ref, ones_ref, wo_ref, w1_ref, w2_ref, o_ref):
    ones = ones_ref[...]
    oa = oaf_ref[0] + oab_ref[0]
    ob = obf_ref[0] + obb_ref[0]
    inv_d = 1.0 / HEAD_DIM
    ya = oa * lax.rsqrt(_dot((oa * oa).astype(BF16), ones) * inv_d + EPS) * gg_ref[...] * _silu(ga_ref[0])
    yb = ob * lax.rsqrt(_dot((ob * ob).astype(BF16), ones) * inv_d + EPS) * hg_ref[...] * _sigmoid(gb_ref[0])
    y = _dot(ya.astype(BF16), wo_ref[0:GROUP_W, :]) + _dot(yb.astype(BF16), wo_ref[GROUP_W:2 * GROUP_W, :])
    x1 = x_ref[0] + mod_ref[0, 2:3, :] * y
    h = _norm_mod(x1, n2g_ref[...], mod_ref[0, 3:4, :], mod_ref[0, 4:5, :]).astype(BF16)
    hid = jnp.maximum(_dot(h, w1_ref[...]), 0.0)
    x2 = x1 + mod_ref[0, 5:6, :] * _dot((hid * hid).astype(BF16), w2_ref[...])
    if final:
        x2 = x2 * lax.rsqrt(jnp.mean(x2 * x2, axis=-1, keepdims=True) + EPS) * fg_ref[...]
    o_ref[0] = x2


def _out_mlp(x, oa_f, oa_b, ob_f, ob_b, ga, gb, mod, gdn_g, hgrn_g, n2g, final_g, w_out, w1, w2,
             shared_mod, final):
    b, t, d = x.shape
    tm = min(MLP_TM, t)
    mod_map = (lambda bi, i: (0, 0, 0)) if shared_mod else (lambda bi, i: (bi, 0, 0))
    tok = lambda w: pl.BlockSpec((1, tm, w), lambda bi, i: (bi, i, 0))
    const = lambda shape: pl.BlockSpec(shape, lambda bi, i: (0,) * len(shape))
    return pl.pallas_call(
        functools.partial(_mlp_kernel, final),
        grid=(b, t // tm),
        in_specs=[tok(d)] + [tok(GROUP_W)] * 6
                 + [pl.BlockSpec((1, N_MOD, d), mod_map),
                    const((1, GROUP_W)), const((1, GROUP_W)), const((1, d)), const((1, d)),
                    const((GROUP_W, GROUP_W)), const(w_out.shape), const(w1.shape), const(w2.shape)],
        out_specs=tok(d),
        out_shape=jax.ShapeDtypeStruct((b, t, d), F32),
        compiler_params=_cparams(("arbitrary", "arbitrary")),
        name="out_mlp",
    )(x, oa_f, oa_b, ob_f, ob_b, ga, gb, mod, gdn_g, hgrn_g, n2g, final_g,
      jnp.asarray(_seg_ones(GROUP_W), BF16), w_out, w1, w2)


def _split_w_in(w):
    g = GROUP_W
    qkv, ga, ab = w[:, 0:3 * g], w[:, 3 * g:4 * g], w[:, 4 * g:4 * g + 4 * N_HEADS]
    rest = w[:, 4 * g + 4 * N_HEADS:]
    ph, gb = rest[:, 0:4 * g], rest[:, 4 * g:5 * g]
    ab = jnp.pad(ab, ((0, 0), (0, AB_PAD - 4 * N_HEADS)))
    return jnp.concatenate([qkv, ga, ph, gb, ab], axis=1).astype(BF16)


def _pad_lanes(v):
    flat = v.reshape(1, -1)
    return jnp.pad(flat, ((0, 0), (0, AB_PAD - flat.shape[1])))


def kernel(x, c, ctx, c_ctx, w_mod, b_mod, norm1_g, norm2_g, w_in, conv_w, a_log, dt_bias,
           gdn_norm_g, hgrn_norm_g, lb_logits, w_out, w_mlp1, w_mlp2, final_g):
    depth = w_mod.shape[0]
    b, _, d = x.shape
    cvec = jnp.concatenate([c, c_ctx[None, :], jnp.zeros((8 - b - 1, d), F32)], axis=0)
    mod = _modulation(cvec, w_mod, b_mod).reshape(depth, 8, N_MOD, d)
    zero_state = jnp.zeros((b, N_PAIRS, PAIR_W, PAIR_W), F32)
    fg = final_g.reshape(1, d)

    x_lat, x_ctx = x, ctx
    for l in range(depth):
        need_ctx = l < depth - 1
        mod_lat, mod_ctx = mod[l, 0:b], mod[l, b:b + 1]
        w_cat = _split_w_in(w_in[l])
        n1g = norm1_g[l].reshape(1, d)
        a_pad, dt_pad = _pad_lanes(a_log[l]), _pad_lanes(dt_bias[l])
        gg = jnp.tile(gdn_norm_g[l], N_HEADS).reshape(1, GROUP_W)
        hg = jnp.tile(hgrn_norm_g[l], N_HEADS).reshape(1, GROUP_W)
        wo, w1, w2 = w_out[l].astype(BF16), w_mlp1[l].astype(BF16), w_mlp2[l].astype(BF16)

        pc = _projection(x_ctx, mod_ctx, n1g, w_cat, True)
        pl_ = _projection(x_lat, mod_lat, n1g, w_cat, False)

        outs_c, outs_l = [], []
        for rev in (False, True):
            oc, sc = _gdn_scan(pc[0], pc[4], conv_w[l], a_pad, dt_pad, zero_state, rev)
            ol, _ = _gdn_scan(pl_[0], pl_[4], conv_w[l], a_pad, dt_pad, sc, rev)
            outs_c.append(oc)
            outs_l.append(ol)
        for rev in (False, True):
            lbd = lb_logits[:, 1 if rev else 0, :]
            oc, sc = _hgrn_scan(pc[2], lbd, zero_state, l, rev, False)
            ol, _ = _hgrn_scan(pl_[2], lbd, sc, l, rev, True)
            outs_c.append(oc)
            outs_l.append(ol)

        n2g = norm2_g[l].reshape(1, d)
        x_lat = _out_mlp(x_lat, *outs_l, pl_[1], pl_[3], mod_lat, gg, hg, n2g, fg, wo, w1, w2,
                         False, not need_ctx)
        if need_ctx:
            x_ctx = _out_mlp(x_ctx, *outs_c, pc[1], pc[3], mod_ctx, gg, hg, n2g, fg, wo, w1, w2,
                             True, False)
    return x_lat
```

```python
import functools

import numpy as np
import jax
import jax.numpy as jnp
from jax import lax
from jax.experimental import pallas as pl
from jax.experimental.pallas import tpu as pltpu

F32 = jnp.float32
BF16 = jnp.bfloat16

HEAD_DIM = 64
N_HEADS = 8
GROUP_W = N_HEADS * HEAD_DIM
N_PAIRS = N_HEADS // 2
PAIR_W = 2 * HEAD_DIM
CHUNK = 64
GRID_W = 64
CONV_W = 3
N_MOD = 6
EPS = 1e-6
LB_FLOOR = 1e-30
AB_PAD = 128
VMEM_LIMIT = 56 * 1024 * 1024

PROJ_TM = 256
MLP_TM = 256
GDN_TT = 256
HGRN_TT = 128


def _dot(a, b):
    return jnp.dot(a, b, preferred_element_type=F32)


def _dot_nt(a, b):
    return lax.dot_general(a, b, (((1,), (1,)), ((), ())), preferred_element_type=F32)


def _dot_sel(sel, x):
    hi = x.astype(BF16)
    lo = (x - hi.astype(F32)).astype(BF16)
    return _dot(sel, hi) + _dot(sel, lo)


def _dot_sel_rhs(x, sel):
    hi = x.astype(BF16)
    lo = (x - hi.astype(F32)).astype(BF16)
    return _dot(hi, sel) + _dot(lo, sel)


def _bd(x, bdmask):
    return jnp.concatenate([x, x], axis=0) * bdmask.astype(x.dtype)


def _sigmoid(x):
    return 1.0 / (1.0 + jnp.exp(-x))


def _silu(x):
    return x * _sigmoid(x)


def _softplus(x):
    return jnp.maximum(x, 0.0) + jnp.log1p(jnp.exp(-jnp.abs(x)))


def _cparams(sem):
    return pltpu.CompilerParams(dimension_semantics=sem, vmem_limit_bytes=VMEM_LIMIT)


def _packed_ij():
    i = np.arange(CHUNK)[:, None]
    j = (np.arange(PAIR_W) % HEAD_DIM)[None, :]
    return i, j


def _bd_mask():
    r = np.arange(PAIR_W)[:, None] // HEAD_DIM
    c = np.arange(PAIR_W)[None, :] // HEAD_DIM
    return (r == c).astype(np.float32)


def _seg_ones(width):
    r = np.arange(width)[:, None] // HEAD_DIM
    c = np.arange(width)[None, :] // HEAD_DIM
    return (r == c).astype(np.float32)


_G_SUP, _G_INCL, _G_STRICT, _G_EYE, _G_B8, _G_C16, _G_C32, _G_C64 = range(8)


def _gdn_consts(reverse):
    i, j = _packed_ij()
    t = np.arange(CHUNK)
    if reverse:
        tri = (t[None, :] >= t[:, None])
        sup = i < j
        incl = j >= i
        strict = j > i
    else:
        tri = (t[None, :] <= t[:, None])
        sup = i > j
        incl = j <= i
        strict = j < i
    eye = i == j
    b8 = (i // 8) == (j // 8)
    c16 = ((i // 16) == (j // 16)) & ~b8
    c32 = ((i // 32) == (j // 32)) & ((i // 16) != (j // 16))
    c64 = (i // 32) != (j // 32)
    stack = np.stack([sup, incl, strict, eye, b8, c16, c32, c64]).astype(np.float32)
    return tri.astype(np.float32), stack


def _gdn_expand(reverse):
    d = 1 if reverse else 0
    col = np.arange(AB_PAD)[:, None]
    head = (np.arange(GROUP_W) // HEAD_DIM)[None, :]
    e_lg = (col == d * N_HEADS + head)
    e_bt = (col == 2 * N_HEADS + d * N_HEADS + head)
    return e_lg.astype(np.float32), e_bt.astype(np.float32)


_H_LEVELS = (1, 2, 4, 8, 16, 32)


def _hgrn_consts(reverse):
    r = np.arange(CHUNK)[:, None]
    t = np.arange(CHUNK)[None, :]
    i, j = _packed_ij()
    sels, masks = [], []
    for s in _H_LEVELS:
        start = (r // s) * s
        end = start + s - 1
        odd = ((r // s) % 2) == 1
        if reverse:
            sel = np.where(~odd, (t >= r) & (t <= end), (t >= start) & (t < r))
            m = ((i // (2 * s)) == (j // (2 * s))) & (((i // s) % 2) == 0) & (((j // s) % 2) == 1)
        else:
            sel = np.where(odd, (t >= start) & (t <= r), (t > r) & (t <= end))
            m = ((i // (2 * s)) == (j // (2 * s))) & (((i // s) % 2) == 1) & (((j // s) % 2) == 0)
        sels.append(sel)
        masks.append(m)
    if reverse:
        sels.append(t >= r)
        sels.append(t < r)
    else:
        sels.append(t <= r)
        sels.append(t > r)
    masks.append(i == j)
    sel = np.concatenate(sels, axis=0).astype(np.float32)
    return sel, np.stack(masks).astype(np.float32)


def _mod_kernel(c_ref, w_ref, b_ref, o_ref):
    sc = _silu(c_ref[...]).astype(BF16)
    o_ref[0] = _dot(sc, w_ref[0].astype(BF16)) + b_ref[0]


def _modulation(cvec, w_mod, b_mod):
    depth, d, n = w_mod.shape
    tn = 1536
    return pl.pallas_call(
        _mod_kernel,
        grid=(depth, n // tn),
        in_specs=[pl.BlockSpec((8, d), lambda l, j: (0, 0)),
                  pl.BlockSpec((1, d, tn), lambda l, j: (l, 0, j)),
                  pl.BlockSpec((1, 1, tn), lambda l, j: (l, 0, j))],
        out_specs=pl.BlockSpec((1, 8, tn), lambda l, j: (l, 0, j)),
        out_shape=jax.ShapeDtypeStruct((depth, 8, n), F32),
        compiler_params=_cparams(("arbitrary", "arbitrary")),
        name="modulation",
    )(cvec, w_mod, b_mod.reshape(depth, 1, n))


_PROJ_WIDTHS = (3 * GROUP_W, GROUP_W, 4 * GROUP_W, GROUP_W, AB_PAD)


def _norm_mod(x, g, shift, scale):
    y = x * lax.rsqrt(jnp.mean(x * x, axis=-1, keepdims=True) + EPS) * g
    return y * (1.0 + scale) + shift


def _proj_kernel(x_ref, mod_ref, g_ref, w_ref, qkv_ref, ga_ref, ph_ref, gb_ref, ab_ref):
    h = _norm_mod(x_ref[0], g_ref[...], mod_ref[0, 0:1, :], mod_ref[0, 1:2, :]).astype(BF16)
    off = 0
    for ref, w in zip((qkv_ref, ga_ref, ph_ref, gb_ref, ab_ref), _PROJ_WIDTHS):
        ref[0] = _dot(h, w_ref[:, off:off + w])
        off += w


def _projection(x, mod, g, w_cat, shared_mod):
    b, t, d = x.shape
    tm = min(PROJ_TM, t)
    mod_map = (lambda bi, i: (0, 0, 0)) if shared_mod else (lambda bi, i: (bi, 0, 0))
    tok = lambda w: pl.BlockSpec((1, tm, w), lambda bi, i: (bi, i, 0))
    return pl.pallas_call(
        _proj_kernel,
        grid=(b, t // tm),
        in_specs=[tok(d),
                  pl.BlockSpec((1, N_MOD, d), mod_map),
                  pl.BlockSpec((1, d), lambda bi, i: (0, 0)),
                  pl.BlockSpec(w_cat.shape, lambda bi, i: (0, 0))],
        out_specs=[tok(w) for w in _PROJ_WIDTHS],
        out_shape=[jax.ShapeDtypeStruct((b, t, w), F32) for w in _PROJ_WIDTHS],
        compiler_params=_cparams(("arbitrary", "arbitrary")),
        name="projection",
    )(x, mod, g, w_cat)


def _gdn_prep(tile, n_tiles, qkv_ref, prev_ref, next_ref, ab_ref, convw_ref, alog_ref, dtb_ref,
              elg, ebt, ones, q_sc, k_sc, v_sc, lg_sc, bt_sc):
    tt = qkv_ref.shape[1]
    x = qkv_ref[0]
    row = lax.broadcasted_iota(jnp.int32, (tt, 1), 0)
    pv = jnp.where(tile > 0, prev_ref[0, 7:8, :], 0.0)
    nx = jnp.where(tile < n_tiles - 1, next_ref[0, 0:1, :], 0.0)
    x_prev = jnp.where(row == 0, pv, pltpu.roll(x, 1, axis=0))
    x_next = jnp.where(row == tt - 1, nx, pltpu.roll(x, tt - 1, axis=0))
    y = _silu(convw_ref[0:1, :] * x_prev + convw_ref[1:2, :] * x + convw_ref[2:3, :] * x_next)
    q = y[:, 0:GROUP_W]
    k = y[:, GROUP_W:2 * GROUP_W]
    q_sc[...] = q * lax.rsqrt(_dot((q * q).astype(BF16), ones) + EPS) * (HEAD_DIM ** -0.5)
    k_sc[...] = k * lax.rsqrt(_dot((k * k).astype(BF16), ones) + EPS)
    v_sc[...] = y[:, 2 * GROUP_W:3 * GROUP_W]
    ab = ab_ref[0]
    lg_c = -jnp.exp(alog_ref[...]) * _softplus(ab + dtb_ref[...])
    lg_sc[...] = _dot_sel_rhs(lg_c, elg)
    bt_sc[...] = _dot_sel_rhs(_sigmoid(ab), ebt)


def _gdn_chunk_terms(reverse, items, q_sc, k_sc, v_sc, lg_sc, bt_sc, tri, cm, bdm):
    bdm16 = bdm.astype(BF16)
    ld = lambda ref: [ref[c * CHUNK:(c + 1) * CHUNK, p * PAIR_W:(p + 1) * PAIR_W] for c, p in items]
    qp, kp, vp, lg, bt = ld(q_sc), ld(k_sc), ld(v_sc), ld(lg_sc), ld(bt_sc)
    bd16 = lambda xs: [_bd(x.astype(BF16), bdm16) for x in xs]
    pmul = lambda xs, ys: [_dot(x.astype(BF16), y) for x, y in zip(xs, bd16(ys))]

    gam = [_dot_sel(tri, x) for x in lg]
    dincl = [jnp.exp(_dot_sel(tri, x * cm(_G_SUP))) * cm(_G_INCL) for x in lg]
    kb = [x.astype(BF16) for x in kp]
    k2 = [_bd(x, bdm16) for x in kb]
    kk = [_dot_nt(a, b) for a, b in zip(kb, k2)]
    qk = [_dot_nt(a.astype(BF16), b) for a, b in zip(qp, k2)]
    m = [a * b * d * cm(_G_STRICT) for a, b, d in zip(kk, bt, dincl)]
    n1 = [-(x * cm(_G_B8)) for x in m]
    n2 = pmul(n1, n1)
    n4 = pmul(n2, n2)
    t_inv = [cm(_G_EYE) + x for x in n1]
    t_inv = [t + d for t, d in zip(t_inv, pmul(t_inv, n2))]
    t_inv = [t + d for t, d in zip(t_inv, pmul(t_inv, n4))]
    for lvl in (_G_C16, _G_C32, _G_C64):
        left = pmul(t_inv, [x * cm(lvl) for x in m])
        t_inv = [t - d for t, d in zip(t_inv, pmul(left, t_inv))]
    eg = [jnp.exp(x) for x in gam]
    rhs = [jnp.concatenate([_bd((v * b).astype(BF16), bdm16), _bd((k * b * e).astype(BF16), bdm16)], axis=1)
           for v, k, b, e in zip(vp, kp, bt, eg)]
    uw = [_dot(t.astype(BF16), r) for t, r in zip(t_inv, rhs)]
    last = 0 if reverse else CHUNK - 1
    g_last = [x[last:last + 1, :] for x in gam]
    return dict(
        u=[x[:, 0:PAIR_W] for x in uw],
        wq=[jnp.concatenate([x[:, PAIR_W:2 * PAIR_W], q * e], axis=0).astype(BF16) for x, q, e in zip(uw, qp, eg)],
        qkd=[(a * d).astype(BF16) for a, d in zip(qk, dincl)],
        kdt=[(k * jnp.exp(g - x)).T.astype(BF16) for k, g, x in zip(kp, g_last, gam)],
        decay=[jnp.exp(g) for g in g_last])


def _gdn_kernel(n_tiles, *refs):
    (qkv_f, prev_f, next_f, ab_f, qkv_b, prev_b, next_b, ab_b, convw_ref, alog_ref, dtb_ref,
     s0_ref, tri_ref, cm_ref, bdm_ref, elg_ref, ebt_ref, ones_ref,
     of_ref, ob_ref, sfin_ref, q_sc, k_sc, v_sc, lg_sc, bt_sc, s_sc) = refs
    step = pl.program_id(1)
    tt = qkv_f.shape[1]
    n_chunks = tt // CHUNK

    @pl.when(step == 0)
    def _():
        s_sc[...] = s0_ref[0]

    bdm = bdm_ref[...]
    bdm16 = bdm.astype(BF16)
    ones = ones_ref[...]
    tok = ((qkv_f, prev_f, next_f, ab_f), (qkv_b, prev_b, next_b, ab_b))
    o_refs = (of_ref, ob_ref)
    terms = []
    for d, reverse in enumerate((False, True)):
        tile = (n_tiles - 1 - step) if reverse else step
        sc = (q_sc.at[d], k_sc.at[d], v_sc.at[d], lg_sc.at[d], bt_sc.at[d])
        _gdn_prep(tile, n_tiles, *tok[d], convw_ref, alog_ref, dtb_ref, elg_ref[d], ebt_ref[d], ones, *sc)
        items = [(c, p) for c in range(n_chunks) for p in range(N_PAIRS)]
        cm = functools.partial(lambda dd, idx: cm_ref[dd, idx], d)
        terms.append(_gdn_chunk_terms(reverse, items, *sc, tri_ref[d], cm, bdm))

    s = [[s_sc[d, p] for p in range(N_PAIRS)] for d in range(2)]
    for ci in range(n_chunks):
        lanes = [(d, p, ((n_chunks - 1 - ci) if d else ci) * N_PAIRS + p) for d in range(2) for p in range(N_PAIRS)]
        sq = [_dot(terms[d]["wq"][i], s[d][p].astype(BF16)) for d, p, i in lanes]
        v_new = [terms[d]["u"][i] - x[0:CHUNK] for (d, p, i), x in zip(lanes, sq)]
        v16 = [x.astype(BF16) for x in v_new]
        o = [x[CHUNK:2 * CHUNK] + _dot(terms[d]["qkd"][i], _bd(v, bdm16)) for (d, p, i), x, v in zip(lanes, sq, v16)]
        upd = [_dot(terms[d]["kdt"][i], v) * bdm for (d, p, i), v in zip(lanes, v16)]
        for (d, p, i), x, y in zip(lanes, o, upd):
            c = i // N_PAIRS
            o_refs[d][0, c * CHUNK:(c + 1) * CHUNK, p * PAIR_W:(p + 1) * PAIR_W] = x
            s[d][p] = s[d][p] * terms[d]["decay"][i] + y
    for d in range(2):
        for p in range(N_PAIRS):
            s_sc[d, p] = s[d][p]

    @pl.when(step == n_tiles - 1)
    def _():
        sfin_ref[0] = s_sc[...]


def _gdn_scan(qkv, ab, conv_w, a_log_pad, dt_bias_pad, s0):
    b, t, _ = qkv.shape
    tt = min(GDN_TT, t)
    n_tiles = t // tt
    hb = tt // 8
    n_hb = t // 8
    consts = [_gdn_consts(rev) for rev in (False, True)]
    tri = np.stack([c[0] for c in consts])
    cm = np.stack([c[1] for c in consts])
    expand = [_gdn_expand(rev) for rev in (False, True)]
    e_lg = np.stack([e[0] for e in expand])
    e_bt = np.stack([e[1] for e in expand])
    const = lambda shape: pl.BlockSpec(shape, lambda bi, i: (0,) * len(shape))
    state_spec = pl.BlockSpec((1, 2, N_PAIRS, PAIR_W, PAIR_W), lambda bi, i: (bi, 0, 0, 0, 0))

    def tok_specs(tile_of):
        return [pl.BlockSpec((1, tt, 3 * GROUP_W), lambda bi, i: (bi, tile_of(i), 0)),
                pl.BlockSpec((1, 8, 3 * GROUP_W), lambda bi, i: (bi, jnp.maximum(tile_of(i) * hb - 1, 0), 0)),
                pl.BlockSpec((1, 8, 3 * GROUP_W), lambda bi, i: (bi, jnp.minimum((tile_of(i) + 1) * hb, n_hb - 1), 0)),
                pl.BlockSpec((1, tt, AB_PAD), lambda bi, i: (bi, tile_of(i), 0))]

    fwd_tile = lambda i: i
    bwd_tile = lambda i: n_tiles - 1 - i
    o_f, o_b, s_fin = pl.pallas_call(
        functools.partial(_gdn_kernel, n_tiles),
        grid=(b, n_tiles),
        in_specs=tok_specs(fwd_tile) + tok_specs(bwd_tile)
                 + [const((CONV_W, 3 * GROUP_W)), const((1, AB_PAD)), const((1, AB_PAD)), state_spec,
                    const(tri.shape), const(cm.shape), const((PAIR_W, PAIR_W)),
                    const(e_lg.shape), const(e_bt.shape), const((GROUP_W, GROUP_W))],
        out_specs=[pl.BlockSpec((1, tt, GROUP_W), lambda bi, i: (bi, fwd_tile(i), 0)),
                   pl.BlockSpec((1, tt, GROUP_W), lambda bi, i: (bi, bwd_tile(i), 0)),
                   state_spec],
        out_shape=[jax.ShapeDtypeStruct((b, t, GROUP_W), F32),
                   jax.ShapeDtypeStruct((b, t, GROUP_W), F32),
                   jax.ShapeDtypeStruct((b, 2, N_PAIRS, PAIR_W, PAIR_W), F32)],
        scratch_shapes=[pltpu.VMEM((2, tt, GROUP_W), F32)] * 5 + [pltpu.VMEM((2, N_PAIRS, PAIR_W, PAIR_W), F32)],
        compiler_params=_cparams(("arbitrary", "arbitrary")),
        name="gdn_scan",
    )(qkv, qkv, qkv, ab, qkv, qkv, qkv, ab, conv_w, a_log_pad, dt_bias_pad, s0,
      jnp.asarray(tri, BF16), jnp.asarray(cm), jnp.asarray(_bd_mask()),
      jnp.asarray(e_lg, BF16), jnp.asarray(e_bt, BF16), jnp.asarray(_seg_ones(GROUP_W), BF16))
    return o_f, o_b, s_fin


def _hgrn_kernel(reverse, layer, n_steps, ph_ref, lbl_ref, s0_ref, sel_ref, hm_ref, bdm_ref,
                 o_ref, sfin_ref, q_sc, k_sc, lf_sc, ez_sc, s_sc):
    step = pl.program_id(1)
    tt = ph_ref.shape[1]
    n_chunks = tt // CHUNK
    n_lvl = len(_H_LEVELS)

    @pl.when(step == 0)
    def _():
        s_sc[...] = s0_ref[0]

    logits = lbl_ref[...]
    e = jnp.exp(logits - jnp.max(logits, axis=0, keepdims=True))
    prob = e / jnp.sum(e, axis=0, keepdims=True)
    lb = jnp.maximum(jnp.sum(prob[0:layer + 1], axis=0, keepdims=True) - prob[0:1], 0.0)

    z_off = 2 * GROUP_W if reverse else GROUP_W
    z = ph_ref[0, :, z_off:z_off + GROUP_W]
    q_sc[...] = _silu(ph_ref[0, :, 0:GROUP_W]) * (HEAD_DIM ** -0.5)
    sig = _sigmoid(z)
    lf_sc[...] = jnp.log(jnp.maximum(lb, LB_FLOOR) + (1.0 - lb) * sig)
    k_sc[...] = (1.0 - lb) * (1.0 - sig)

    bdm = bdm_ref[...]
    sel = sel_ref[...]
    last = 0 if reverse else CHUNK - 1
    q_row0 = n_lvl * CHUNK
    k_row0 = (n_lvl + 1) * CHUNK

    for ci in range(n_chunks):
        c = (n_chunks - 1 - ci) if reverse else ci
        rows = slice(c * CHUNK, (c + 1) * CHUNK)
        ez_sc[...] = jnp.exp(_dot_sel(sel, lf_sc[rows, :]))
        for p in range(N_PAIRS):
            lanes = slice(p * PAIR_W, (p + 1) * PAIR_W)
            qp, kp = q_sc[rows, lanes], k_sc[rows, lanes]
            vp = ph_ref[0, rows, 3 * GROUP_W + p * PAIR_W:3 * GROUP_W + (p + 1) * PAIR_W]
            a = _dot_nt(qp.astype(BF16), _bd(kp.astype(BF16), bdm)) * hm_ref[n_lvl]
            for li in range(n_lvl):
                ezl = ez_sc[li * CHUNK:(li + 1) * CHUNK, lanes]
                a = a + _dot_nt((qp * ezl).astype(BF16), _bd((kp * ezl).astype(BF16), bdm)) * hm_ref[li]
            qd = qp * ez_sc[q_row0:q_row0 + CHUNK, lanes]
            kd = kp * ez_sc[k_row0:k_row0 + CHUNK, lanes]
            st = s_sc[p]
            o = _dot(a.astype(BF16), _bd(vp.astype(BF16), bdm)) + _dot_nt(qd.astype(BF16), st.astype(BF16))
            o_ref[0, rows, lanes] = o
            f_last = ez_sc[q_row0 + last:q_row0 + last + 1, lanes]
            s_sc[p] = st * f_last + _dot(vp.T.astype(BF16), kd.astype(BF16)) * bdm

    @pl.when(step == n_steps - 1)
    def _():
        sfin_ref[0] = s_sc[...]


def _hgrn_scan(ph, lb_logits_dir, s0, layer, reverse, col_major):
    b, t, _ = ph.shape
    depth = lb_logits_dir.shape[0]
    sel, hm = _hgrn_consts(reverse)
    if col_major:
        rows = t // GRID_W
        tt = rows
        n_steps = GRID_W
        ph_v = ph.reshape(b, rows, GRID_W * 4 * GROUP_W)
        step_of = (lambda i: n_steps - 1 - i) if reverse else (lambda i: i)
        in_map = lambda bi, i: (bi, 0, step_of(i))
        out_arr = (b, rows, GRID_W * GROUP_W)
    else:
        tt = min(HGRN_TT, t)
        n_steps = t // tt
        ph_v = ph
        step_of = (lambda i: n_steps - 1 - i) if reverse else (lambda i: i)
        in_map = lambda bi, i: (bi, step_of(i), 0)
        out_arr = (b, t, GROUP_W)
    const = lambda shape: pl.BlockSpec(shape, lambda bi, i: (0,) * len(shape))
    o, s_fin = pl.pallas_call(
        functools.partial(_hgrn_kernel, reverse, layer, n_steps),
        grid=(b, n_steps),
        in_specs=[pl.BlockSpec((1, tt, 4 * GROUP_W), in_map),
                  const((depth, GROUP_W)),
                  pl.BlockSpec((1, N_PAIRS, PAIR_W, PAIR_W), lambda bi, i: (bi, 0, 0, 0)),
                  const(sel.shape),
                  const(hm.shape),
                  const((PAIR_W, PAIR_W))],
        out_specs=[pl.BlockSpec((1, tt, GROUP_W), in_map),
                   pl.BlockSpec((1, N_PAIRS, PAIR_W, PAIR_W), lambda bi, i: (bi, 0, 0, 0))],
        out_shape=[jax.ShapeDtypeStruct(out_arr, F32),
                   jax.ShapeDtypeStruct((b, N_PAIRS, PAIR_W, PAIR_W), F32)],
        scratch_shapes=[pltpu.VMEM((tt, GROUP_W), F32)] * 3
                       + [pltpu.VMEM((sel.shape[0], GROUP_W), F32),
                          pltpu.VMEM((N_PAIRS, PAIR_W, PAIR_W), F32)],
        compiler_params=_cparams(("arbitrary", "arbitrary")),
        name="hgrn_bwd" if reverse else "hgrn_fwd",
    )(ph_v, lb_logits_dir, s0, jnp.asarray(sel, BF16), jnp.asarray(hm), jnp.asarray(_bd_mask()))
    return o.reshape(b, t, GROUP_W), s_fin


def _mlp_kernel(final, x_ref, oaf_ref, oab_ref, obf_ref, obb_ref, ga_ref, gb_ref, mod_ref,
                gg_ref, hg_ref, n2g_ref, fg_ref, ones_ref, wo_ref, w1_ref, w2_ref, o_ref):
    ones = ones_ref[...]
    oa = oaf_ref[0] + oab_ref[0]
    ob = obf_ref[0] + obb_ref[0]
    inv_d = 1.0 / HEAD_DIM
    ya = oa * lax.rsqrt(_dot((oa * oa).astype(BF16), ones) * inv_d + EPS) * gg_ref[...] * _silu(ga_ref[0])
    yb = ob * lax.rsqrt(_dot((ob * ob).astype(BF16), ones) * inv_d + EPS) * hg_ref[...] * _sigmoid(gb_ref[0])
    y = _dot(ya.astype(BF16), wo_ref[0:GROUP_W, :]) + _dot(yb.astype(BF16), wo_ref[GROUP_W:2 * GROUP_W, :])
    x1 = x_ref[0] + mod_ref[0, 2:3, :] * y
    h = _norm_mod(x1, n2g_ref[...], mod_ref[0, 3:4, :], mod_ref[0, 4:5, :]).astype(BF16)
    hid = jnp.maximum(_dot(h, w1_ref[...]), 0.0)
    x2 = x1 + mod_ref[0, 5:6, :] * _dot((hid * hid).astype(BF16), w2_ref[...])
    if final:
        x2 = x2 * lax.rsqrt(jnp.mean(x2 * x2, axis=-1, keepdims=True) + EPS) * fg_ref[...]
    o_ref[0] = x2


def _out_mlp(x, oa_f, oa_b, ob_f, ob_b, ga, gb, mod, gdn_g, hgrn_g, n2g, final_g, w_out, w1, w2,
             shared_mod, final):
    b, t, d = x.shape
    tm = min(MLP_TM, t)
    mod_map = (lambda bi, i: (0, 0, 0)) if shared_mod else (lambda bi, i: (bi, 0, 0))
    tok = lambda w: pl.BlockSpec((1, tm, w), lambda bi, i: (bi, i, 0))
    const = lambda shape: pl.BlockSpec(shape, lambda bi, i: (0,) * len(shape))
    return pl.pallas_call(
        functools.partial(_mlp_kernel, final),
        grid=(b, t // tm),
        in_specs=[tok(d)] + [tok(GROUP_W)] * 6
                 + [pl.BlockSpec((1, N_MOD, d), mod_map),
                    const((1, GROUP_W)), const((1, GROUP_W)), const((1, d)), const((1, d)),
                    const((GROUP_W, GROUP_W)), const(w_out.shape), const(w1.shape), const(w2.shape)],
        out_specs=tok(d),
        out_shape=jax.ShapeDtypeStruct((b, t, d), F32),
        compiler_params=_cparams(("arbitrary", "arbitrary")),
        name="out_mlp",
    )(x, oa_f, oa_b, ob_f, ob_b, ga, gb, mod, gdn_g, hgrn_g, n2g, final_g,
      jnp.asarray(_seg_ones(GROUP_W), BF16), w_out, w1, w2)


def _split_w_in(w):
    g = GROUP_W
    qkv, ga, ab = w[:, 0:3 * g], w[:, 3 * g:4 * g], w[:, 4 * g:4 * g + 4 * N_HEADS]
    rest = w[:, 4 * g + 4 * N_HEADS:]
    ph, gb = rest[:, 0:4 * g], rest[:, 4 * g:5 * g]
    ab = jnp.pad(ab, ((0, 0), (0, AB_PAD - 4 * N_HEADS)))
    return jnp.concatenate([qkv, ga, ph, gb, ab], axis=1).astype(BF16)


def _pad_lanes(v):
    flat = v.reshape(1, -1)
    return jnp.pad(flat, ((0, 0), (0, AB_PAD - flat.shape[1])))


def kernel(x, c, ctx, c_ctx, w_mod, b_mod, norm1_g, norm2_g, w_in, conv_w, a_log, dt_bias,
           gdn_norm_g, hgrn_norm_g, lb_logits, w_out, w_mlp1, w_mlp2, final_g):
    depth = w_mod.shape[0]
    b, _, d = x.shape
    cvec = jnp.concatenate([c, c_ctx[None, :], jnp.zeros((8 - b - 1, d), F32)], axis=0)
    mod = _modulation(cvec, w_mod, b_mod).reshape(depth, 8, N_MOD, d)
    zero_state = jnp.zeros((b, N_PAIRS, PAIR_W, PAIR_W), F32)
    fg = final_g.reshape(1, d)

    x_lat, x_ctx = x, ctx
    for l in range(depth):
        need_ctx = l < depth - 1
        mod_lat, mod_ctx = mod[l, 0:b], mod[l, b:b + 1]
        w_cat = _split_w_in(w_in[l])
        n1g = norm1_g[l].reshape(1, d)
        a_pad, dt_pad = _pad_lanes(a_log[l]), _pad_lanes(dt_bias[l])
        gg = jnp.tile(gdn_norm_g[l], N_HEADS).reshape(1, GROUP_W)
        hg = jnp.tile(hgrn_norm_g[l], N_HEADS).reshape(1, GROUP_W)
        wo, w1, w2 = w_out[l].astype(BF16), w_mlp1[l].astype(BF16), w_mlp2[l].astype(BF16)

        pc = _projection(x_ctx, mod_ctx, n1g, w_cat, True)
        pl_ = _projection(x_lat, mod_lat, n1g, w_cat, False)

        oc_f, oc_b, sc = _gdn_scan(pc[0], pc[4], conv_w[l], a_pad, dt_pad, jnp.stack([zero_state] * 2, axis=1))
        ol_f, ol_b, _ = _gdn_scan(pl_[0], pl_[4], conv_w[l], a_pad, dt_pad, sc)
        outs_c, outs_l = [oc_f, oc_b], [ol_f, ol_b]
        for rev in (False, True):
            lbd = lb_logits[:, 1 if rev else 0, :]
            oc, sc = _hgrn_scan(pc[2], lbd, zero_state, l, rev, False)
            ol, _ = _hgrn_scan(pl_[2], lbd, sc, l, rev, True)
            outs_c.append(oc)
            outs_l.append(ol)

        n2g = norm2_g[l].reshape(1, d)
        x_lat = _out_mlp(x_lat, *outs_l, pl_[1], pl_[3], mod_lat, gg, hg, n2g, fg, wo, w1, w2,
                         False, not need_ctx)
        if need_ctx:
            x_ctx = _out_mlp(x_ctx, *outs_c, pc[1], pc[3], mod_ctx, gg, hg, n2g, fg, wo, w1, w2,
                             True, False)
    return x_lat
```

```python
import functools

import numpy as np
import jax
import jax.numpy as jnp
from jax import lax
from jax.experimental import pallas as pl
from jax.experimental.pallas import tpu as pltpu

F32 = jnp.float32
BF16 = jnp.bfloat16

HEAD_DIM = 64
N_HEADS = 8
GROUP_W = N_HEADS * HEAD_DIM
N_PAIRS = N_HEADS // 2
PAIR_W = 2 * HEAD_DIM
CHUNK = 64
GRID_W = 64
CONV_W = 3
N_MOD = 6
EPS = 1e-6
LB_FLOOR = 1e-30
AB_PAD = 128
VMEM_LIMIT = 56 * 1024 * 1024

PROJ_TM = 512
MLP_TM = 512
GDN_TT = 256
HGRN_TT = 128


def _dot(a, b):
    return jnp.dot(a, b, preferred_element_type=F32)


def _dot_nt(a, b):
    return lax.dot_general(a, b, (((1,), (1,)), ((), ())), preferred_element_type=F32)


def _dot_sel(sel, x):
    hi = x.astype(BF16)
    lo = (x - hi.astype(F32)).astype(BF16)
    return _dot(sel, hi) + _dot(sel, lo)


def _dot_sel_rhs(x, sel):
    hi = x.astype(BF16)
    lo = (x - hi.astype(F32)).astype(BF16)
    return _dot(hi, sel) + _dot(lo, sel)


def _bd(x, bdmask):
    return jnp.concatenate([x, x], axis=0) * bdmask.astype(x.dtype)


def _sigmoid(x):
    return 0.5 * jnp.tanh(0.5 * x) + 0.5


def _silu(x):
    return x * _sigmoid(x)


def _softplus(x):
    return jnp.maximum(x, 0.0) + jnp.log1p(jnp.exp(-jnp.abs(x)))


def _cparams(sem):
    return pltpu.CompilerParams(dimension_semantics=sem, vmem_limit_bytes=VMEM_LIMIT)


def _packed_ij():
    i = np.arange(CHUNK)[:, None]
    j = (np.arange(PAIR_W) % HEAD_DIM)[None, :]
    return i, j


def _bd_mask():
    r = np.arange(PAIR_W)[:, None] // HEAD_DIM
    c = np.arange(PAIR_W)[None, :] // HEAD_DIM
    return (r == c).astype(np.float32)


def _seg_ones(width):
    r = np.arange(width)[:, None] // HEAD_DIM
    c = np.arange(width)[None, :] // HEAD_DIM
    return (r == c).astype(np.float32)


_G_SUP, _G_INCL, _G_STRICT, _G_EYE, _G_B8, _G_C16, _G_C32, _G_C64 = range(8)


def _gdn_consts(reverse):
    i, j = _packed_ij()
    t = np.arange(CHUNK)
    if reverse:
        tri = (t[None, :] >= t[:, None])
        sup = i < j
        incl = j >= i
        strict = j > i
    else:
        tri = (t[None, :] <= t[:, None])
        sup = i > j
        incl = j <= i
        strict = j < i
    eye = i == j
    b8 = (i // 8) == (j // 8)
    c16 = ((i // 16) == (j // 16)) & ~b8
    c32 = ((i // 32) == (j // 32)) & ((i // 16) != (j // 16))
    c64 = (i // 32) != (j // 32)
    stack = np.stack([sup, incl, strict, eye, b8, c16, c32, c64]).astype(np.float32)
    return tri.astype(np.float32), stack


def _gdn_expand(reverse):
    d = 1 if reverse else 0
    col = np.arange(AB_PAD)[:, None]
    head = (np.arange(GROUP_W) // HEAD_DIM)[None, :]
    e_lg = (col == d * N_HEADS + head)
    e_bt = (col == 2 * N_HEADS + d * N_HEADS + head)
    return e_lg.astype(np.float32), e_bt.astype(np.float32)


_H_LEVELS = (1, 2, 4, 8, 16, 32)


def _hgrn_consts(reverse):
    r = np.arange(CHUNK)[:, None]
    t = np.arange(CHUNK)[None, :]
    i, j = _packed_ij()
    q_par, k_par = (0, 1) if reverse else (1, 0)
    masks = [((i // (2 * s)) == (j // (2 * s))) & (((i // s) % 2) == q_par) & (((j // s) % 2) == k_par)
             for s in _H_LEVELS]
    masks.append(i == j)
    tri = (t >= r) if reverse else (t <= r)
    return tri.astype(np.float32), np.stack(masks).astype(np.float32)


def _hgrn_exponents(reverse, b_ref, lf):
    w = lf.shape[1]
    rig = lax.broadcasted_iota(jnp.int32, (8, w), 0)
    grp = lambda g: b_ref[8 * g:8 * g + 8, :]
    row = lambda r: b_ref[r:r + 1, :]
    n_grp = CHUNK // 8
    q_par = 0 if reverse else 1
    sign = lambda s: jnp.where(((rig // s) % 2) == q_par, 1.0, -1.0)
    out = []
    r64 = lax.broadcasted_iota(jnp.int32, lf.shape, 0)
    out.append(jnp.where((r64 % 2) == q_par, lf, 0.0))
    bnd2 = (2, 6) if reverse else (1, 5)
    sg2, sg4 = sign(2), sign(4)
    out.append(jnp.concatenate(
        [(grp(g) - jnp.where(rig < 4, row(8 * g + bnd2[0]), row(8 * g + bnd2[1]))) * sg2 for g in range(n_grp)], axis=0))
    bnd4 = 4 if reverse else 3
    out.append(jnp.concatenate([(grp(g) - row(8 * g + bnd4)) * sg4 for g in range(n_grp)], axis=0))
    for s in (8, 16, 32):
        parts = []
        for g in range(n_grp):
            blk = (8 * g) // s
            bnd = 2 * s * (blk // 2) + (s if reverse else s - 1)
            parts.append(grp(g) - row(bnd) if (blk % 2) == q_par else row(bnd) - grp(g))
        out.append(jnp.concatenate(parts, axis=0))
    out.append(b_ref[...])
    out.append(row(0 if reverse else CHUNK - 1) - b_ref[...])
    return out


def _mod_kernel(c_ref, w_ref, b_ref, o_ref):
    sc = _silu(c_ref[...]).astype(BF16)
    o_ref[0] = _dot(sc, w_ref[0].astype(BF16)) + b_ref[0]


def _modulation(cvec, w_mod, b_mod):
    depth, d, n = w_mod.shape
    tn = 1536
    return pl.pallas_call(
        _mod_kernel,
        grid=(depth, n // tn),
        in_specs=[pl.BlockSpec((8, d), lambda l, j: (0, 0)),
                  pl.BlockSpec((1, d, tn), lambda l, j: (l, 0, j)),
                  pl.BlockSpec((1, 1, tn), lambda l, j: (l, 0, j))],
        out_specs=pl.BlockSpec((1, 8, tn), lambda l, j: (l, 0, j)),
        out_shape=jax.ShapeDtypeStruct((depth, 8, n), F32),
        compiler_params=_cparams(("arbitrary", "arbitrary")),
        name="modulation",
    )(cvec, w_mod, b_mod.reshape(depth, 1, n))


_PROJ_WIDTHS = (3 * GROUP_W, GROUP_W, AB_PAD, 4 * GROUP_W, GROUP_W)
_N_ROW_MAJOR = 3


def _norm_mod(x, g, shift, scale):
    y = x * lax.rsqrt(jnp.mean(x * x, axis=-1, keepdims=True) + EPS) * g
    return y * (1.0 + scale) + shift


def _grid_perm(tm):
    rows = tm // GRID_W
    n = np.arange(tm)
    src = (n % rows) * GRID_W + n // rows
    p = np.zeros((tm, tm), np.float32)
    p[n, src] = 1.0
    return p


def _proj_kernel(col_major, x_ref, mod_ref, g_ref, w_ref, perm_ref, *out_refs):
    h = _norm_mod(x_ref[0], g_ref[...], mod_ref[0, 0:1, :], mod_ref[0, 1:2, :]).astype(BF16)
    off = 0
    for n, (ref, w) in enumerate(zip(out_refs, _PROJ_WIDTHS)):
        if n == _N_ROW_MAJOR and col_major:
            h = _dot(perm_ref[...], h).astype(BF16)
        val = _dot(h, w_ref[:, off:off + w])
        ref[0] = val.reshape(ref.shape[1:])
        off += w


def _projection(x, mod, g, w_cat, shared_mod, col_major):
    b, t, d = x.shape
    tm = PROJ_TM if col_major else min(PROJ_TM, t)
    rows_t = tm // GRID_W
    mod_map = (lambda bi, i: (0, 0, 0)) if shared_mod else (lambda bi, i: (bi, 0, 0))
    tok = lambda w: pl.BlockSpec((1, tm, w), lambda bi, i: (bi, i, 0))
    const = lambda shape: pl.BlockSpec(shape, lambda bi, i: (0,) * len(shape), pipeline_mode=pl.Buffered(1))
    out_specs = [tok(w) for w in _PROJ_WIDTHS[:_N_ROW_MAJOR]]
    out_shape = [jax.ShapeDtypeStruct((b, t, w), F32) for w in _PROJ_WIDTHS[:_N_ROW_MAJOR]]
    for w in _PROJ_WIDTHS[_N_ROW_MAJOR:]:
        if col_major:
            out_specs.append(pl.BlockSpec((1, GRID_W, rows_t, w), lambda bi, i: (bi, 0, i, 0)))
            out_shape.append(jax.ShapeDtypeStruct((b, GRID_W, t // GRID_W, w), F32))
        else:
            out_specs.append(tok(w))
            out_shape.append(jax.ShapeDtypeStruct((b, t, w), F32))
    perm = jnp.asarray(_grid_perm(tm), BF16)
    return pl.pallas_call(
        functools.partial(_proj_kernel, col_major),
        grid=(b, t // tm),
        in_specs=[tok(d),
                  pl.BlockSpec((1, N_MOD, d), mod_map),
                  const((1, d)), const(w_cat.shape), const(perm.shape)],
        out_specs=out_specs,
        out_shape=out_shape,
        compiler_params=_cparams(("arbitrary", "arbitrary")),
        name="projection",
    )(x, mod, g, w_cat, perm)


def _gdn_prep(tile, n_tiles, qkv_ref, prev_ref, next_ref, ab_ref, convw_ref, alog_ref, dtb_ref,
              elg, ebt, ones, q_sc, k_sc, v_sc, lg_sc, bt_sc):
    tt = qkv_ref.shape[1]
    x = qkv_ref[0]
    row = lax.broadcasted_iota(jnp.int32, (tt, 1), 0)
    pv = jnp.where(tile > 0, prev_ref[0, 7:8, :], 0.0)
    nx = jnp.where(tile < n_tiles - 1, next_ref[0, 0:1, :], 0.0)
    x_prev = jnp.where(row == 0, pv, pltpu.roll(x, 1, axis=0))
    x_next = jnp.where(row == tt - 1, nx, pltpu.roll(x, tt - 1, axis=0))
    y = _silu(convw_ref[0:1, :] * x_prev + convw_ref[1:2, :] * x + convw_ref[2:3, :] * x_next)
    q = y[:, 0:GROUP_W]
    k = y[:, GROUP_W:2 * GROUP_W]
    q_sc[...] = q * lax.rsqrt(_dot((q * q).astype(BF16), ones) + EPS) * (HEAD_DIM ** -0.5)
    k_sc[...] = k * lax.rsqrt(_dot((k * k).astype(BF16), ones) + EPS)
    v_sc[...] = y[:, 2 * GROUP_W:3 * GROUP_W]
    ab = ab_ref[0]
    lg_c = -jnp.exp(alog_ref[...]) * _softplus(ab + dtb_ref[...])
    lg_sc[...] = _dot_sel_rhs(lg_c, elg)
    bt_sc[...] = _dot_sel_rhs(_sigmoid(ab), ebt)


def _gdn_chunk_terms(reverse, items, q_sc, k_sc, v_sc, lg_sc, bt_sc, tri, cm, bdm):
    bdm16 = bdm.astype(BF16)
    ld = lambda ref: [ref[c * CHUNK:(c + 1) * CHUNK, p * PAIR_W:(p + 1) * PAIR_W] for c, p in items]
    qp, kp, vp, lg, bt = ld(q_sc), ld(k_sc), ld(v_sc), ld(lg_sc), ld(bt_sc)
    bd16 = lambda xs: [_bd(x.astype(BF16), bdm16) for x in xs]
    pmul = lambda xs, ys: [_dot(x.astype(BF16), y) for x, y in zip(xs, bd16(ys))]

    gam = [_dot_sel(tri, x) for x in lg]
    dincl = [jnp.exp(_dot_sel(tri, x * cm(_G_SUP))) * cm(_G_INCL) for x in lg]
    kb = [x.astype(BF16) for x in kp]
    k2 = [_bd(x, bdm16) for x in kb]
    kk = [_dot_nt(a, b) for a, b in zip(kb, k2)]
    qk = [_dot_nt(a.astype(BF16), b) for a, b in zip(qp, k2)]
    m = [a * b * d * cm(_G_STRICT) for a, b, d in zip(kk, bt, dincl)]
    n1 = [-(x * cm(_G_B8)) for x in m]
    n2 = pmul(n1, n1)
    n4 = pmul(n2, n2)
    t_inv = [cm(_G_EYE) + x for x in n1]
    t_inv = [t + d for t, d in zip(t_inv, pmul(t_inv, n2))]
    t_inv = [t + d for t, d in zip(t_inv, pmul(t_inv, n4))]
    for lvl in (_G_C16, _G_C32, _G_C64):
        left = pmul(t_inv, [x * cm(lvl) for x in m])
        t_inv = [t - d for t, d in zip(t_inv, pmul(left, t_inv))]
    eg = [jnp.exp(x) for x in gam]
    rhs = [jnp.concatenate([_bd((v * b).astype(BF16), bdm16), _bd((k * b * e).astype(BF16), bdm16)], axis=1)
           for v, k, b, e in zip(vp, kp, bt, eg)]
    uw = [_dot(t.astype(BF16), r) for t, r in zip(t_inv, rhs)]
    last = 0 if reverse else CHUNK - 1
    g_last = [x[last:last + 1, :] for x in gam]
    return dict(
        u=[x[:, 0:PAIR_W] for x in uw],
        wq=[jnp.concatenate([x[:, PAIR_W:2 * PAIR_W], q * e], axis=0).astype(BF16) for x, q, e in zip(uw, qp, eg)],
        qkd=[(a * d).astype(BF16) for a, d in zip(qk, dincl)],
        kdt=[(k * jnp.exp(g - x)).T.astype(BF16) for k, g, x in zip(kp, g_last, gam)],
        decay=[jnp.exp(g) for g in g_last])


def _gdn_kernel(n_tiles, *refs):
    (qkv_f, prev_f, next_f, ab_f, qkv_b, prev_b, next_b, ab_b, convw_ref, alog_ref, dtb_ref,
     s0_ref, tri_ref, cm_ref, bdm_ref, elg_ref, ebt_ref, ones_ref,
     of_ref, ob_ref, sfin_ref, q_sc, k_sc, v_sc, lg_sc, bt_sc, s_sc) = refs
    step = pl.program_id(1)
    tt = qkv_f.shape[1]
    n_chunks = tt // CHUNK

    @pl.when(step == 0)
    def _():
        s_sc[...] = s0_ref[0]

    bdm = bdm_ref[...]
    bdm16 = bdm.astype(BF16)
    ones = ones_ref[...]
    tok = ((qkv_f, prev_f, next_f, ab_f), (qkv_b, prev_b, next_b, ab_b))
    o_refs = (of_ref, ob_ref)
    terms = []
    for d, reverse in enumerate((False, True)):
        tile = (n_tiles - 1 - step) if reverse else step
        sc = (q_sc.at[d], k_sc.at[d], v_sc.at[d], lg_sc.at[d], bt_sc.at[d])
        _gdn_prep(tile, n_tiles, *tok[d], convw_ref, alog_ref, dtb_ref, elg_ref[d], ebt_ref[d], ones, *sc)
        items = [(c, p) for c in range(n_chunks) for p in range(N_PAIRS)]
        cm = functools.partial(lambda dd, idx: cm_ref[dd, idx], d)
        terms.append(_gdn_chunk_terms(reverse, items, *sc, tri_ref[d], cm, bdm))

    s = [[s_sc[d, p] for p in range(N_PAIRS)] for d in range(2)]
    for ci in range(n_chunks):
        lanes = [(d, p, ((n_chunks - 1 - ci) if d else ci) * N_PAIRS + p) for d in range(2) for p in range(N_PAIRS)]
        sq = [_dot(terms[d]["wq"][i], s[d][p].astype(BF16)) for d, p, i in lanes]
        v_new = [terms[d]["u"][i] - x[0:CHUNK] for (d, p, i), x in zip(lanes, sq)]
        v16 = [x.astype(BF16) for x in v_new]
        o = [x[CHUNK:2 * CHUNK] + _dot(terms[d]["qkd"][i], _bd(v, bdm16)) for (d, p, i), x, v in zip(lanes, sq, v16)]
        upd = [_dot(terms[d]["kdt"][i], v) * bdm for (d, p, i), v in zip(lanes, v16)]
        for (d, p, i), x, y in zip(lanes, o, upd):
            c = i // N_PAIRS
            o_refs[d][0, c * CHUNK:(c + 1) * CHUNK, p * PAIR_W:(p + 1) * PAIR_W] = x
            s[d][p] = s[d][p] * terms[d]["decay"][i] + y
    for d in range(2):
        for p in range(N_PAIRS):
            s_sc[d, p] = s[d][p]

    @pl.when(step == n_tiles - 1)
    def _():
        sfin_ref[0] = s_sc[...]


def _gdn_scan(qkv, ab, conv_w, a_log_pad, dt_bias_pad, s0):
    b, t, _ = qkv.shape
    tt = min(GDN_TT, t)
    n_tiles = t // tt
    hb = tt // 8
    n_hb = t // 8
    consts = [_gdn_consts(rev) for rev in (False, True)]
    tri = np.stack([c[0] for c in consts])
    cm = np.stack([c[1] for c in consts])
    expand = [_gdn_expand(rev) for rev in (False, True)]
    e_lg = np.stack([e[0] for e in expand])
    e_bt = np.stack([e[1] for e in expand])
    const = lambda shape: pl.BlockSpec(shape, lambda bi, i: (0,) * len(shape))
    state_spec = pl.BlockSpec((1, 2, N_PAIRS, PAIR_W, PAIR_W), lambda bi, i: (bi, 0, 0, 0, 0))

    def tok_specs(tile_of):
        return [pl.BlockSpec((1, tt, 3 * GROUP_W), lambda bi, i: (bi, tile_of(i), 0)),
                pl.BlockSpec((1, 8, 3 * GROUP_W), lambda bi, i: (bi, jnp.maximum(tile_of(i) * hb - 1, 0), 0)),
                pl.BlockSpec((1, 8, 3 * GROUP_W), lambda bi, i: (bi, jnp.minimum((tile_of(i) + 1) * hb, n_hb - 1), 0)),
                pl.BlockSpec((1, tt, AB_PAD), lambda bi, i: (bi, tile_of(i), 0))]

    fwd_tile = lambda i: i
    bwd_tile = lambda i: n_tiles - 1 - i
    o_f, o_b, s_fin = pl.pallas_call(
        functools.partial(_gdn_kernel, n_tiles),
        grid=(b, n_tiles),
        in_specs=tok_specs(fwd_tile) + tok_specs(bwd_tile)
                 + [const((CONV_W, 3 * GROUP_W)), const((1, AB_PAD)), const((1, AB_PAD)), state_spec,
                    const(tri.shape), const(cm.shape), const((PAIR_W, PAIR_W)),
                    const(e_lg.shape), const(e_bt.shape), const((GROUP_W, GROUP_W))],
        out_specs=[pl.BlockSpec((1, tt, GROUP_W), lambda bi, i: (bi, fwd_tile(i), 0)),
                   pl.BlockSpec((1, tt, GROUP_W), lambda bi, i: (bi, bwd_tile(i), 0)),
                   state_spec],
        out_shape=[jax.ShapeDtypeStruct((b, t, GROUP_W), F32),
                   jax.ShapeDtypeStruct((b, t, GROUP_W), F32),
                   jax.ShapeDtypeStruct((b, 2, N_PAIRS, PAIR_W, PAIR_W), F32)],
        scratch_shapes=[pltpu.VMEM((2, tt, GROUP_W), F32)] * 5 + [pltpu.VMEM((2, N_PAIRS, PAIR_W, PAIR_W), F32)],
        compiler_params=_cparams(("arbitrary", "arbitrary")),
        name="gdn_scan",
    )(qkv, qkv, qkv, ab, qkv, qkv, qkv, ab, conv_w, a_log_pad, dt_bias_pad, s0,
      jnp.asarray(tri, BF16), jnp.asarray(cm), jnp.asarray(_bd_mask()),
      jnp.asarray(e_lg, BF16), jnp.asarray(e_bt, BF16), jnp.asarray(_seg_ones(GROUP_W), BF16))
    return o_f, o_b, s_fin


def _hgrn_prep(reverse, layer, ph_ref, logits, q_sc, k_sc, lf_sc):
    e = jnp.exp(logits - jnp.max(logits, axis=0, keepdims=True))
    prob = e / jnp.sum(e, axis=0, keepdims=True)
    lb = jnp.maximum(jnp.sum(prob[0:layer + 1], axis=0, keepdims=True) - prob[0:1], 0.0)
    z_off = 2 * GROUP_W if reverse else GROUP_W
    q_sc[...] = _silu(ph_ref[:, 0:GROUP_W]) * (HEAD_DIM ** -0.5)
    sig = _sigmoid(ph_ref[:, z_off:z_off + GROUP_W])
    lf_sc[...] = jnp.log(jnp.maximum(lb, LB_FLOOR) + (1.0 - lb) * sig)
    k_sc[...] = (1.0 - lb) * (1.0 - sig)


def _hgrn_chunk_terms(reverse, n_chunks, ph_ref, q_sc, k_sc, lf_sc, b_sc, ez_sc, tri, hm, bdm16):
    n_lvl = len(_H_LEVELS)
    q_row0 = n_lvl * CHUNK
    k_row0 = (n_lvl + 1) * CHUNK
    last = 0 if reverse else CHUNK - 1
    rows = lambda c: slice(c * CHUNK, (c + 1) * CHUNK)
    lanes = lambda p: slice(p * PAIR_W, (p + 1) * PAIR_W)
    for c in range(n_chunks):
        lf = lf_sc[rows(c), :]
        b_sc[c] = _dot_sel(tri, lf)
        for n, ex in enumerate(_hgrn_exponents(reverse, b_sc.at[c], lf)):
            ez_sc[c, n * CHUNK:(n + 1) * CHUNK, :] = jnp.exp(ex.astype(BF16))
    items = [(c, p) for c in range(n_chunks) for p in range(N_PAIRS)]
    ez = lambda row0: [ez_sc[c, row0:row0 + CHUNK, lanes(p)] for c, p in items]
    q16 = [q_sc[rows(c), lanes(p)].astype(BF16) for c, p in items]
    k16 = [k_sc[rows(c), lanes(p)].astype(BF16) for c, p in items]
    vp = [ph_ref[rows(c), 3 * GROUP_W + p * PAIR_W:3 * GROUP_W + (p + 1) * PAIR_W] for c, p in items]
    bdk = [_bd(k, bdm16) for k in k16]
    a = [_dot_nt(q, k) * hm(n_lvl) for q, k in zip(q16, bdk)]
    for li in range(n_lvl):
        e16 = ez(li * CHUNK)
        sc = [_dot_nt(q * e, k * jnp.concatenate([e, e], axis=0)) for q, k, e in zip(q16, bdk, e16)]
        a = [x + y * hm(li) for x, y in zip(a, sc)]
    o_intra = [_dot(x.astype(BF16), _bd(v.astype(BF16), bdm16)) for x, v in zip(a, vp)]
    return dict(
        o=o_intra,
        qd=[q * e for q, e in zip(q16, ez(q_row0))],
        kd=[k * e for k, e in zip(k16, ez(k_row0))],
        vt=[v.T.astype(BF16) for v in vp],
        decay=[jnp.exp(b_sc[c, last:last + 1, lanes(p)]) for c, p in items])


def _hgrn_kernel(layer, n_steps, ph_f, ph_b, lbl_ref, s0_ref, tri_ref, hm_ref, bdm_ref,
                 of_ref, ob_ref, sfin_ref, q_sc, k_sc, lf_sc, b_sc, ez_sc, s_sc):
    step = pl.program_id(1)
    tt = ph_f.shape[2]
    n_chunks = tt // CHUNK

    @pl.when(step == 0)
    def _():
        s_sc[...] = s0_ref[0]

    bdm = bdm_ref[...]
    bdm16 = bdm.astype(BF16)
    ph = (ph_f.at[0, 0], ph_b.at[0, 0])
    o_refs = (of_ref, ob_ref)
    terms = []
    for d, reverse in enumerate((False, True)):
        sc = (q_sc.at[d], k_sc.at[d], lf_sc.at[d])
        _hgrn_prep(reverse, layer, ph[d], lbl_ref[d], *sc)
        hm = functools.partial(lambda dd, idx: hm_ref[dd, idx], d)
        terms.append(_hgrn_chunk_terms(reverse, n_chunks, ph[d], *sc, b_sc.at[d], ez_sc.at[d], tri_ref[d], hm, bdm16))

    s = [[s_sc[d, p] for p in range(N_PAIRS)] for d in range(2)]
    for ci in range(n_chunks):
        lanes = [(d, p, ((n_chunks - 1 - ci) if d else ci) * N_PAIRS + p) for d in range(2) for p in range(N_PAIRS)]
        o = [terms[d]["o"][i] + _dot_nt(terms[d]["qd"][i], s[d][p].astype(BF16)) for d, p, i in lanes]
        upd = [_dot(terms[d]["vt"][i], terms[d]["kd"][i]) * bdm for d, p, i in lanes]
        for (d, p, i), x, y in zip(lanes, o, upd):
            c = i // N_PAIRS
            o_refs[d][0, 0, c * CHUNK:(c + 1) * CHUNK, p * PAIR_W:(p + 1) * PAIR_W] = x
            s[d][p] = s[d][p] * terms[d]["decay"][i] + y
    for d in range(2):
        for p in range(N_PAIRS):
            s_sc[d, p] = s[d][p]

    @pl.when(step == n_steps - 1)
    def _():
        sfin_ref[0] = s_sc[...]


def _hgrn_scan(ph, lb_logits, s0, layer):
    b, n_steps, tt, _ = ph.shape
    depth = lb_logits.shape[0]
    consts = [_hgrn_consts(rev) for rev in (False, True)]
    tri = np.stack([c[0] for c in consts])
    hm = np.stack([c[1] for c in consts])
    n_chunks = tt // CHUNK
    n_exp = len(_H_LEVELS) + 2
    const = lambda shape: pl.BlockSpec(shape, lambda bi, i: (0,) * len(shape))
    state_spec = pl.BlockSpec((1, 2, N_PAIRS, PAIR_W, PAIR_W), lambda bi, i: (bi, 0, 0, 0, 0))
    fwd = lambda w: pl.BlockSpec((1, 1, tt, w), lambda bi, i: (bi, i, 0, 0))
    bwd = lambda w: pl.BlockSpec((1, 1, tt, w), lambda bi, i: (bi, n_steps - 1 - i, 0, 0))
    return pl.pallas_call(
        functools.partial(_hgrn_kernel, layer, n_steps),
        grid=(b, n_steps),
        in_specs=[fwd(4 * GROUP_W), bwd(4 * GROUP_W), const((2, depth, GROUP_W)), state_spec,
                  const(tri.shape), const(hm.shape), const((PAIR_W, PAIR_W))],
        out_specs=[fwd(GROUP_W), bwd(GROUP_W), state_spec],
        out_shape=[jax.ShapeDtypeStruct((b, n_steps, tt, GROUP_W), F32),
                   jax.ShapeDtypeStruct((b, n_steps, tt, GROUP_W), F32),
                   jax.ShapeDtypeStruct((b, 2, N_PAIRS, PAIR_W, PAIR_W), F32)],
        scratch_shapes=[pltpu.VMEM((2, tt, GROUP_W), F32)] * 3
                       + [pltpu.VMEM((2, n_chunks, CHUNK, GROUP_W), F32),
                          pltpu.VMEM((2, n_chunks, n_exp * CHUNK, GROUP_W), BF16),
                          pltpu.VMEM((2, N_PAIRS, PAIR_W, PAIR_W), F32)],
        compiler_params=_cparams(("arbitrary", "arbitrary")),
        name="hgrn_scan",
    )(ph, ph, jnp.transpose(lb_logits, (1, 0, 2)), s0, jnp.asarray(tri, BF16), jnp.asarray(hm),
      jnp.asarray(_bd_mask()))


def _mlp_kernel(final, col_major, x_ref, oaf_ref, oab_ref, obf_ref, obb_ref, ga_ref, gb_ref, mod_ref,
                gg_ref, hg_ref, n2g_ref, fg_ref, ones_ref, perm_ref, wo_ref, w1_ref, w2_ref, o_ref):
    ones = ones_ref[...]
    tm = x_ref.shape[1]
    oa = oaf_ref[0] + oab_ref[0]
    ob = (obf_ref[0] + obb_ref[0]).reshape(tm, GROUP_W)
    gb = gb_ref[0].reshape(tm, GROUP_W)
    inv_d = 1.0 / HEAD_DIM
    ya = oa * lax.rsqrt(_dot((oa * oa).astype(BF16), ones) * inv_d + EPS) * gg_ref[...] * _silu(ga_ref[0])
    yb = ob * lax.rsqrt(_dot((ob * ob).astype(BF16), ones) * inv_d + EPS) * hg_ref[...] * _sigmoid(gb)
    yb = yb.astype(BF16)
    if col_major:
        yb = _dot(perm_ref[...], yb).astype(BF16)
    y = _dot(ya.astype(BF16), wo_ref[0:GROUP_W, :]) + _dot(yb, wo_ref[GROUP_W:2 * GROUP_W, :])
    x1 = x_ref[0] + mod_ref[0, 2:3, :] * y
    h = _norm_mod(x1, n2g_ref[...], mod_ref[0, 3:4, :], mod_ref[0, 4:5, :]).astype(BF16)
    hid = jnp.maximum(_dot(h, w1_ref[...]), 0.0)
    x2 = x1 + mod_ref[0, 5:6, :] * _dot((hid * hid).astype(BF16), w2_ref[...])
    if final:
        x2 = x2 * lax.rsqrt(jnp.mean(x2 * x2, axis=-1, keepdims=True) + EPS) * fg_ref[...]
    o_ref[0] = x2


def _out_mlp(x, oa_f, oa_b, ob_f, ob_b, ga, gb, mod, gdn_g, hgrn_g, n2g, final_g, w_out, w1, w2,
             shared_mod, final, col_major):
    b, t, d = x.shape
    tm = MLP_TM if col_major else min(MLP_TM, t)
    mod_map = (lambda bi, i: (0, 0, 0)) if shared_mod else (lambda bi, i: (bi, 0, 0))
    tok = lambda w: pl.BlockSpec((1, tm, w), lambda bi, i: (bi, i, 0))
    col = pl.BlockSpec((1, GRID_W, tm // GRID_W, GROUP_W), lambda bi, i: (bi, 0, i, 0))
    hg_spec = col if col_major else tok(GROUP_W)
    const = lambda shape: pl.BlockSpec(shape, lambda bi, i: (0,) * len(shape), pipeline_mode=pl.Buffered(1))
    perm_t = jnp.asarray(_grid_perm(tm).T, BF16)
    return pl.pallas_call(
        functools.partial(_mlp_kernel, final, col_major),
        grid=(b, t // tm),
        in_specs=[tok(d), tok(GROUP_W), tok(GROUP_W), hg_spec, hg_spec, tok(GROUP_W), hg_spec,
                  pl.BlockSpec((1, N_MOD, d), mod_map),
                  const((1, GROUP_W)), const((1, GROUP_W)), const((1, d)), const((1, d)),
                  const((GROUP_W, GROUP_W)), const(perm_t.shape),
                  const(w_out.shape), const(w1.shape), const(w2.shape)],
        out_specs=tok(d),
        out_shape=jax.ShapeDtypeStruct((b, t, d), F32),
        compiler_params=_cparams(("arbitrary", "arbitrary")),
        name="out_mlp",
    )(x, oa_f, oa_b, ob_f, ob_b, ga, gb, mod, gdn_g, hgrn_g, n2g, final_g,
      jnp.asarray(_seg_ones(GROUP_W), BF16), perm_t, w_out, w1, w2)


def _split_w_in(w):
    g = GROUP_W
    qkv, ga, ab = w[:, 0:3 * g], w[:, 3 * g:4 * g], w[:, 4 * g:4 * g + 4 * N_HEADS]
    rest = w[:, 4 * g + 4 * N_HEADS:]
    ph, gb = rest[:, 0:4 * g], rest[:, 4 * g:5 * g]
    ab = jnp.pad(ab, ((0, 0), (0, AB_PAD - 4 * N_HEADS)))
    return jnp.concatenate([qkv, ga, ab, ph, gb], axis=1).astype(BF16)


def _pad_lanes(v):
    flat = v.reshape(1, -1)
    return jnp.pad(flat, ((0, 0), (0, AB_PAD - flat.shape[1])))


def kernel(x, c, ctx, c_ctx, w_mod, b_mod, norm1_g, norm2_g, w_in, conv_w, a_log, dt_bias,
           gdn_norm_g, hgrn_norm_g, lb_logits, w_out, w_mlp1, w_mlp2, final_g):
    depth = w_mod.shape[0]
    b, _, d = x.shape
    cvec = jnp.concatenate([c, c_ctx[None, :], jnp.zeros((8 - b - 1, d), F32)], axis=0)
    mod = _modulation(cvec, w_mod, b_mod).reshape(depth, 8, N_MOD, d)
    zero_state = jnp.zeros((b, 2, N_PAIRS, PAIR_W, PAIR_W), F32)
    fg = final_g.reshape(1, d)

    x_lat, x_ctx = x, ctx
    for l in range(depth):
        need_ctx = l < depth - 1
        mod_lat, mod_ctx = mod[l, 0:b], mod[l, b:b + 1]
        w_cat = _split_w_in(w_in[l])
        n1g = norm1_g[l].reshape(1, d)
        a_pad, dt_pad = _pad_lanes(a_log[l]), _pad_lanes(dt_bias[l])
        gg = jnp.tile(gdn_norm_g[l], N_HEADS).reshape(1, GROUP_W)
        hg = jnp.tile(hgrn_norm_g[l], N_HEADS).reshape(1, GROUP_W)
        wo, w1, w2 = w_out[l].astype(BF16), w_mlp1[l].astype(BF16), w_mlp2[l].astype(BF16)

        qkv_c, ga_c, ab_c, ph_c, gb_c = _projection(x_ctx, mod_ctx, n1g, w_cat, True, False)
        qkv_l, ga_l, ab_l, ph_l, gb_l = _projection(x_lat, mod_lat, n1g, w_cat, False, True)

        oa_cf, oa_cb, sa = _gdn_scan(qkv_c, ab_c, conv_w[l], a_pad, dt_pad, zero_state)
        oa_lf, oa_lb, _ = _gdn_scan(qkv_l, ab_l, conv_w[l], a_pad, dt_pad, sa)

        t_ctx = ph_c.shape[1]
        tt_c = min(HGRN_TT, t_ctx)
        ob_cf, ob_cb, sb = _hgrn_scan(ph_c.reshape(b, t_ctx // tt_c, tt_c, 4 * GROUP_W), lb_logits, zero_state, l)
        ob_lf, ob_lb, _ = _hgrn_scan(ph_l, lb_logits, sb, l)

        n2g = norm2_g[l].reshape(1, d)
        x_lat = _out_mlp(x_lat, oa_lf, oa_lb, ob_lf, ob_lb, ga_l, gb_l, mod_lat, gg, hg, n2g, fg, wo, w1, w2,
                         False, not need_ctx, True)
        if need_ctx:
            x_ctx = _out_mlp(x_ctx, oa_cf, oa_cb, ob_cf.reshape(b, t_ctx, GROUP_W), ob_cb.reshape(b, t_ctx, GROUP_W),
                             ga_c, gb_c, mod_ctx, gg, hg, n2g, fg, wo, w1, w2, True, False, False)
    return x_lat
```

```python
import functools

import numpy as np
import jax
import jax.numpy as jnp
from jax import lax
from jax.experimental import pallas as pl
from jax.experimental.pallas import tpu as pltpu

F32 = jnp.float32
BF16 = jnp.bfloat16

HEAD_DIM = 64
N_HEADS = 8
GROUP_W = N_HEADS * HEAD_DIM
N_PAIRS = N_HEADS // 2
PAIR_W = 2 * HEAD_DIM
CHUNK = 64
GRID_W = 64
CONV_W = 3
N_MOD = 6
EPS = 1e-6
LB_FLOOR = 1e-30
AB_PAD = 128
VMEM_LIMIT = 56 * 1024 * 1024

PROJ_TM = 512
MLP_TM = 512
GDN_TT = 256
HGRN_TT = 128


def _dot(a, b):
    return jnp.dot(a, b, preferred_element_type=F32)


def _dot_nt(a, b):
    return lax.dot_general(a, b, (((1,), (1,)), ((), ())), preferred_element_type=F32)


def _dot_sel(sel, x):
    hi = x.astype(BF16)
    lo = (x - hi.astype(F32)).astype(BF16)
    return _dot(sel, hi) + _dot(sel, lo)


def _dot_sel_rhs(x, sel):
    hi = x.astype(BF16)
    lo = (x - hi.astype(F32)).astype(BF16)
    return _dot(hi, sel) + _dot(lo, sel)


def _bd(x, bdmask):
    return jnp.concatenate([x, x], axis=0) * bdmask.astype(x.dtype)


def _sigmoid(x):
    return 0.5 * jnp.tanh(0.5 * x) + 0.5


def _silu(x):
    return x * _sigmoid(x)


def _softplus(x):
    return jnp.maximum(x, 0.0) + jnp.log1p(jnp.exp(-jnp.abs(x)))


def _cparams(sem):
    return pltpu.CompilerParams(dimension_semantics=sem, vmem_limit_bytes=VMEM_LIMIT)


def _packed_ij():
    i = np.arange(CHUNK)[:, None]
    j = (np.arange(PAIR_W) % HEAD_DIM)[None, :]
    return i, j


def _bd_mask():
    r = np.arange(PAIR_W)[:, None] // HEAD_DIM
    c = np.arange(PAIR_W)[None, :] // HEAD_DIM
    return (r == c).astype(np.float32)


def _seg_ones(width):
    r = np.arange(width)[:, None] // HEAD_DIM
    c = np.arange(width)[None, :] // HEAD_DIM
    return (r == c).astype(np.float32)


_G_SUP, _G_INCL, _G_STRICT, _G_EYE, _G_B8, _G_C16, _G_C32, _G_C64 = range(8)


def _gdn_consts(reverse):
    i, j = _packed_ij()
    t = np.arange(CHUNK)
    if reverse:
        tri = (t[None, :] >= t[:, None])
        sup = i < j
        incl = j >= i
        strict = j > i
    else:
        tri = (t[None, :] <= t[:, None])
        sup = i > j
        incl = j <= i
        strict = j < i
    eye = i == j
    b8 = (i // 8) == (j // 8)
    c16 = ((i // 16) == (j // 16)) & ~b8
    c32 = ((i // 32) == (j // 32)) & ((i // 16) != (j // 16))
    c64 = (i // 32) != (j // 32)
    stack = np.stack([sup, incl, strict, eye, b8, c16, c32, c64]).astype(np.float32)
    return tri.astype(np.float32), stack


def _gdn_expand(reverse):
    d = 1 if reverse else 0
    col = np.arange(AB_PAD)[:, None]
    head = (np.arange(GROUP_W) // HEAD_DIM)[None, :]
    e_lg = (col == d * N_HEADS + head)
    e_bt = (col == 2 * N_HEADS + d * N_HEADS + head)
    return e_lg.astype(np.float32), e_bt.astype(np.float32)


_H_LEVELS = (1, 2, 4, 8, 16, 32)


def _hgrn_consts(reverse):
    r = np.arange(CHUNK)[:, None]
    t = np.arange(CHUNK)[None, :]
    i, j = _packed_ij()
    q_par, k_par = (0, 1) if reverse else (1, 0)
    masks = [((i // (2 * s)) == (j // (2 * s))) & (((i // s) % 2) == q_par) & (((j // s) % 2) == k_par)
             for s in _H_LEVELS]
    masks.append(i == j)
    tri = (t >= r) if reverse else (t <= r)
    return tri.astype(np.float32), np.stack(masks).astype(np.float32)


def _hgrn_exponents(reverse, b_ref, lf):
    w = lf.shape[1]
    rig = lax.broadcasted_iota(jnp.int32, (8, w), 0)
    grp = lambda g: b_ref[8 * g:8 * g + 8, :]
    row = lambda r: b_ref[r:r + 1, :]
    n_grp = CHUNK // 8
    q_par = 0 if reverse else 1
    sign = lambda s: jnp.where(((rig // s) % 2) == q_par, 1.0, -1.0)
    out = []
    r64 = lax.broadcasted_iota(jnp.int32, lf.shape, 0)
    out.append(jnp.where((r64 % 2) == q_par, lf, 0.0))
    bnd2 = (2, 6) if reverse else (1, 5)
    sg2, sg4 = sign(2), sign(4)
    out.append(jnp.concatenate(
        [(grp(g) - jnp.where(rig < 4, row(8 * g + bnd2[0]), row(8 * g + bnd2[1]))) * sg2 for g in range(n_grp)], axis=0))
    bnd4 = 4 if reverse else 3
    out.append(jnp.concatenate([(grp(g) - row(8 * g + bnd4)) * sg4 for g in range(n_grp)], axis=0))
    for s in (8, 16, 32):
        parts = []
        for g in range(n_grp):
            blk = (8 * g) // s
            bnd = 2 * s * (blk // 2) + (s if reverse else s - 1)
            parts.append(grp(g) - row(bnd) if (blk % 2) == q_par else row(bnd) - grp(g))
        out.append(jnp.concatenate(parts, axis=0))
    out.append(b_ref[...])
    out.append(row(0 if reverse else CHUNK - 1) - b_ref[...])
    return out


def _mod_kernel(c_ref, w_ref, b_ref, o_ref):
    sc = _silu(c_ref[...]).astype(BF16)
    o_ref[0] = _dot(sc, w_ref[0].astype(BF16)) + b_ref[0]


def _modulation(cvec, w_mod, b_mod):
    depth, d, n = w_mod.shape
    tn = 1536
    return pl.pallas_call(
        _mod_kernel,
        grid=(depth, n // tn),
        in_specs=[pl.BlockSpec((8, d), lambda l, j: (0, 0)),
                  pl.BlockSpec((1, d, tn), lambda l, j: (l, 0, j)),
                  pl.BlockSpec((1, 1, tn), lambda l, j: (l, 0, j))],
        out_specs=pl.BlockSpec((1, 8, tn), lambda l, j: (l, 0, j)),
        out_shape=jax.ShapeDtypeStruct((depth, 8, n), F32),
        compiler_params=_cparams(("arbitrary", "arbitrary")),
        name="modulation",
    )(cvec, w_mod, b_mod.reshape(depth, 1, n))


_PROJ_WIDTHS = (3 * GROUP_W, GROUP_W, AB_PAD, 4 * GROUP_W, GROUP_W)
_N_ROW_MAJOR = 3


def _norm_mod(x, g, shift, scale):
    y = x * lax.rsqrt(jnp.mean(x * x, axis=-1, keepdims=True) + EPS) * g
    return y * (1.0 + scale) + shift


def _grid_perm(tm):
    rows = tm // GRID_W
    n = np.arange(tm)
    src = (n % rows) * GRID_W + n // rows
    p = np.zeros((tm, tm), np.float32)
    p[n, src] = 1.0
    return p


def _head_sums(x, ones):
    half = ones.shape[0]
    return jnp.concatenate([_dot(x[:, 0:half].astype(BF16), ones), _dot(x[:, half:2 * half].astype(BF16), ones)],
                           axis=1)


def _proj_kernel(col_major, n_tiles, x_ref, xp_ref, xn_ref, mod_ref, g_ref, w_ref, perm_ref, convw_ref, ones_ref,
                 *out_refs):
    tile = pl.program_id(1)
    g, shift, scale = g_ref[...], mod_ref[0, 0:1, :], mod_ref[0, 1:2, :]
    h = _norm_mod(x_ref[0], g, shift, scale).astype(BF16)
    tm = h.shape[0]

    qkv_ref = out_refs[0]
    halo = jnp.concatenate([xp_ref[0], xn_ref[0]], axis=0)
    h_halo = _norm_mod(halo, g, shift, scale).astype(BF16)
    groups = [slice(n * GROUP_W, (n + 1) * GROUP_W) for n in range(3)]
    p_halo = [_dot(h_halo, w_ref[:, c]) for c in groups]
    p = [_dot(h, w_ref[:, c]) for c in groups]
    off = _PROJ_WIDTHS[0]
    for n, (ref, w) in enumerate(zip(out_refs, _PROJ_WIDTHS)):
        if n == 0:
            continue
        if n == _N_ROW_MAJOR and col_major:
            h = _dot(perm_ref[...], h).astype(BF16)
        val = _dot(h, w_ref[:, off:off + w])
        ref[0] = val.reshape(ref.shape[1:])
        off += w

    row = lax.broadcasted_iota(jnp.int32, (tm, 1), 0)
    y = []
    for c, pc, ph in zip(groups, p, p_halo):
        pv = jnp.where(tile > 0, ph[7:8, :], 0.0)
        nx = jnp.where(tile < n_tiles - 1, ph[8:9, :], 0.0)
        p_prev = jnp.where(row == 0, pv, pltpu.roll(pc, 1, axis=0))
        p_next = jnp.where(row == tm - 1, nx, pltpu.roll(pc, tm - 1, axis=0))
        y.append(_silu(convw_ref[0:1, c] * p_prev + convw_ref[1:2, c] * pc + convw_ref[2:3, c] * p_next))
    ones = ones_ref[...]
    q, k, v = y
    qkv_ref[0, :, groups[2]] = v
    qkv_ref[0, :, groups[0]] = q * lax.rsqrt(_head_sums(q * q, ones) + EPS) * (HEAD_DIM ** -0.5)
    qkv_ref[0, :, groups[1]] = k * lax.rsqrt(_head_sums(k * k, ones) + EPS)


def _projection(x, mod, g, w_cat, conv_w, shared_mod, col_major):
    b, t, d = x.shape
    tm = PROJ_TM if col_major else min(PROJ_TM, t)
    rows_t = tm // GRID_W
    n_tiles = t // tm
    hb = tm // 8
    n_hb = t // 8
    mod_map = (lambda bi, i: (0, 0, 0)) if shared_mod else (lambda bi, i: (bi, 0, 0))
    tok = lambda w: pl.BlockSpec((1, tm, w), lambda bi, i: (bi, i, 0))
    const = lambda shape: pl.BlockSpec(shape, lambda bi, i: (0,) * len(shape), pipeline_mode=pl.Buffered(1))
    out_specs = [tok(w) for w in _PROJ_WIDTHS[:_N_ROW_MAJOR]]
    out_shape = [jax.ShapeDtypeStruct((b, t, w), F32) for w in _PROJ_WIDTHS[:_N_ROW_MAJOR]]
    for w in _PROJ_WIDTHS[_N_ROW_MAJOR:]:
        if col_major:
            out_specs.append(pl.BlockSpec((1, GRID_W, rows_t, w), lambda bi, i: (bi, 0, i, 0)))
            out_shape.append(jax.ShapeDtypeStruct((b, GRID_W, t // GRID_W, w), F32))
        else:
            out_specs.append(tok(w))
            out_shape.append(jax.ShapeDtypeStruct((b, t, w), F32))
    perm = jnp.asarray(_grid_perm(tm), BF16)
    ones = jnp.asarray(_seg_ones(GROUP_W // 2), BF16)
    return pl.pallas_call(
        functools.partial(_proj_kernel, col_major, n_tiles),
        grid=(b, n_tiles),
        in_specs=[tok(d),
                  pl.BlockSpec((1, 8, d), lambda bi, i: (bi, jnp.maximum(i * hb - 1, 0), 0)),
                  pl.BlockSpec((1, 8, d), lambda bi, i: (bi, jnp.minimum((i + 1) * hb, n_hb - 1), 0)),
                  pl.BlockSpec((1, N_MOD, d), mod_map),
                  const((1, d)), const(w_cat.shape), const(perm.shape), const(conv_w.shape), const(ones.shape)],
        out_specs=out_specs,
        out_shape=out_shape,
        compiler_params=_cparams(("arbitrary", "arbitrary")),
        name="projection",
    )(x, x, x, mod, g, w_cat, perm, conv_w, ones)


def _gdn_gates(ab_ref, alog_ref, dtb_ref, elg, ebt, lg_sc, bt_sc):
    ab = ab_ref[0]
    lg_c = -jnp.exp(alog_ref[...]) * _softplus(ab + dtb_ref[...])
    lg_sc[...] = _dot_sel_rhs(lg_c, elg)
    bt_sc[...] = _dot_sel_rhs(_sigmoid(ab), ebt)


def _gdn_chunk_terms(reverse, items, qkv_ref, lg_sc, bt_sc, tri, cm, bdm):
    bdm16 = bdm.astype(BF16)
    ld = lambda ref, off=0: [ref[c * CHUNK:(c + 1) * CHUNK, off + p * PAIR_W:off + (p + 1) * PAIR_W] for c, p in items]
    qp, kp, vp = ld(qkv_ref), ld(qkv_ref, GROUP_W), ld(qkv_ref, 2 * GROUP_W)
    bt = ld(bt_sc)
    bd16 = lambda xs: [_bd(x.astype(BF16), bdm16) for x in xs]
    pmul = lambda xs, ys: [_dot(x.astype(BF16), y) for x, y in zip(xs, bd16(ys))]

    gam_c = {c: _dot_sel(tri, lg_sc[c * CHUNK:(c + 1) * CHUNK, :]) for c in sorted({c for c, _ in items})}
    gam = [gam_c[c][:, p * PAIR_W:(p + 1) * PAIR_W] for c, p in items]
    gam_row = [jnp.sum(x * cm(_G_EYE), axis=0, keepdims=True) for x in gam]
    dincl = [jnp.exp((x - r) * cm(_G_INCL)) * cm(_G_INCL) for x, r in zip(gam, gam_row)]
    kb = [x.astype(BF16) for x in kp]
    k2 = [_bd(x, bdm16) for x in kb]
    kk = [_dot_nt(a, b) for a, b in zip(kb, k2)]
    qk = [_dot_nt(a.astype(BF16), b) for a, b in zip(qp, k2)]
    m = [a * b * d * cm(_G_STRICT) for a, b, d in zip(kk, bt, dincl)]
    n1 = [-(x * cm(_G_B8)) for x in m]
    n2 = pmul(n1, n1)
    n4 = pmul(n2, n2)
    t_inv = [cm(_G_EYE) + x for x in n1]
    t_inv = [t + d for t, d in zip(t_inv, pmul(t_inv, n2))]
    t_inv = [t + d for t, d in zip(t_inv, pmul(t_inv, n4))]
    for lvl in (_G_C16, _G_C32, _G_C64):
        left = pmul(t_inv, [x * cm(lvl) for x in m])
        t_inv = [t - d for t, d in zip(t_inv, pmul(left, t_inv))]
    eg = [jnp.exp(x) for x in gam]
    rhs = [jnp.concatenate([_bd((v * b).astype(BF16), bdm16), _bd((k * b * e).astype(BF16), bdm16)], axis=1)
           for v, k, b, e in zip(vp, kp, bt, eg)]
    uw = [_dot(t.astype(BF16), r) for t, r in zip(t_inv, rhs)]
    last = 0 if reverse else CHUNK - 1
    g_last = [x[last:last + 1, :] for x in gam]
    return dict(
        u=[x[:, 0:PAIR_W] for x in uw],
        wq=[jnp.concatenate([x[:, PAIR_W:2 * PAIR_W], q * e], axis=0).astype(BF16) for x, q, e in zip(uw, qp, eg)],
        qkd=[(a * d).astype(BF16) for a, d in zip(qk, dincl)],
        kdt=[(k * jnp.exp(g - x)).T.astype(BF16) for k, g, x in zip(kp, g_last, gam)],
        decay=[jnp.exp(g) for g in g_last])


def _gdn_kernel(n_tiles, *refs):
    (qkv_f, ab_f, qkv_b, ab_b, alog_ref, dtb_ref, s0_ref, tri_ref, cm_ref, bdm_ref, elg_ref, ebt_ref,
     of_ref, ob_ref, sfin_ref, lg_sc, bt_sc, s_sc) = refs
    step = pl.program_id(1)
    tt = qkv_f.shape[1]
    n_chunks = tt // CHUNK

    @pl.when(step == 0)
    def _():
        s_sc[...] = s0_ref[0]

    bdm = bdm_ref[...]
    bdm16 = bdm.astype(BF16)
    tok = ((qkv_f, ab_f), (qkv_b, ab_b))
    o_refs = (of_ref, ob_ref)
    terms = []
    for d, reverse in enumerate((False, True)):
        qkv_ref, ab_ref = tok[d]
        _gdn_gates(ab_ref, alog_ref, dtb_ref, elg_ref[d], ebt_ref[d], lg_sc.at[d], bt_sc.at[d])
        items = [(c, p) for c in range(n_chunks) for p in range(N_PAIRS)]
        cm = functools.partial(lambda dd, idx: cm_ref[dd, idx], d)
        terms.append(_gdn_chunk_terms(reverse, items, qkv_ref.at[0], lg_sc.at[d], bt_sc.at[d], tri_ref[d], cm, bdm))

    s = [[s_sc[d, p] for p in range(N_PAIRS)] for d in range(2)]
    for ci in range(n_chunks):
        lanes = [(d, p, ((n_chunks - 1 - ci) if d else ci) * N_PAIRS + p) for d in range(2) for p in range(N_PAIRS)]
        sq = [_dot(terms[d]["wq"][i], s[d][p].astype(BF16)) for d, p, i in lanes]
        v_new = [terms[d]["u"][i] - x[0:CHUNK] for (d, p, i), x in zip(lanes, sq)]
        v16 = [x.astype(BF16) for x in v_new]
        o = [x[CHUNK:2 * CHUNK] + _dot(terms[d]["qkd"][i], _bd(v, bdm16)) for (d, p, i), x, v in zip(lanes, sq, v16)]
        upd = [_dot(terms[d]["kdt"][i], v) * bdm for (d, p, i), v in zip(lanes, v16)]
        for (d, p, i), x, y in zip(lanes, o, upd):
            c = i // N_PAIRS
            o_refs[d][0, c * CHUNK:(c + 1) * CHUNK, p * PAIR_W:(p + 1) * PAIR_W] = x
            s[d][p] = s[d][p] * terms[d]["decay"][i] + y
    for d in range(2):
        for p in range(N_PAIRS):
            s_sc[d, p] = s[d][p]

    @pl.when(step == n_tiles - 1)
    def _():
        sfin_ref[0] = s_sc[...]


def _gdn_scan(qkv, ab, a_log_pad, dt_bias_pad, s0):
    b, t, _ = qkv.shape
    tt = min(GDN_TT, t)
    n_tiles = t // tt
    consts = [_gdn_consts(rev) for rev in (False, True)]
    tri = np.stack([c[0] for c in consts])
    cm = np.stack([c[1] for c in consts])
    expand = [_gdn_expand(rev) for rev in (False, True)]
    e_lg = np.stack([e[0] for e in expand])
    e_bt = np.stack([e[1] for e in expand])
    const = lambda shape: pl.BlockSpec(shape, lambda bi, i: (0,) * len(shape))
    state_spec = pl.BlockSpec((1, 2, N_PAIRS, PAIR_W, PAIR_W), lambda bi, i: (bi, 0, 0, 0, 0))

    def tok_specs(tile_of):
        return [pl.BlockSpec((1, tt, 3 * GROUP_W), lambda bi, i: (bi, tile_of(i), 0)),
                pl.BlockSpec((1, tt, AB_PAD), lambda bi, i: (bi, tile_of(i), 0))]

    fwd_tile = lambda i: i
    bwd_tile = lambda i: n_tiles - 1 - i
    o_f, o_b, s_fin = pl.pallas_call(
        functools.partial(_gdn_kernel, n_tiles),
        grid=(b, n_tiles),
        in_specs=tok_specs(fwd_tile) + tok_specs(bwd_tile)
                 + [const((1, AB_PAD)), const((1, AB_PAD)), state_spec,
                    const(tri.shape), const(cm.shape), const((PAIR_W, PAIR_W)),
                    const(e_lg.shape), const(e_bt.shape)],
        out_specs=[pl.BlockSpec((1, tt, GROUP_W), lambda bi, i: (bi, fwd_tile(i), 0)),
                   pl.BlockSpec((1, tt, GROUP_W), lambda bi, i: (bi, bwd_tile(i), 0)),
                   state_spec],
        out_shape=[jax.ShapeDtypeStruct((b, t, GROUP_W), F32),
                   jax.ShapeDtypeStruct((b, t, GROUP_W), F32),
                   jax.ShapeDtypeStruct((b, 2, N_PAIRS, PAIR_W, PAIR_W), F32)],
        scratch_shapes=[pltpu.VMEM((2, tt, GROUP_W), F32)] * 2 + [pltpu.VMEM((2, N_PAIRS, PAIR_W, PAIR_W), F32)],
        compiler_params=_cparams(("arbitrary", "arbitrary")),
        name="gdn_scan",
    )(qkv, ab, qkv, ab, a_log_pad, dt_bias_pad, s0,
      jnp.asarray(tri, BF16), jnp.asarray(cm), jnp.asarray(_bd_mask()),
      jnp.asarray(e_lg, BF16), jnp.asarray(e_bt, BF16))
    return o_f, o_b, s_fin


def _hgrn_prep(reverse, layer, ph_ref, logits, q_sc, k_sc, lf_sc):
    e = jnp.exp(logits - jnp.max(logits, axis=0, keepdims=True))
    prob = e / jnp.sum(e, axis=0, keepdims=True)
    lb = jnp.maximum(jnp.sum(prob[0:layer + 1], axis=0, keepdims=True) - prob[0:1], 0.0)
    z_off = 2 * GROUP_W if reverse else GROUP_W
    q_sc[...] = _silu(ph_ref[:, 0:GROUP_W]) * (HEAD_DIM ** -0.5)
    sig = _sigmoid(ph_ref[:, z_off:z_off + GROUP_W])
    lf_sc[...] = jnp.log(jnp.maximum(lb, LB_FLOOR) + (1.0 - lb) * sig)
    k_sc[...] = (1.0 - lb) * (1.0 - sig)


def _hgrn_chunk_terms(reverse, n_chunks, ph_ref, q_sc, k_sc, lf_sc, b_sc, ez_sc, tri, hm, bdm16):
    n_lvl = len(_H_LEVELS)
    q_row0 = n_lvl * CHUNK
    k_row0 = (n_lvl + 1) * CHUNK
    last = 0 if reverse else CHUNK - 1
    rows = lambda c: slice(c * CHUNK, (c + 1) * CHUNK)
    lanes = lambda p: slice(p * PAIR_W, (p + 1) * PAIR_W)
    for c in range(n_chunks):
        lf = lf_sc[rows(c), :]
        b_sc[c] = _dot_sel(tri, lf)
        for n, ex in enumerate(_hgrn_exponents(reverse, b_sc.at[c], lf)):
            ez_sc[c, n * CHUNK:(n + 1) * CHUNK, :] = jnp.exp(ex.astype(BF16))
    items = [(c, p) for c in range(n_chunks) for p in range(N_PAIRS)]
    ez = lambda row0: [ez_sc[c, row0:row0 + CHUNK, lanes(p)] for c, p in items]
    q16 = [q_sc[rows(c), lanes(p)].astype(BF16) for c, p in items]
    k16 = [k_sc[rows(c), lanes(p)].astype(BF16) for c, p in items]
    vp = [ph_ref[rows(c), 3 * GROUP_W + p * PAIR_W:3 * GROUP_W + (p + 1) * PAIR_W] for c, p in items]
    bdk = [_bd(k, bdm16) for k in k16]
    a = [_dot_nt(q, k) * hm(n_lvl) for q, k in zip(q16, bdk)]
    for li in range(n_lvl):
        e16 = ez(li * CHUNK)
        sc = [_dot_nt(q * e, k * jnp.concatenate([e, e], axis=0)) for q, k, e in zip(q16, bdk, e16)]
        a = [x + y * hm(li) for x, y in zip(a, sc)]
    o_intra = [_dot(x.astype(BF16), _bd(v.astype(BF16), bdm16)) for x, v in zip(a, vp)]
    return dict(
        o=o_intra,
        qd=[q * e for q, e in zip(q16, ez(q_row0))],
        kd=[k * e for k, e in zip(k16, ez(k_row0))],
        vt=[v.T.astype(BF16) for v in vp],
        decay=[jnp.exp(b_sc[c, last:last + 1, lanes(p)]) for c, p in items])


def _hgrn_kernel(layer, n_steps, ph_f, ph_b, lbl_ref, s0_ref, tri_ref, hm_ref, bdm_ref,
                 of_ref, ob_ref, sfin_ref, q_sc, k_sc, lf_sc, b_sc, ez_sc, s_sc):
    step = pl.program_id(1)
    tt = ph_f.shape[2]
    n_chunks = tt // CHUNK

    @pl.when(step == 0)
    def _():
        s_sc[...] = s0_ref[0]

    bdm = bdm_ref[...]
    bdm16 = bdm.astype(BF16)
    ph = (ph_f.at[0, 0], ph_b.at[0, 0])
    o_refs = (of_ref, ob_ref)
    terms = []
    for d, reverse in enumerate((False, True)):
        sc = (q_sc.at[d], k_sc.at[d], lf_sc.at[d])
        _hgrn_prep(reverse, layer, ph[d], lbl_ref[d], *sc)
        hm = functools.partial(lambda dd, idx: hm_ref[dd, idx], d)
        terms.append(_hgrn_chunk_terms(reverse, n_chunks, ph[d], *sc, b_sc.at[d], ez_sc.at[d], tri_ref[d], hm, bdm16))

    s = [[s_sc[d, p] for p in range(N_PAIRS)] for d in range(2)]
    for ci in range(n_chunks):
        lanes = [(d, p, ((n_chunks - 1 - ci) if d else ci) * N_PAIRS + p) for d in range(2) for p in range(N_PAIRS)]
        o = [terms[d]["o"][i] + _dot_nt(terms[d]["qd"][i], s[d][p].astype(BF16)) for d, p, i in lanes]
        upd = [_dot(terms[d]["vt"][i], terms[d]["kd"][i]) * bdm for d, p, i in lanes]
        for (d, p, i), x, y in zip(lanes, o, upd):
            c = i // N_PAIRS
            o_refs[d][0, 0, c * CHUNK:(c + 1) * CHUNK, p * PAIR_W:(p + 1) * PAIR_W] = x
            s[d][p] = s[d][p] * terms[d]["decay"][i] + y
    for d in range(2):
        for p in range(N_PAIRS):
            s_sc[d, p] = s[d][p]

    @pl.when(step == n_steps - 1)
    def _():
        sfin_ref[0] = s_sc[...]


def _hgrn_scan(ph, lb_logits, s0, layer):
    b, n_steps, tt, _ = ph.shape
    depth = lb_logits.shape[0]
    consts = [_hgrn_consts(rev) for rev in (False, True)]
    tri = np.stack([c[0] for c in consts])
    hm = np.stack([c[1] for c in consts])
    n_chunks = tt // CHUNK
    n_exp = len(_H_LEVELS) + 2
    const = lambda shape: pl.BlockSpec(shape, lambda bi, i: (0,) * len(shape))
    state_spec = pl.BlockSpec((1, 2, N_PAIRS, PAIR_W, PAIR_W), lambda bi, i: (bi, 0, 0, 0, 0))
    fwd = lambda w: pl.BlockSpec((1, 1, tt, w), lambda bi, i: (bi, i, 0, 0))
    bwd = lambda w: pl.BlockSpec((1, 1, tt, w), lambda bi, i: (bi, n_steps - 1 - i, 0, 0))
    return pl.pallas_call(
        functools.partial(_hgrn_kernel, layer, n_steps),
        grid=(b, n_steps),
        in_specs=[fwd(4 * GROUP_W), bwd(4 * GROUP_W), const((2, depth, GROUP_W)), state_spec,
                  const(tri.shape), const(hm.shape), const((PAIR_W, PAIR_W))],
        out_specs=[fwd(GROUP_W), bwd(GROUP_W), state_spec],
        out_shape=[jax.ShapeDtypeStruct((b, n_steps, tt, GROUP_W), F32),
                   jax.ShapeDtypeStruct((b, n_steps, tt, GROUP_W), F32),
                   jax.ShapeDtypeStruct((b, 2, N_PAIRS, PAIR_W, PAIR_W), F32)],
        scratch_shapes=[pltpu.VMEM((2, tt, GROUP_W), F32)] * 3
                       + [pltpu.VMEM((2, n_chunks, CHUNK, GROUP_W), F32),
                          pltpu.VMEM((2, n_chunks, n_exp * CHUNK, GROUP_W), BF16),
                          pltpu.VMEM((2, N_PAIRS, PAIR_W, PAIR_W), F32)],
        compiler_params=_cparams(("arbitrary", "arbitrary")),
        name="hgrn_scan",
    )(ph, ph, jnp.transpose(lb_logits, (1, 0, 2)), s0, jnp.asarray(tri, BF16), jnp.asarray(hm),
      jnp.asarray(_bd_mask()))


def _mlp_kernel(final, col_major, x_ref, oaf_ref, oab_ref, obf_ref, obb_ref, ga_ref, gb_ref, mod_ref,
                gg_ref, hg_ref, n2g_ref, fg_ref, ones_ref, perm_ref, wo_ref, w1_ref, w2_ref, o_ref):
    ones = ones_ref[...]
    tm = x_ref.shape[1]
    oa = oaf_ref[0] + oab_ref[0]
    ob = (obf_ref[0] + obb_ref[0]).reshape(tm, GROUP_W)
    gb = gb_ref[0].reshape(tm, GROUP_W)
    inv_d = 1.0 / HEAD_DIM
    ya = oa * lax.rsqrt(_head_sums(oa * oa, ones) * inv_d + EPS) * gg_ref[...] * _silu(ga_ref[0])
    yb = ob * lax.rsqrt(_head_sums(ob * ob, ones) * inv_d + EPS) * hg_ref[...] * _sigmoid(gb)
    yb = yb.astype(BF16)
    if col_major:
        yb = _dot(perm_ref[...], yb).astype(BF16)
    y = _dot(ya.astype(BF16), wo_ref[0:GROUP_W, :]) + _dot(yb, wo_ref[GROUP_W:2 * GROUP_W, :])
    x1 = x_ref[0] + mod_ref[0, 2:3, :] * y
    h = _norm_mod(x1, n2g_ref[...], mod_ref[0, 3:4, :], mod_ref[0, 4:5, :]).astype(BF16)
    hid = jnp.maximum(_dot(h, w1_ref[...]), 0.0)
    x2 = x1 + mod_ref[0, 5:6, :] * _dot((hid * hid).astype(BF16), w2_ref[...])
    if final:
        x2 = x2 * lax.rsqrt(jnp.mean(x2 * x2, axis=-1, keepdims=True) + EPS) * fg_ref[...]
    o_ref[0] = x2


def _out_mlp(x, oa_f, oa_b, ob_f, ob_b, ga, gb, mod, gdn_g, hgrn_g, n2g, final_g, w_out, w1, w2,
             shared_mod, final, col_major):
    b, t, d = x.shape
    tm = MLP_TM if col_major else min(MLP_TM, t)
    mod_map = (lambda bi, i: (0, 0, 0)) if shared_mod else (lambda bi, i: (bi, 0, 0))
    tok = lambda w: pl.BlockSpec((1, tm, w), lambda bi, i: (bi, i, 0))
    col = pl.BlockSpec((1, GRID_W, tm // GRID_W, GROUP_W), lambda bi, i: (bi, 0, i, 0))
    hg_spec = col if col_major else tok(GROUP_W)
    const = lambda shape: pl.BlockSpec(shape, lambda bi, i: (0,) * len(shape), pipeline_mode=pl.Buffered(1))
    perm_t = jnp.asarray(_grid_perm(tm).T, BF16)
    return pl.pallas_call(
        functools.partial(_mlp_kernel, final, col_major),
        grid=(b, t // tm),
        in_specs=[tok(d), tok(GROUP_W), tok(GROUP_W), hg_spec, hg_spec, tok(GROUP_W), hg_spec,
                  pl.BlockSpec((1, N_MOD, d), mod_map),
                  const((1, GROUP_W)), const((1, GROUP_W)), const((1, d)), const((1, d)),
                  const((GROUP_W // 2, GROUP_W // 2)), const(perm_t.shape),
                  const(w_out.shape), const(w1.shape), const(w2.shape)],
        out_specs=tok(d),
        out_shape=jax.ShapeDtypeStruct((b, t, d), F32),
        compiler_params=_cparams(("arbitrary", "arbitrary")),
        name="out_mlp",
    )(x, oa_f, oa_b, ob_f, ob_b, ga, gb, mod, gdn_g, hgrn_g, n2g, final_g,
      jnp.asarray(_seg_ones(GROUP_W // 2), BF16), perm_t, w_out, w1, w2)


def _split_w_in(w):
    g = GROUP_W
    qkv, ga, ab = w[:, 0:3 * g], w[:, 3 * g:4 * g], w[:, 4 * g:4 * g + 4 * N_HEADS]
    rest = w[:, 4 * g + 4 * N_HEADS:]
    ph, gb = rest[:, 0:4 * g], rest[:, 4 * g:5 * g]
    ab = jnp.pad(ab, ((0, 0), (0, AB_PAD - 4 * N_HEADS)))
    return jnp.concatenate([qkv, ga, ab, ph, gb], axis=1).astype(BF16)


def _pad_lanes(v):
    flat = v.reshape(1, -1)
    return jnp.pad(flat, ((0, 0), (0, AB_PAD - flat.shape[1])))


def kernel(x, c, ctx, c_ctx, w_mod, b_mod, norm1_g, norm2_g, w_in, conv_w, a_log, dt_bias,
           gdn_norm_g, hgrn_norm_g, lb_logits, w_out, w_mlp1, w_mlp2, final_g):
    depth = w_mod.shape[0]
    b, _, d = x.shape
    cvec = jnp.concatenate([c, c_ctx[None, :], jnp.zeros((8 - b - 1, d), F32)], axis=0)
    mod = _modulation(cvec, w_mod, b_mod).reshape(depth, 8, N_MOD, d)
    zero_state = jnp.zeros((b, 2, N_PAIRS, PAIR_W, PAIR_W), F32)
    fg = final_g.reshape(1, d)

    x_lat, x_ctx = x, ctx
    for l in range(depth):
        need_ctx = l < depth - 1
        mod_lat, mod_ctx = mod[l, 0:b], mod[l, b:b + 1]
        w_cat = _split_w_in(w_in[l])
        n1g = norm1_g[l].reshape(1, d)
        a_pad, dt_pad = _pad_lanes(a_log[l]), _pad_lanes(dt_bias[l])
        gg = jnp.tile(gdn_norm_g[l], N_HEADS).reshape(1, GROUP_W)
        hg = jnp.tile(hgrn_norm_g[l], N_HEADS).reshape(1, GROUP_W)
        wo, w1, w2 = w_out[l].astype(BF16), w_mlp1[l].astype(BF16), w_mlp2[l].astype(BF16)

        qkv_c, ga_c, ab_c, ph_c, gb_c = _projection(x_ctx, mod_ctx, n1g, w_cat, conv_w[l], True, False)
        qkv_l, ga_l, ab_l, ph_l, gb_l = _projection(x_lat, mod_lat, n1g, w_cat, conv_w[l], False, True)

        oa_cf, oa_cb, sa = _gdn_scan(qkv_c, ab_c, a_pad, dt_pad, zero_state)
        oa_lf, oa_lb, _ = _gdn_scan(qkv_l, ab_l, a_pad, dt_pad, sa)

        t_ctx = ph_c.shape[1]
        tt_c = min(HGRN_TT, t_ctx)
        ob_cf, ob_cb, sb = _hgrn_scan(ph_c.reshape(b, t_ctx // tt_c, tt_c, 4 * GROUP_W), lb_logits, zero_state, l)
        ob_lf, ob_lb, _ = _hgrn_scan(ph_l, lb_logits, sb, l)

        n2g = norm2_g[l].reshape(1, d)
        x_lat = _out_mlp(x_lat, oa_lf, oa_lb, ob_lf, ob_lb, ga_l, gb_l, mod_lat, gg, hg, n2g, fg, wo, w1, w2,
                         False, not need_ctx, True)
        if need_ctx:
            x_ctx = _out_mlp(x_ctx, oa_cf, oa_cb, ob_cf.reshape(b, t_ctx, GROUP_W), ob_cb.reshape(b, t_ctx, GROUP_W),
                             ga_c, gb_c, mod_ctx, gg, hg, n2g, fg, wo, w1, w2, True, False, False)
    return x_lat
```

```python
import functools

import numpy as np
import jax
import jax.numpy as jnp
from jax import lax
from jax.experimental import pallas as pl
from jax.experimental.pallas import tpu as pltpu

F32 = jnp.float32
BF16 = jnp.bfloat16

HEAD_DIM = 64
N_HEADS = 8
GROUP_W = N_HEADS * HEAD_DIM
N_PAIRS = N_HEADS // 2
PAIR_W = 2 * HEAD_DIM
CHUNK = 64
GRID_W = 64
CONV_W = 3
N_MOD = 6
EPS = 1e-6
LB_FLOOR = 1e-30
AB_PAD = 128
VMEM_LIMIT = 56 * 1024 * 1024

PROJ_TM = 512
MLP_TM = 512
GDN_TT = 256
GDN_NSUB = 2
HGRN_TT = 128
HGRN_NCOL = 4


def _dot(a, b):
    return jnp.dot(a, b, preferred_element_type=F32)


def _dot_nt(a, b):
    return lax.dot_general(a, b, (((1,), (1,)), ((), ())), preferred_element_type=F32)


def _dot_sel(sel, x):
    hi = x.astype(BF16)
    lo = (x - hi.astype(F32)).astype(BF16)
    return _dot(sel, hi) + _dot(sel, lo)


def _dot_sel_rhs(x, sel):
    hi = x.astype(BF16)
    lo = (x - hi.astype(F32)).astype(BF16)
    return _dot(hi, sel) + _dot(lo, sel)


def _bd(x, bdmask):
    return jnp.concatenate([x, x], axis=0) * bdmask.astype(x.dtype)


def _sigmoid(x):
    return 0.5 * jnp.tanh(0.5 * x) + 0.5


def _silu(x):
    return x * _sigmoid(x)


def _softplus(x):
    return jnp.maximum(x, 0.0) + jnp.log1p(jnp.exp(-jnp.abs(x)))


def _cparams(sem):
    return pltpu.CompilerParams(dimension_semantics=sem, vmem_limit_bytes=VMEM_LIMIT)


def _packed_ij():
    i = np.arange(CHUNK)[:, None]
    j = (np.arange(PAIR_W) % HEAD_DIM)[None, :]
    return i, j


def _bd_mask():
    r = np.arange(PAIR_W)[:, None] // HEAD_DIM
    c = np.arange(PAIR_W)[None, :] // HEAD_DIM
    return (r == c).astype(np.float32)


def _seg_ones(width):
    r = np.arange(width)[:, None] // HEAD_DIM
    c = np.arange(width)[None, :] // HEAD_DIM
    return (r == c).astype(np.float32)


_G_SUP, _G_INCL, _G_STRICT, _G_EYE, _G_B8, _G_C16, _G_C32, _G_C64 = range(8)


def _gdn_consts(reverse):
    i, j = _packed_ij()
    t = np.arange(CHUNK)
    if reverse:
        tri = (t[None, :] >= t[:, None])
        sup = i < j
        incl = j >= i
        strict = j > i
    else:
        tri = (t[None, :] <= t[:, None])
        sup = i > j
        incl = j <= i
        strict = j < i
    eye = i == j
    b8 = (i // 8) == (j // 8)
    c16 = ((i // 16) == (j // 16)) & ~b8
    c32 = ((i // 32) == (j // 32)) & ((i // 16) != (j // 16))
    c64 = (i // 32) != (j // 32)
    stack = np.stack([sup, incl, strict, eye, b8, c16, c32, c64]).astype(np.float32)
    return tri.astype(np.float32), stack


def _gdn_expand(reverse):
    d = 1 if reverse else 0
    col = np.arange(AB_PAD)[:, None]
    head = (np.arange(GROUP_W) // HEAD_DIM)[None, :]
    e_lg = (col == d * N_HEADS + head)
    e_bt = (col == 2 * N_HEADS + d * N_HEADS + head)
    return e_lg.astype(np.float32), e_bt.astype(np.float32)


_H_LEVELS = (1, 2, 4, 8, 16, 32)


def _hgrn_consts(reverse):
    r = np.arange(CHUNK)[:, None]
    t = np.arange(CHUNK)[None, :]
    i, j = _packed_ij()
    q_par, k_par = (0, 1) if reverse else (1, 0)
    masks = [((i // (2 * s)) == (j // (2 * s))) & (((i // s) % 2) == q_par) & (((j // s) % 2) == k_par)
             for s in _H_LEVELS]
    masks.append(i == j)
    tri = (t >= r) if reverse else (t <= r)
    return tri.astype(np.float32), np.stack(masks).astype(np.float32)


def _hgrn_exponents(reverse, b_ref, lf):
    w = lf.shape[1]
    rig = lax.broadcasted_iota(jnp.int32, (8, w), 0)
    grp = lambda g: b_ref[8 * g:8 * g + 8, :]
    row = lambda r: b_ref[r:r + 1, :]
    n_grp = CHUNK // 8
    q_par = 0 if reverse else 1
    sign = lambda s: jnp.where(((rig // s) % 2) == q_par, 1.0, -1.0)
    out = []
    r64 = lax.broadcasted_iota(jnp.int32, lf.shape, 0)
    out.append(jnp.where((r64 % 2) == q_par, lf, 0.0))
    bnd2 = (2, 6) if reverse else (1, 5)
    sg2, sg4 = sign(2), sign(4)
    out.append(jnp.concatenate(
        [(grp(g) - jnp.where(rig < 4, row(8 * g + bnd2[0]), row(8 * g + bnd2[1]))) * sg2 for g in range(n_grp)], axis=0))
    bnd4 = 4 if reverse else 3
    out.append(jnp.concatenate([(grp(g) - row(8 * g + bnd4)) * sg4 for g in range(n_grp)], axis=0))
    for s in (8, 16, 32):
        parts = []
        for g in range(n_grp):
            blk = (8 * g) // s
            bnd = 2 * s * (blk // 2) + (s if reverse else s - 1)
            parts.append(grp(g) - row(bnd) if (blk % 2) == q_par else row(bnd) - grp(g))
        out.append(jnp.concatenate(parts, axis=0))
    out.append(b_ref[...])
    out.append(row(0 if reverse else CHUNK - 1) - b_ref[...])
    return out


def _mod_kernel(c_ref, w_ref, b_ref, o_ref):
    sc = _silu(c_ref[...]).astype(BF16)
    o_ref[0] = _dot(sc, w_ref[0].astype(BF16)) + b_ref[0]


def _modulation(cvec, w_mod, b_mod):
    depth, d, n = w_mod.shape
    tn = 1536
    return pl.pallas_call(
        _mod_kernel,
        grid=(depth, n // tn),
        in_specs=[pl.BlockSpec((8, d), lambda l, j: (0, 0)),
                  pl.BlockSpec((1, d, tn), lambda l, j: (l, 0, j)),
                  pl.BlockSpec((1, 1, tn), lambda l, j: (l, 0, j))],
        out_specs=pl.BlockSpec((1, 8, tn), lambda l, j: (l, 0, j)),
        out_shape=jax.ShapeDtypeStruct((depth, 8, n), F32),
        compiler_params=_cparams(("arbitrary", "arbitrary")),
        name="modulation",
    )(cvec, w_mod, b_mod.reshape(depth, 1, n))


_PROJ_WIDTHS = (3 * GROUP_W, GROUP_W, AB_PAD, 4 * GROUP_W, GROUP_W)
_N_ROW_MAJOR = 3


def _norm_mod(x, g, shift, scale):
    y = x * lax.rsqrt(jnp.mean(x * x, axis=-1, keepdims=True) + EPS) * g
    return y * (1.0 + scale) + shift


def _grid_perm(tm):
    rows = tm // GRID_W
    n = np.arange(tm)
    src = (n % rows) * GRID_W + n // rows
    p = np.zeros((tm, tm), np.float32)
    p[n, src] = 1.0
    return p


def _head_sums(x, ones):
    half = ones.shape[0]
    return jnp.concatenate([_dot(x[:, 0:half].astype(BF16), ones), _dot(x[:, half:2 * half].astype(BF16), ones)],
                           axis=1)


def _proj_kernel(col_major, n_tiles, x_ref, xp_ref, xn_ref, mod_ref, g_ref, w_ref, perm_ref, convw_ref, ones_ref,
                 *out_refs):
    tile = pl.program_id(1)
    g, shift, scale = g_ref[...], mod_ref[0, 0:1, :], mod_ref[0, 1:2, :]
    h = _norm_mod(x_ref[0], g, shift, scale).astype(BF16)
    tm = h.shape[0]

    qkv_ref = out_refs[0]
    halo = jnp.concatenate([xp_ref[0], xn_ref[0]], axis=0)
    h_halo = _norm_mod(halo, g, shift, scale).astype(BF16)
    groups = [slice(n * GROUP_W, (n + 1) * GROUP_W) for n in range(3)]
    p_halo = [_dot(h_halo, w_ref[:, c]) for c in groups]
    p = [_dot(h, w_ref[:, c]) for c in groups]
    off = _PROJ_WIDTHS[0]
    for n, (ref, w) in enumerate(zip(out_refs, _PROJ_WIDTHS)):
        if n == 0:
            continue
        if n == _N_ROW_MAJOR and col_major:
            h = _dot(perm_ref[...], h).astype(BF16)
        val = _dot(h, w_ref[:, off:off + w])
        ref[0] = val.reshape(ref.shape[1:])
        off += w

    row = lax.broadcasted_iota(jnp.int32, (tm, 1), 0)
    y = []
    for c, pc, ph in zip(groups, p, p_halo):
        pv = jnp.where(tile > 0, ph[7:8, :], 0.0)
        nx = jnp.where(tile < n_tiles - 1, ph[8:9, :], 0.0)
        p_prev = jnp.where(row == 0, pv, pltpu.roll(pc, 1, axis=0))
        p_next = jnp.where(row == tm - 1, nx, pltpu.roll(pc, tm - 1, axis=0))
        y.append(_silu(convw_ref[0:1, c] * p_prev + convw_ref[1:2, c] * pc + convw_ref[2:3, c] * p_next))
    ones = ones_ref[...]
    q, k, v = y
    qkv_ref[0, :, groups[2]] = v
    qkv_ref[0, :, groups[0]] = q * lax.rsqrt(_head_sums(q * q, ones) + EPS) * (HEAD_DIM ** -0.5)
    qkv_ref[0, :, groups[1]] = k * lax.rsqrt(_head_sums(k * k, ones) + EPS)


def _projection(x, mod, g, w_cat, conv_w, shared_mod, col_major):
    b, t, d = x.shape
    tm = PROJ_TM if col_major else min(PROJ_TM, t)
    rows_t = tm // GRID_W
    n_tiles = t // tm
    hb = tm // 8
    n_hb = t // 8
    mod_map = (lambda bi, i: (0, 0, 0)) if shared_mod else (lambda bi, i: (bi, 0, 0))
    tok = lambda w: pl.BlockSpec((1, tm, w), lambda bi, i: (bi, i, 0))
    const = lambda shape: pl.BlockSpec(shape, lambda bi, i: (0,) * len(shape), pipeline_mode=pl.Buffered(1))
    out_specs = [tok(w) for w in _PROJ_WIDTHS[:_N_ROW_MAJOR]]
    out_shape = [jax.ShapeDtypeStruct((b, t, w), F32) for w in _PROJ_WIDTHS[:_N_ROW_MAJOR]]
    for w in _PROJ_WIDTHS[_N_ROW_MAJOR:]:
        if col_major:
            out_specs.append(pl.BlockSpec((1, GRID_W, rows_t, w), lambda bi, i: (bi, 0, i, 0)))
            out_shape.append(jax.ShapeDtypeStruct((b, GRID_W, t // GRID_W, w), F32))
        else:
            out_specs.append(tok(w))
            out_shape.append(jax.ShapeDtypeStruct((b, t, w), F32))
    perm = jnp.asarray(_grid_perm(tm), BF16)
    ones = jnp.asarray(_seg_ones(GROUP_W // 2), BF16)
    return pl.pallas_call(
        functools.partial(_proj_kernel, col_major, n_tiles),
        grid=(b, n_tiles),
        in_specs=[tok(d),
                  pl.BlockSpec((1, 8, d), lambda bi, i: (bi, jnp.maximum(i * hb - 1, 0), 0)),
                  pl.BlockSpec((1, 8, d), lambda bi, i: (bi, jnp.minimum((i + 1) * hb, n_hb - 1), 0)),
                  pl.BlockSpec((1, N_MOD, d), mod_map),
                  const((1, d)), const(w_cat.shape), const(perm.shape), const(conv_w.shape), const(ones.shape)],
        out_specs=out_specs,
        out_shape=out_shape,
        compiler_params=_cparams(("arbitrary", "arbitrary")),
        name="projection",
    )(x, x, x, mod, g, w_cat, perm, conv_w, ones)


def _gdn_gates(ab, alog_ref, dtb_ref, elg, ebt, lg_sc, bt_sc):
    lg_c = -jnp.exp(alog_ref[...]) * _softplus(ab + dtb_ref[...])
    lg_sc[...] = _dot_sel_rhs(lg_c, elg)
    bt_sc[...] = _dot_sel_rhs(_sigmoid(ab), ebt)


def _gdn_chunk_terms(reverse, items, qkv_ref, row0, lg_sc, bt_sc, tri, cm, bdm):
    bdm16 = bdm.astype(BF16)
    ld = lambda ref, base=0, off=0: [ref[pl.ds(base + c * CHUNK, CHUNK), off + p * PAIR_W:off + (p + 1) * PAIR_W]
                                     for c, p in items]
    qp, kp, vp = ld(qkv_ref, row0), ld(qkv_ref, row0, GROUP_W), ld(qkv_ref, row0, 2 * GROUP_W)
    bt = ld(bt_sc)
    bd16 = lambda xs: [_bd(x.astype(BF16), bdm16) for x in xs]
    pmul = lambda xs, ys: [_dot(x.astype(BF16), y) for x, y in zip(xs, bd16(ys))]

    gam_c = {c: _dot_sel(tri, lg_sc[c * CHUNK:(c + 1) * CHUNK, :]) for c in sorted({c for c, _ in items})}
    gam = [gam_c[c][:, p * PAIR_W:(p + 1) * PAIR_W] for c, p in items]
    gam_row = [jnp.sum(x * cm(_G_EYE), axis=0, keepdims=True) for x in gam]
    dincl = [jnp.exp((x - r) * cm(_G_INCL)) * cm(_G_INCL) for x, r in zip(gam, gam_row)]
    kb = [x.astype(BF16) for x in kp]
    k2 = [_bd(x, bdm16) for x in kb]
    kk = [_dot_nt(a, b) for a, b in zip(kb, k2)]
    qk = [_dot_nt(a.astype(BF16), b) for a, b in zip(qp, k2)]
    m = [a * b * d * cm(_G_STRICT) for a, b, d in zip(kk, bt, dincl)]
    n1 = [-(x * cm(_G_B8)) for x in m]
    n2 = pmul(n1, n1)
    n4 = pmul(n2, n2)
    t_inv = [cm(_G_EYE) + x for x in n1]
    t_inv = [t + d for t, d in zip(t_inv, pmul(t_inv, n2))]
    t_inv = [t + d for t, d in zip(t_inv, pmul(t_inv, n4))]
    for lvl in (_G_C16, _G_C32, _G_C64):
        left = pmul(t_inv, [x * cm(lvl) for x in m])
        t_inv = [t - d for t, d in zip(t_inv, pmul(left, t_inv))]
    eg = [jnp.exp(x) for x in gam]
    rhs = [jnp.concatenate([_bd((v * b).astype(BF16), bdm16), _bd((k * b * e).astype(BF16), bdm16)], axis=1)
           for v, k, b, e in zip(vp, kp, bt, eg)]
    uw = [_dot(t.astype(BF16), r) for t, r in zip(t_inv, rhs)]
    last = 0 if reverse else CHUNK - 1
    g_last = [x[last:last + 1, :] for x in gam]
    return dict(
        u=[x[:, 0:PAIR_W] for x in uw],
        wq=[jnp.concatenate([x[:, PAIR_W:2 * PAIR_W], q * e], axis=0).astype(BF16) for x, q, e in zip(uw, qp, eg)],
        qkd=[(a * d).astype(BF16) for a, d in zip(qk, dincl)],
        kdt=[(k * jnp.exp(g - x)).T.astype(BF16) for k, g, x in zip(kp, g_last, gam)],
        decay=[jnp.exp(g) for g in g_last])


def _gdn_kernel(n_steps, n_sub, *refs):
    (qkv_f, ab_f, qkv_b, ab_b, alog_ref, dtb_ref, s0_ref, tri_ref, cm_ref, bdm_ref, elg_ref, ebt_ref,
     of_ref, ob_ref, sfin_ref, lg_sc, bt_sc, s_sc) = refs
    step = pl.program_id(1)
    tt = qkv_f.shape[1] // n_sub
    n_chunks = tt // CHUNK

    @pl.when(step == 0)
    def _():
        s_sc[...] = s0_ref[0]

    bdm = bdm_ref[...]
    bdm16 = bdm.astype(BF16)
    tok = ((qkv_f, ab_f), (qkv_b, ab_b))
    o_refs = (of_ref, ob_ref)

    def sub_tile(j, carry):
        row0 = [pl.multiple_of(j * tt, tt), pl.multiple_of((n_sub - 1 - j) * tt, tt)]
        terms = []
        for d, reverse in enumerate((False, True)):
            qkv_ref, ab_ref = tok[d]
            _gdn_gates(ab_ref[0, pl.ds(row0[d], tt), :], alog_ref, dtb_ref, elg_ref[d], ebt_ref[d],
                       lg_sc.at[d], bt_sc.at[d])
            items = [(c, p) for c in range(n_chunks) for p in range(N_PAIRS)]
            cm = functools.partial(lambda dd, idx: cm_ref[dd, idx], d)
            terms.append(_gdn_chunk_terms(reverse, items, qkv_ref.at[0], row0[d], lg_sc.at[d], bt_sc.at[d],
                                          tri_ref[d], cm, bdm))

        s = [[s_sc[d, p] for p in range(N_PAIRS)] for d in range(2)]
        for ci in range(n_chunks):
            lanes = [(d, p, ((n_chunks - 1 - ci) if d else ci) * N_PAIRS + p)
                     for d in range(2) for p in range(N_PAIRS)]
            sq = [_dot(terms[d]["wq"][i], s[d][p].astype(BF16)) for d, p, i in lanes]
            v_new = [terms[d]["u"][i] - x[0:CHUNK] for (d, p, i), x in zip(lanes, sq)]
            v16 = [x.astype(BF16) for x in v_new]
            o = [x[CHUNK:2 * CHUNK] + _dot(terms[d]["qkd"][i], _bd(v, bdm16))
                 for (d, p, i), x, v in zip(lanes, sq, v16)]
            upd = [_dot(terms[d]["kdt"][i], v) * bdm for (d, p, i), v in zip(lanes, v16)]
            for (d, p, i), x, y in zip(lanes, o, upd):
                c = i // N_PAIRS
                o_refs[d][0, pl.ds(row0[d] + c * CHUNK, CHUNK), p * PAIR_W:(p + 1) * PAIR_W] = x
                s[d][p] = s[d][p] * terms[d]["decay"][i] + y
        for d in range(2):
            for p in range(N_PAIRS):
                s_sc[d, p] = s[d][p]
        return carry

    lax.fori_loop(0, n_sub, sub_tile, 0)

    @pl.when(step == n_steps - 1)
    def _():
        sfin_ref[0] = s_sc[...]


def _gdn_scan(qkv, ab, a_log_pad, dt_bias_pad, s0):
    b, t, _ = qkv.shape
    sub = min(GDN_TT, t)
    n_sub = min(GDN_NSUB, t // sub)
    tt = sub * n_sub
    n_tiles = t // tt
    consts = [_gdn_consts(rev) for rev in (False, True)]
    tri = np.stack([c[0] for c in consts])
    cm = np.stack([c[1] for c in consts])
    expand = [_gdn_expand(rev) for rev in (False, True)]
    e_lg = np.stack([e[0] for e in expand])
    e_bt = np.stack([e[1] for e in expand])
    const = lambda shape: pl.BlockSpec(shape, lambda bi, i: (0,) * len(shape))
    state_spec = pl.BlockSpec((1, 2, N_PAIRS, PAIR_W, PAIR_W), lambda bi, i: (bi, 0, 0, 0, 0))

    def tok_specs(tile_of):
        return [pl.BlockSpec((1, tt, 3 * GROUP_W), lambda bi, i: (bi, tile_of(i), 0)),
                pl.BlockSpec((1, tt, AB_PAD), lambda bi, i: (bi, tile_of(i), 0))]

    fwd_tile = lambda i: i
    bwd_tile = lambda i: n_tiles - 1 - i
    o_f, o_b, s_fin = pl.pallas_call(
        functools.partial(_gdn_kernel, n_tiles, n_sub),
        grid=(b, n_tiles),
        in_specs=tok_specs(fwd_tile) + tok_specs(bwd_tile)
                 + [const((1, AB_PAD)), const((1, AB_PAD)), state_spec,
                    const(tri.shape), const(cm.shape), const((PAIR_W, PAIR_W)),
                    const(e_lg.shape), const(e_bt.shape)],
        out_specs=[pl.BlockSpec((1, tt, GROUP_W), lambda bi, i: (bi, fwd_tile(i), 0)),
                   pl.BlockSpec((1, tt, GROUP_W), lambda bi, i: (bi, bwd_tile(i), 0)),
                   state_spec],
        out_shape=[jax.ShapeDtypeStruct((b, t, GROUP_W), F32),
                   jax.ShapeDtypeStruct((b, t, GROUP_W), F32),
                   jax.ShapeDtypeStruct((b, 2, N_PAIRS, PAIR_W, PAIR_W), F32)],
        scratch_shapes=[pltpu.VMEM((2, sub, GROUP_W), F32)] * 2 + [pltpu.VMEM((2, N_PAIRS, PAIR_W, PAIR_W), F32)],
        compiler_params=_cparams(("arbitrary", "arbitrary")),
        name="gdn_scan",
    )(qkv, ab, qkv, ab, a_log_pad, dt_bias_pad, s0,
      jnp.asarray(tri, BF16), jnp.asarray(cm), jnp.asarray(_bd_mask()),
      jnp.asarray(e_lg, BF16), jnp.asarray(e_bt, BF16))
    return o_f, o_b, s_fin


def _hgrn_prep(reverse, layer, ph_ref, logits, q_sc, k_sc, lf_sc):
    e = jnp.exp(logits - jnp.max(logits, axis=0, keepdims=True))
    prob = e / jnp.sum(e, axis=0, keepdims=True)
    lb = jnp.maximum(jnp.sum(prob[0:layer + 1], axis=0, keepdims=True) - prob[0:1], 0.0)
    z_off = 2 * GROUP_W if reverse else GROUP_W
    q_sc[...] = _silu(ph_ref[:, 0:GROUP_W]) * (HEAD_DIM ** -0.5)
    sig = _sigmoid(ph_ref[:, z_off:z_off + GROUP_W])
    lf_sc[...] = jnp.log(jnp.maximum(lb, LB_FLOOR) + (1.0 - lb) * sig)
    k_sc[...] = (1.0 - lb) * (1.0 - sig)


def _hgrn_chunk_terms(reverse, n_chunks, ph_ref, q_sc, k_sc, lf_sc, b_sc, ez_sc, tri, hm, bdm16):
    n_lvl = len(_H_LEVELS)
    q_row0 = n_lvl * CHUNK
    k_row0 = (n_lvl + 1) * CHUNK
    last = 0 if reverse else CHUNK - 1
    rows = lambda c: slice(c * CHUNK, (c + 1) * CHUNK)
    lanes = lambda p: slice(p * PAIR_W, (p + 1) * PAIR_W)
    for c in range(n_chunks):
        lf = lf_sc[rows(c), :]
        b_sc[c] = _dot_sel(tri, lf)
        for n, ex in enumerate(_hgrn_exponents(reverse, b_sc.at[c], lf)):
            ez_sc[c, n * CHUNK:(n + 1) * CHUNK, :] = jnp.exp(ex.astype(BF16)).astype(ez_sc.dtype)
    out = dict(o=[], qd=[], kd=[], vt=[], decay=[])
    groups = [[(c, p) for c in range(n_chunks) for p in range(N_PAIRS)]]
    for items in groups:
        ez = lambda row0: [ez_sc[c, row0:row0 + CHUNK, lanes(p)].astype(BF16) for c, p in items]
        q16 = [q_sc[rows(c), lanes(p)].astype(BF16) for c, p in items]
        k16 = [k_sc[rows(c), lanes(p)].astype(BF16) for c, p in items]
        vp = [ph_ref[rows(c), 3 * GROUP_W + p * PAIR_W:3 * GROUP_W + (p + 1) * PAIR_W] for c, p in items]
        bdk = [_bd(k, bdm16) for k in k16]
        a = [_dot_nt(q, k) * hm(n_lvl) for q, k in zip(q16, bdk)]
        for li in range(n_lvl):
            e16 = ez(li * CHUNK)
            sc = [_dot_nt(q * e, k * jnp.concatenate([e, e], axis=0)) for q, k, e in zip(q16, bdk, e16)]
            a = [x + y * hm(li) for x, y in zip(a, sc)]
        out["o"] += [_dot(x.astype(BF16), _bd(v.astype(BF16), bdm16)) for x, v in zip(a, vp)]
        out["qd"] += [q * e for q, e in zip(q16, ez(q_row0))]
        out["kd"] += [k * e for k, e in zip(k16, ez(k_row0))]
        out["vt"] += [v.T.astype(BF16) for v in vp]
        out["decay"] += [jnp.exp(b_sc[c, last:last + 1, lanes(p)]) for c, p in items]
    return out


def _hgrn_kernel(layer, n_steps, ph_f, ph_b, lbl_ref, s0_ref, tri_ref, hm_ref, bdm_ref,
                 of_ref, ob_ref, sfin_ref, q_sc, k_sc, lf_sc, b_sc, ez_sc, s_sc):
    step = pl.program_id(1)
    n_col, tt = ph_f.shape[1], ph_f.shape[2]
    n_chunks = tt // CHUNK

    @pl.when(step == 0)
    def _():
        s_sc[...] = s0_ref[0]

    bdm = bdm_ref[...]
    bdm16 = bdm.astype(BF16)
    o_refs = (of_ref, ob_ref)

    def column(j, carry):
        col = (j, n_col - 1 - j)
        ph = (ph_f.at[0, col[0]], ph_b.at[0, col[1]])
        terms = []
        for d, reverse in enumerate((False, True)):
            sc = (q_sc.at[d], k_sc.at[d], lf_sc.at[d])
            _hgrn_prep(reverse, layer, ph[d], lbl_ref[d], *sc)
            hm = functools.partial(lambda dd, idx: hm_ref[dd, idx], d)
            terms.append(_hgrn_chunk_terms(reverse, n_chunks, ph[d], *sc, b_sc.at[d], ez_sc.at[d], tri_ref[d], hm,
                                           bdm16))

        s = [[s_sc[d, p] for p in range(N_PAIRS)] for d in range(2)]
        for ci in range(n_chunks):
            lanes = [(d, p, ((n_chunks - 1 - ci) if d else ci) * N_PAIRS + p)
                     for d in range(2) for p in range(N_PAIRS)]
            o = [terms[d]["o"][i] + _dot_nt(terms[d]["qd"][i], s[d][p].astype(BF16)) for d, p, i in lanes]
            upd = [_dot(terms[d]["vt"][i], terms[d]["kd"][i]) * bdm for d, p, i in lanes]
            for (d, p, i), x, y in zip(lanes, o, upd):
                c = i // N_PAIRS
                o_refs[d][0, col[d], c * CHUNK:(c + 1) * CHUNK, p * PAIR_W:(p + 1) * PAIR_W] = x
                s[d][p] = s[d][p] * terms[d]["decay"][i] + y
        for d in range(2):
            for p in range(N_PAIRS):
                s_sc[d, p] = s[d][p]
        return carry

    lax.fori_loop(0, n_col, column, 0)

    @pl.when(step == n_steps - 1)
    def _():
        sfin_ref[0] = s_sc[...]


def _hgrn_scan(ph, lb_logits, s0, layer):
    b, n_tiles, tt, _ = ph.shape
    n_col = min(HGRN_NCOL, n_tiles)
    n_steps = n_tiles // n_col
    depth = lb_logits.shape[0]
    consts = [_hgrn_consts(rev) for rev in (False, True)]
    tri = np.stack([c[0] for c in consts])
    hm = np.stack([c[1] for c in consts])
    n_chunks = tt // CHUNK
    n_exp = len(_H_LEVELS) + 2
    const = lambda shape: pl.BlockSpec(shape, lambda bi, i: (0,) * len(shape))
    state_spec = pl.BlockSpec((1, 2, N_PAIRS, PAIR_W, PAIR_W), lambda bi, i: (bi, 0, 0, 0, 0))
    fwd = lambda w: pl.BlockSpec((1, n_col, tt, w), lambda bi, i: (bi, i, 0, 0))
    bwd = lambda w: pl.BlockSpec((1, n_col, tt, w), lambda bi, i: (bi, n_steps - 1 - i, 0, 0))
    return pl.pallas_call(
        functools.partial(_hgrn_kernel, layer, n_steps),
        grid=(b, n_steps),
        in_specs=[fwd(4 * GROUP_W), bwd(4 * GROUP_W), const((2, depth, GROUP_W)), state_spec,
                  const(tri.shape), const(hm.shape), const((PAIR_W, PAIR_W))],
        out_specs=[fwd(GROUP_W), bwd(GROUP_W), state_spec],
        out_shape=[jax.ShapeDtypeStruct((b, n_tiles, tt, GROUP_W), F32),
                   jax.ShapeDtypeStruct((b, n_tiles, tt, GROUP_W), F32),
                   jax.ShapeDtypeStruct((b, 2, N_PAIRS, PAIR_W, PAIR_W), F32)],
        scratch_shapes=[pltpu.VMEM((2, tt, GROUP_W), F32)] * 3
                       + [pltpu.VMEM((2, n_chunks, CHUNK, GROUP_W), F32),
                          pltpu.VMEM((2, n_chunks, n_exp * CHUNK, GROUP_W), F32),
                          pltpu.VMEM((2, N_PAIRS, PAIR_W, PAIR_W), F32)],
        compiler_params=_cparams(("arbitrary", "arbitrary")),
        name="hgrn_scan",
    )(ph, ph, jnp.transpose(lb_logits, (1, 0, 2)), s0, jnp.asarray(tri, BF16), jnp.asarray(hm),
      jnp.asarray(_bd_mask()))


def _mlp_kernel(final, col_major, x_ref, oaf_ref, oab_ref, obf_ref, obb_ref, ga_ref, gb_ref, mod_ref,
                gg_ref, hg_ref, n2g_ref, fg_ref, ones_ref, perm_ref, wo_ref, w1_ref, w2_ref, o_ref):
    ones = ones_ref[...]
    tm = x_ref.shape[1]
    oa = oaf_ref[0] + oab_ref[0]
    ob = (obf_ref[0] + obb_ref[0]).reshape(tm, GROUP_W)
    gb = gb_ref[0].reshape(tm, GROUP_W)
    inv_d = 1.0 / HEAD_DIM
    ya = oa * lax.rsqrt(_head_sums(oa * oa, ones) * inv_d + EPS) * gg_ref[...] * _silu(ga_ref[0])
    yb = ob * lax.rsqrt(_head_sums(ob * ob, ones) * inv_d + EPS) * hg_ref[...] * _sigmoid(gb)
    yb = yb.astype(BF16)
    if col_major:
        yb = _dot(perm_ref[...], yb).astype(BF16)
    y = _dot(ya.astype(BF16), wo_ref[0:GROUP_W, :]) + _dot(yb, wo_ref[GROUP_W:2 * GROUP_W, :])
    x1 = x_ref[0] + mod_ref[0, 2:3, :] * y
    h = _norm_mod(x1, n2g_ref[...], mod_ref[0, 3:4, :], mod_ref[0, 4:5, :]).astype(BF16)
    hid = jnp.maximum(_dot(h, w1_ref[...]), 0.0)
    x2 = x1 + mod_ref[0, 5:6, :] * _dot((hid * hid).astype(BF16), w2_ref[...])
    if final:
        x2 = x2 * lax.rsqrt(jnp.mean(x2 * x2, axis=-1, keepdims=True) + EPS) * fg_ref[...]
    o_ref[0] = x2


def _out_mlp(x, oa_f, oa_b, ob_f, ob_b, ga, gb, mod, gdn_g, hgrn_g, n2g, final_g, w_out, w1, w2,
             shared_mod, final, col_major):
    b, t, d = x.shape
    tm = MLP_TM if col_major else min(MLP_TM, t)
    mod_map = (lambda bi, i: (0, 0, 0)) if shared_mod else (lambda bi, i: (bi, 0, 0))
    tok = lambda w: pl.BlockSpec((1, tm, w), lambda bi, i: (bi, i, 0))
    col = pl.BlockSpec((1, GRID_W, tm // GRID_W, GROUP_W), lambda bi, i: (bi, 0, i, 0))
    hg_spec = col if col_major else tok(GROUP_W)
    const = lambda shape: pl.BlockSpec(shape, lambda bi, i: (0,) * len(shape), pipeline_mode=pl.Buffered(1))
    perm_t = jnp.asarray(_grid_perm(tm).T, BF16)
    return pl.pallas_call(
        functools.partial(_mlp_kernel, final, col_major),
        grid=(b, t // tm),
        in_specs=[tok(d), tok(GROUP_W), tok(GROUP_W), hg_spec, hg_spec, tok(GROUP_W), hg_spec,
                  pl.BlockSpec((1, N_MOD, d), mod_map),
                  const((1, GROUP_W)), const((1, GROUP_W)), const((1, d)), const((1, d)),
                  const((GROUP_W // 2, GROUP_W // 2)), const(perm_t.shape),
                  const(w_out.shape), const(w1.shape), const(w2.shape)],
        out_specs=tok(d),
        out_shape=jax.ShapeDtypeStruct((b, t, d), F32),
        compiler_params=_cparams(("arbitrary", "arbitrary")),
        name="out_mlp",
    )(x, oa_f, oa_b, ob_f, ob_b, ga, gb, mod, gdn_g, hgrn_g, n2g, final_g,
      jnp.asarray(_seg_ones(GROUP_W // 2), BF16), perm_t, w_out, w1, w2)


def _split_w_in(w):
    g = GROUP_W
    qkv, ga, ab = w[:, 0:3 * g], w[:, 3 * g:4 * g], w[:, 4 * g:4 * g + 4 * N_HEADS]
    rest = w[:, 4 * g + 4 * N_HEADS:]
    ph, gb = rest[:, 0:4 * g], rest[:, 4 * g:5 * g]
    ab = jnp.pad(ab, ((0, 0), (0, AB_PAD - 4 * N_HEADS)))
    return jnp.concatenate([qkv, ga, ab, ph, gb], axis=1).astype(BF16)


def _pad_lanes(v):
    flat = v.reshape(1, -1)
    return jnp.pad(flat, ((0, 0), (0, AB_PAD - flat.shape[1])))


def kernel(x, c, ctx, c_ctx, w_mod, b_mod, norm1_g, norm2_g, w_in, conv_w, a_log, dt_bias,
           gdn_norm_g, hgrn_norm_g, lb_logits, w_out, w_mlp1, w_mlp2, final_g):
    depth = w_mod.shape[0]
    b, _, d = x.shape
    cvec = jnp.concatenate([c, c_ctx[None, :], jnp.zeros((8 - b - 1, d), F32)], axis=0)
    mod = _modulation(cvec, w_mod, b_mod).reshape(depth, 8, N_MOD, d)
    zero_state = jnp.zeros((b, 2, N_PAIRS, PAIR_W, PAIR_W), F32)
    fg = final_g.reshape(1, d)

    x_lat, x_ctx = x, ctx
    for l in range(depth):
        need_ctx = l < depth - 1
        mod_lat, mod_ctx = mod[l, 0:b], mod[l, b:b + 1]
        w_cat = _split_w_in(w_in[l])
        n1g = norm1_g[l].reshape(1, d)
        a_pad, dt_pad = _pad_lanes(a_log[l]), _pad_lanes(dt_bias[l])
        gg = jnp.tile(gdn_norm_g[l], N_HEADS).reshape(1, GROUP_W)
        hg = jnp.tile(hgrn_norm_g[l], N_HEADS).reshape(1, GROUP_W)
        wo, w1, w2 = w_out[l].astype(BF16), w_mlp1[l].astype(BF16), w_mlp2[l].astype(BF16)

        qkv_c, ga_c, ab_c, ph_c, gb_c = _projection(x_ctx, mod_ctx, n1g, w_cat, conv_w[l], True, False)
        qkv_l, ga_l, ab_l, ph_l, gb_l = _projection(x_lat, mod_lat, n1g, w_cat, conv_w[l], False, True)

        oa_cf, oa_cb, sa = _gdn_scan(qkv_c, ab_c, a_pad, dt_pad, zero_state)
        oa_lf, oa_lb, _ = _gdn_scan(qkv_l, ab_l, a_pad, dt_pad, sa)

        t_ctx = ph_c.shape[1]
        tt_c = min(HGRN_TT, t_ctx)
        ob_cf, ob_cb, sb = _hgrn_scan(ph_c.reshape(b, t_ctx // tt_c, tt_c, 4 * GROUP_W), lb_logits, zero_state, l)
        ob_lf, ob_lb, _ = _hgrn_scan(ph_l, lb_logits, sb, l)

        n2g = norm2_g[l].reshape(1, d)
        x_lat = _out_mlp(x_lat, oa_lf, oa_lb, ob_lf, ob_lb, ga_l, gb_l, mod_lat, gg, hg, n2g, fg, wo, w1, w2,
                         False, not need_ctx, True)
        if need_ctx:
            x_ctx = _out_mlp(x_ctx, oa_cf, oa_cb, ob_cf.reshape(b, t_ctx, GROUP_W), ob_cb.reshape(b, t_ctx, GROUP_W),
                             ga_c, gb_c, mod_ctx, gg, hg, n2g, fg, wo, w1, w2, True, False, False)
    return x_lat
```

```python
import functools

import numpy as np
import jax
import jax.numpy as jnp
from jax import lax
from jax.experimental import pallas as pl
from jax.experimental.pallas import tpu as pltpu

F32 = jnp.float32
BF16 = jnp.bfloat16

HEAD_DIM = 64
N_HEADS = 8
GROUP_W = N_HEADS * HEAD_DIM
N_PAIRS = N_HEADS // 2
PAIR_W = 2 * HEAD_DIM
CHUNK = 64
GRID_W = 64
CONV_W = 3
N_MOD = 6
EPS = 1e-6
LB_FLOOR = 1e-30
AB_PAD = 128
VMEM_LIMIT = 56 * 1024 * 1024

PROJ_TM = 512
MLP_TM = 512
GDN_TT = 256
GDN_NSUB = 2
HGRN_TT = 128
HGRN_NCOL = 4


def _dot(a, b):
    return jnp.dot(a, b, preferred_element_type=F32)


def _dot_nt(a, b):
    return lax.dot_general(a, b, (((1,), (1,)), ((), ())), preferred_element_type=F32)


def _dot_sel(sel, x):
    hi = x.astype(BF16)
    lo = (x - hi.astype(F32)).astype(BF16)
    return _dot(sel, hi) + _dot(sel, lo)


def _dot_sel_rhs(x, sel):
    hi = x.astype(BF16)
    lo = (x - hi.astype(F32)).astype(BF16)
    return _dot(hi, sel) + _dot(lo, sel)


def _bd(x, bdmask):
    return jnp.concatenate([x, x], axis=0) * bdmask.astype(x.dtype)


def _sigmoid(x):
    return 0.5 * jnp.tanh(0.5 * x) + 0.5


def _silu(x):
    return x * _sigmoid(x)


def _softplus(x):
    return jnp.maximum(x, 0.0) + jnp.log1p(jnp.exp(-jnp.abs(x)))


def _cparams(sem):
    return pltpu.CompilerParams(dimension_semantics=sem, vmem_limit_bytes=VMEM_LIMIT)


def _packed_ij():
    i = np.arange(CHUNK)[:, None]
    j = (np.arange(PAIR_W) % HEAD_DIM)[None, :]
    return i, j


def _bd_mask():
    r = np.arange(PAIR_W)[:, None] // HEAD_DIM
    c = np.arange(PAIR_W)[None, :] // HEAD_DIM
    return (r == c).astype(np.float32)


def _seg_ones(width):
    r = np.arange(width)[:, None] // HEAD_DIM
    c = np.arange(width)[None, :] // HEAD_DIM
    return (r == c).astype(np.float32)


_G_INCL, _G_STRICT, _G_EYE, _G_NB8 = range(4)


def _gdn_consts(reverse):
    i, j = _packed_ij()
    t = np.arange(CHUNK)
    if reverse:
        tri = (t[None, :] >= t[:, None])
        incl = j >= i
        strict = j > i
    else:
        tri = (t[None, :] <= t[:, None])
        incl = j <= i
        strict = j < i
    b8 = (i // 8) == (j // 8)
    stack = np.stack([incl, strict, i == j, b8]).astype(np.float32)
    stack[_G_NB8] *= -1.0
    return tri.astype(np.float32), stack


def _gdn_level_masks():
    i, j = _packed_ij()
    b8 = (i // 8) == (j // 8)
    c16 = ((i // 16) == (j // 16)) & ~b8
    c32 = ((i // 32) == (j // 32)) & ((i // 16) != (j // 16))
    c64 = (i // 32) != (j // 32)
    bd = _bd_mask()
    lv = [-b8.astype(np.float32)] + [c.astype(np.float32) for c in (c16, c32, c64)]
    return np.stack([np.concatenate([x, x], axis=0) * bd for x in lv])


def _gdn_expand(reverse):
    d = 1 if reverse else 0
    col = np.arange(AB_PAD)[:, None]
    head = (np.arange(GROUP_W) // HEAD_DIM)[None, :]
    e_lg = (col == d * N_HEADS + head)
    e_bt = (col == 2 * N_HEADS + d * N_HEADS + head)
    return e_lg.astype(np.float32), e_bt.astype(np.float32)


_H_LEVELS = (1, 2, 4, 8, 16, 32)


def _hgrn_consts(reverse):
    r = np.arange(CHUNK)[:, None]
    t = np.arange(CHUNK)[None, :]
    i, j = _packed_ij()
    q_par, k_par = (0, 1) if reverse else (1, 0)
    masks = [((i // (2 * s)) == (j // (2 * s))) & (((i // s) % 2) == q_par) & (((j // s) % 2) == k_par)
             for s in _H_LEVELS]
    masks.append(i == j)
    tri = (t >= r) if reverse else (t <= r)
    return tri.astype(np.float32), np.stack(masks).astype(np.float32)


def _hgrn_exponents(reverse, b_ref, lf):
    w = lf.shape[1]
    rig = lax.broadcasted_iota(jnp.int32, (8, w), 0)
    grp = lambda g: b_ref[8 * g:8 * g + 8, :]
    row = lambda r: b_ref[r:r + 1, :]
    n_grp = CHUNK // 8
    q_par = 0 if reverse else 1
    sign = lambda s: jnp.where(((rig // s) % 2) == q_par, 1.0, -1.0)
    out = []
    r64 = lax.broadcasted_iota(jnp.int32, lf.shape, 0)
    out.append(jnp.where((r64 % 2) == q_par, lf, 0.0))
    bnd2 = (2, 6) if reverse else (1, 5)
    sg2, sg4 = sign(2), sign(4)
    out.append(jnp.concatenate(
        [(grp(g) - jnp.where(rig < 4, row(8 * g + bnd2[0]), row(8 * g + bnd2[1]))) * sg2 for g in range(n_grp)], axis=0))
    bnd4 = 4 if reverse else 3
    out.append(jnp.concatenate([(grp(g) - row(8 * g + bnd4)) * sg4 for g in range(n_grp)], axis=0))
    for s in (8, 16, 32):
        parts = []
        for g in range(n_grp):
            blk = (8 * g) // s
            bnd = 2 * s * (blk // 2) + (s if reverse else s - 1)
            parts.append(grp(g) - row(bnd) if (blk % 2) == q_par else row(bnd) - grp(g))
        out.append(jnp.concatenate(parts, axis=0))
    out.append(b_ref[...])
    out.append(row(0 if reverse else CHUNK - 1) - b_ref[...])
    return out


def _mod_kernel(c_ref, w_ref, b_ref, o_ref):
    sc = _silu(c_ref[...]).astype(BF16)
    o_ref[0] = _dot(sc, w_ref[0].astype(BF16)) + b_ref[0]


def _modulation(cvec, w_mod, b_mod):
    depth, d, n = w_mod.shape
    tn = 1536
    return pl.pallas_call(
        _mod_kernel,
        grid=(depth, n // tn),
        in_specs=[pl.BlockSpec((8, d), lambda l, j: (0, 0)),
                  pl.BlockSpec((1, d, tn), lambda l, j: (l, 0, j)),
                  pl.BlockSpec((1, 1, tn), lambda l, j: (l, 0, j))],
        out_specs=pl.BlockSpec((1, 8, tn), lambda l, j: (l, 0, j)),
        out_shape=jax.ShapeDtypeStruct((depth, 8, n), F32),
        compiler_params=_cparams(("arbitrary", "arbitrary")),
        name="modulation",
    )(cvec, w_mod, b_mod.reshape(depth, 1, n))


_PROJ_WIDTHS = (3 * GROUP_W, GROUP_W, AB_PAD, 4 * GROUP_W, GROUP_W)
_N_ROW_MAJOR = 3


def _norm_mod(x, g, shift, scale):
    y = x * lax.rsqrt(jnp.mean(x * x, axis=-1, keepdims=True) + EPS) * g
    return y * (1.0 + scale) + shift


def _grid_perm(tm):
    rows = tm // GRID_W
    n = np.arange(tm)
    src = (n % rows) * GRID_W + n // rows
    p = np.zeros((tm, tm), np.float32)
    p[n, src] = 1.0
    return p


def _head_sums(x, ones):
    half = ones.shape[0]
    return jnp.concatenate([_dot(x[:, 0:half].astype(BF16), ones), _dot(x[:, half:2 * half].astype(BF16), ones)],
                           axis=1)


def _proj_kernel(col_major, n_tiles, x_ref, xp_ref, xn_ref, mod_ref, g_ref, w_ref, perm_ref, convw_ref, ones_ref,
                 *out_refs):
    tile = pl.program_id(1)
    g, shift, scale = g_ref[...], mod_ref[0, 0:1, :], mod_ref[0, 1:2, :]
    h = _norm_mod(x_ref[0], g, shift, scale).astype(BF16)
    tm = h.shape[0]

    qkv_ref = out_refs[0]
    halo = jnp.concatenate([xp_ref[0], xn_ref[0]], axis=0)
    h_halo = _norm_mod(halo, g, shift, scale).astype(BF16)
    groups = [slice(n * GROUP_W, (n + 1) * GROUP_W) for n in range(3)]
    p_halo = [_dot(h_halo, w_ref[:, c]) for c in groups]
    p = [_dot(h, w_ref[:, c]) for c in groups]

    row = lax.broadcasted_iota(jnp.int32, (tm, 1), 0)
    y = []
    for c, pc, ph in zip(groups, p, p_halo):
        pv = jnp.where(tile > 0, ph[7:8, :], 0.0)
        nx = jnp.where(tile < n_tiles - 1, ph[8:9, :], 0.0)
        p_prev = jnp.where(row == 0, pv, pltpu.roll(pc, 1, axis=0))
        p_next = jnp.where(row == tm - 1, nx, pltpu.roll(pc, tm - 1, axis=0))
        y.append(_silu(convw_ref[0:1, c] * p_prev + convw_ref[1:2, c] * pc + convw_ref[2:3, c] * p_next))
    off = _PROJ_WIDTHS[0]
    for n, (ref, w) in enumerate(zip(out_refs, _PROJ_WIDTHS)):
        if n == 0:
            continue
        if n == _N_ROW_MAJOR and col_major:
            h = _dot(perm_ref[...], h).astype(BF16)
        val = _dot(h, w_ref[:, off:off + w])
        ref[0] = val.reshape(ref.shape[1:])
        off += w

    ones = ones_ref[...]
    q, k, v = y
    qkv_ref[0, :, groups[2]] = v
    qkv_ref[0, :, groups[0]] = q * lax.rsqrt(_head_sums(q * q, ones) + EPS) * (HEAD_DIM ** -0.5)
    qkv_ref[0, :, groups[1]] = k * lax.rsqrt(_head_sums(k * k, ones) + EPS)


def _projection(x, mod, g, w_cat, conv_w, shared_mod, col_major):
    b, t, d = x.shape
    tm = PROJ_TM if col_major else min(PROJ_TM, t)
    rows_t = tm // GRID_W
    n_tiles = t // tm
    hb = tm // 8
    n_hb = t // 8
    mod_map = (lambda bi, i: (0, 0, 0)) if shared_mod else (lambda bi, i: (bi, 0, 0))
    tok = lambda w: pl.BlockSpec((1, tm, w), lambda bi, i: (bi, i, 0))
    const = lambda shape: pl.BlockSpec(shape, lambda bi, i: (0,) * len(shape), pipeline_mode=pl.Buffered(1))
    out_specs = [tok(w) for w in _PROJ_WIDTHS[:_N_ROW_MAJOR]]
    out_shape = [jax.ShapeDtypeStruct((b, t, w), F32) for w in _PROJ_WIDTHS[:_N_ROW_MAJOR]]
    for w in _PROJ_WIDTHS[_N_ROW_MAJOR:]:
        if col_major:
            out_specs.append(pl.BlockSpec((1, GRID_W, rows_t, w), lambda bi, i: (bi, 0, i, 0)))
            out_shape.append(jax.ShapeDtypeStruct((b, GRID_W, t // GRID_W, w), F32))
        else:
            out_specs.append(tok(w))
            out_shape.append(jax.ShapeDtypeStruct((b, t, w), F32))
    perm = jnp.asarray(_grid_perm(tm), BF16)
    ones = jnp.asarray(_seg_ones(GROUP_W // 2), BF16)
    return pl.pallas_call(
        functools.partial(_proj_kernel, col_major, n_tiles),
        grid=(b, n_tiles),
        in_specs=[tok(d),
                  pl.BlockSpec((1, 8, d), lambda bi, i: (bi, jnp.maximum(i * hb - 1, 0), 0)),
                  pl.BlockSpec((1, 8, d), lambda bi, i: (bi, jnp.minimum((i + 1) * hb, n_hb - 1), 0)),
                  pl.BlockSpec((1, N_MOD, d), mod_map),
                  const((1, d)), const(w_cat.shape), const(perm.shape), const(conv_w.shape), const(ones.shape)],
        out_specs=out_specs,
        out_shape=out_shape,
        compiler_params=_cparams(("arbitrary", "arbitrary")),
        name="projection",
    )(x, x, x, mod, g, w_cat, perm, conv_w, ones)


def _gdn_gates(ab, alog_ref, dtb_ref, elg, ebt, lg_sc, bt_sc):
    lg_c = -jnp.exp(alog_ref[...]) * _softplus(ab + dtb_ref[...])
    lg_sc[...] = _dot_sel_rhs(lg_c, elg)
    bt_sc[...] = _dot_sel_rhs(_sigmoid(ab), ebt)


def _gdn_chunk_terms(reverse, items, qkv_ref, row0, lg_sc, bt_sc, tri, cm, bdl, bdm):
    bdm16 = bdm.astype(BF16)
    ld = lambda ref, base=0, off=0: [ref[pl.ds(base + c * CHUNK, CHUNK), off + p * PAIR_W:off + (p + 1) * PAIR_W]
                                     for c, p in items]
    qp, kp, vp = ld(qkv_ref, row0), ld(qkv_ref, row0, GROUP_W), ld(qkv_ref, row0, 2 * GROUP_W)
    bt = ld(bt_sc)
    bd16 = lambda xs: [_bd(x.astype(BF16), bdm16) for x in xs]
    pmul = lambda xs, ys: [_dot(x.astype(BF16), y) for x, y in zip(xs, bd16(ys))]

    gam_c = {c: _dot_sel(tri, lg_sc[c * CHUNK:(c + 1) * CHUNK, :]) for c in sorted({c for c, _ in items})}
    gam = [gam_c[c][:, p * PAIR_W:(p + 1) * PAIR_W] for c, p in items]
    gam_row = [jnp.sum(x * cm(_G_EYE), axis=0, keepdims=True) for x in gam]
    dincl = [jnp.exp((x - r) * cm(_G_INCL)) * cm(_G_INCL) for x, r in zip(gam, gam_row)]
    kb = [x.astype(BF16) for x in kp]
    k2 = [_bd(x, bdm16) for x in kb]
    kk = [_dot_nt(a, b) for a, b in zip(kb, k2)]
    qk = [_dot_nt(a.astype(BF16), b) for a, b in zip(qp, k2)]
    m = [a * b * d * cm(_G_STRICT) for a, b, d in zip(kk, bt, dincl)]
    m2 = [jnp.concatenate([x, x], axis=0) for x in (y.astype(BF16) for y in m)]
    n1 = [x * cm(_G_NB8) for x in m]
    n2 = [_dot(a.astype(BF16), b * bdl(0)) for a, b in zip(n1, m2)]
    n4 = pmul(n2, n2)
    t_inv = [cm(_G_EYE) + x for x in n1]
    t_inv = [t + d for t, d in zip(t_inv, pmul(t_inv, n2))]
    t_inv = [t + d for t, d in zip(t_inv, pmul(t_inv, n4))]
    for lvl in (1, 2, 3):
        left = [_dot(t.astype(BF16), b * bdl(lvl)) for t, b in zip(t_inv, m2)]
        t_inv = [t - d for t, d in zip(t_inv, pmul(left, t_inv))]
    eg = [jnp.exp(x) for x in gam]
    rhs = [jnp.concatenate([_bd((v * b).astype(BF16), bdm16), _bd((k * b * e).astype(BF16), bdm16)], axis=1)
           for v, k, b, e in zip(vp, kp, bt, eg)]
    uw = [_dot(t.astype(BF16), r) for t, r in zip(t_inv, rhs)]
    last = 0 if reverse else CHUNK - 1
    g_last = [x[last:last + 1, :] for x in gam]
    return dict(
        u=[x[:, 0:PAIR_W] for x in uw],
        wq=[jnp.concatenate([x[:, PAIR_W:2 * PAIR_W], q * e], axis=0).astype(BF16) for x, q, e in zip(uw, qp, eg)],
        qkd=[(a * d).astype(BF16) for a, d in zip(qk, dincl)],
        kdt=[(k * jnp.exp(g - x)).T.astype(BF16) for k, g, x in zip(kp, g_last, gam)],
        decay=[jnp.exp(g) for g in g_last])


def _gdn_kernel(n_steps, n_sub, *refs):
    (qkv_f, ab_f, qkv_b, ab_b, alog_ref, dtb_ref, s0_ref, tri_ref, cm_ref, bdl_ref, bdm_ref, elg_ref, ebt_ref,
     of_ref, ob_ref, sfin_ref, lg_sc, bt_sc, s_sc) = refs
    step = pl.program_id(1)
    tt = qkv_f.shape[1] // n_sub
    n_chunks = tt // CHUNK

    @pl.when(step == 0)
    def _():
        s_sc[...] = s0_ref[0]

    bdm = bdm_ref[...]
    bdm16 = bdm.astype(BF16)
    tok = ((qkv_f, ab_f), (qkv_b, ab_b))
    o_refs = (of_ref, ob_ref)

    def sub_tile(j, carry):
        row0 = [pl.multiple_of(j * tt, tt), pl.multiple_of((n_sub - 1 - j) * tt, tt)]
        terms = []
        for d, reverse in enumerate((False, True)):
            qkv_ref, ab_ref = tok[d]
            _gdn_gates(ab_ref[0, pl.ds(row0[d], tt), :], alog_ref, dtb_ref, elg_ref[d], ebt_ref[d],
                       lg_sc.at[d], bt_sc.at[d])
            items = [(c, p) for c in range(n_chunks) for p in range(N_PAIRS)]
            cm = functools.partial(lambda dd, idx: cm_ref[dd, idx], d)
            terms.append(_gdn_chunk_terms(reverse, items, qkv_ref.at[0], row0[d], lg_sc.at[d], bt_sc.at[d],
                                          tri_ref[d], cm, lambda idx: bdl_ref[idx], bdm))

        s = [[s_sc[d, p] for p in range(N_PAIRS)] for d in range(2)]
        for ci in range(n_chunks):
            lanes = [(d, p, ((n_chunks - 1 - ci) if d else ci) * N_PAIRS + p)
                     for d in range(2) for p in range(N_PAIRS)]
            sq = [_dot(terms[d]["wq"][i], s[d][p].astype(BF16)) for d, p, i in lanes]
            v_new = [terms[d]["u"][i] - x[0:CHUNK] for (d, p, i), x in zip(lanes, sq)]
            v16 = [x.astype(BF16) for x in v_new]
            o = [x[CHUNK:2 * CHUNK] + _dot(terms[d]["qkd"][i], _bd(v, bdm16))
                 for (d, p, i), x, v in zip(lanes, sq, v16)]
            upd = [_dot(terms[d]["kdt"][i], v) * bdm for (d, p, i), v in zip(lanes, v16)]
            for (d, p, i), x, y in zip(lanes, o, upd):
                c = i // N_PAIRS
                o_refs[d][0, pl.ds(row0[d] + c * CHUNK, CHUNK), p * PAIR_W:(p + 1) * PAIR_W] = x
                s[d][p] = s[d][p] * terms[d]["decay"][i] + y
        for d in range(2):
            for p in range(N_PAIRS):
                s_sc[d, p] = s[d][p]
        return carry

    lax.fori_loop(0, n_sub, sub_tile, 0)

    @pl.when(step == n_steps - 1)
    def _():
        sfin_ref[0] = s_sc[...]


def _gdn_scan(qkv, ab, a_log_pad, dt_bias_pad, s0):
    b, t, _ = qkv.shape
    sub = min(GDN_TT, t)
    n_sub = min(GDN_NSUB, t // sub)
    tt = sub * n_sub
    n_tiles = t // tt
    consts = [_gdn_consts(rev) for rev in (False, True)]
    tri = np.stack([c[0] for c in consts])
    cm = np.stack([c[1] for c in consts])
    bdl = _gdn_level_masks()
    expand = [_gdn_expand(rev) for rev in (False, True)]
    e_lg = np.stack([e[0] for e in expand])
    e_bt = np.stack([e[1] for e in expand])
    const = lambda shape: pl.BlockSpec(shape, lambda bi, i: (0,) * len(shape))
    state_spec = pl.BlockSpec((1, 2, N_PAIRS, PAIR_W, PAIR_W), lambda bi, i: (bi, 0, 0, 0, 0))

    def tok_specs(tile_of):
        return [pl.BlockSpec((1, tt, 3 * GROUP_W), lambda bi, i: (bi, tile_of(i), 0)),
                pl.BlockSpec((1, tt, AB_PAD), lambda bi, i: (bi, tile_of(i), 0))]

    fwd_tile = lambda i: i
    bwd_tile = lambda i: n_tiles - 1 - i
    o_f, o_b, s_fin = pl.pallas_call(
        functools.partial(_gdn_kernel, n_tiles, n_sub),
        grid=(b, n_tiles),
        in_specs=tok_specs(fwd_tile) + tok_specs(bwd_tile)
                 + [const((1, AB_PAD)), const((1, AB_PAD)), state_spec,
                    const(tri.shape), const(cm.shape), const(bdl.shape), const((PAIR_W, PAIR_W)),
                    const(e_lg.shape), const(e_bt.shape)],
        out_specs=[pl.BlockSpec((1, tt, GROUP_W), lambda bi, i: (bi, fwd_tile(i), 0)),
                   pl.BlockSpec((1, tt, GROUP_W), lambda bi, i: (bi, bwd_tile(i), 0)),
                   state_spec],
        out_shape=[jax.ShapeDtypeStruct((b, t, GROUP_W), F32),
                   jax.ShapeDtypeStruct((b, t, GROUP_W), F32),
                   jax.ShapeDtypeStruct((b, 2, N_PAIRS, PAIR_W, PAIR_W), F32)],
        scratch_shapes=[pltpu.VMEM((2, sub, GROUP_W), F32)] * 2 + [pltpu.VMEM((2, N_PAIRS, PAIR_W, PAIR_W), F32)],
        compiler_params=_cparams(("arbitrary", "arbitrary")),
        name="gdn_scan",
    )(qkv, ab, qkv, ab, a_log_pad, dt_bias_pad, s0,
      jnp.asarray(tri, BF16), jnp.asarray(cm), jnp.asarray(bdl, BF16), jnp.asarray(_bd_mask()),
      jnp.asarray(e_lg, BF16), jnp.asarray(e_bt, BF16))
    return o_f, o_b, s_fin


def _hgrn_prep(reverse, layer, ph_ref, logits, q_sc, k_sc, lf_sc):
    e = jnp.exp(logits - jnp.max(logits, axis=0, keepdims=True))
    prob = e / jnp.sum(e, axis=0, keepdims=True)
    lb = jnp.maximum(jnp.sum(prob[0:layer + 1], axis=0, keepdims=True) - prob[0:1], 0.0)
    z_off = 2 * GROUP_W if reverse else GROUP_W
    q_sc[...] = _silu(ph_ref[:, 0:GROUP_W]) * (HEAD_DIM ** -0.5)
    sig = _sigmoid(ph_ref[:, z_off:z_off + GROUP_W])
    lf_sc[...] = jnp.log(jnp.maximum(lb, LB_FLOOR) + (1.0 - lb) * sig)
    k_sc[...] = (1.0 - lb) * (1.0 - sig)


def _hgrn_chunk_terms(reverse, n_chunks, ph_ref, q_sc, k_sc, lf_sc, b_sc, ez_sc, tri, hm, bdm16):
    n_lvl = len(_H_LEVELS)
    q_row0 = n_lvl * CHUNK
    k_row0 = (n_lvl + 1) * CHUNK
    last = 0 if reverse else CHUNK - 1
    rows = lambda c: slice(c * CHUNK, (c + 1) * CHUNK)
    lanes = lambda p: slice(p * PAIR_W, (p + 1) * PAIR_W)
    for c in range(n_chunks):
        lf = lf_sc[rows(c), :]
        b_sc[c] = _dot_sel(tri, lf)
        for n, ex in enumerate(_hgrn_exponents(reverse, b_sc.at[c], lf)):
            ez_sc[c, n * CHUNK:(n + 1) * CHUNK, :] = jnp.exp(ex.astype(BF16)).astype(ez_sc.dtype)
    out = dict(o=[], qd=[], kd=[], vt=[], decay=[])
    groups = [[(c, p) for c in range(n_chunks) for p in range(N_PAIRS)]]
    for items in groups:
        ez = lambda row0: [ez_sc[c, row0:row0 + CHUNK, lanes(p)].astype(BF16) for c, p in items]
        q16 = [q_sc[rows(c), lanes(p)].astype(BF16) for c, p in items]
        k16 = [k_sc[rows(c), lanes(p)].astype(BF16) for c, p in items]
        vp = [ph_ref[rows(c), 3 * GROUP_W + p * PAIR_W:3 * GROUP_W + (p + 1) * PAIR_W] for c, p in items]
        bdk = [_bd(k, bdm16) for k in k16]
        a = [_dot_nt(q, k) * hm(n_lvl) for q, k in zip(q16, bdk)]
        for li in range(n_lvl):
            e16 = ez(li * CHUNK)
            sc = [_dot_nt(q * e, k * jnp.concatenate([e, e], axis=0)) for q, k, e in zip(q16, bdk, e16)]
            a = [x + y * hm(li) for x, y in zip(a, sc)]
        out["o"] += [_dot(x.astype(BF16), _bd(v.astype(BF16), bdm16)) for x, v in zip(a, vp)]
        out["qd"] += [q * e for q, e in zip(q16, ez(q_row0))]
        out["kd"] += [k * e for k, e in zip(k16, ez(k_row0))]
        out["vt"] += [v.T.astype(BF16) for v in vp]
        out["decay"] += [jnp.exp(b_sc[c, last:last + 1, lanes(p)]) for c, p in items]
    return out


def _hgrn_kernel(layer, n_steps, ph_f, ph_b, lbl_ref, s0_ref, tri_ref, hm_ref, bdm_ref,
                 of_ref, ob_ref, sfin_ref, q_sc, k_sc, lf_sc, b_sc, ez_sc, s_sc):
    step = pl.program_id(1)
    n_col, tt = ph_f.shape[1], ph_f.shape[2]
    n_chunks = tt // CHUNK

    @pl.when(step == 0)
    def _():
        s_sc[...] = s0_ref[0]

    bdm = bdm_ref[...]
    bdm16 = bdm.astype(BF16)
    o_refs = (of_ref, ob_ref)

    def column(j, carry):
        col = (j, n_col - 1 - j)
        ph = (ph_f.at[0, col[0]], ph_b.at[0, col[1]])
        terms = []
        for d, reverse in enumerate((False, True)):
            sc = (q_sc.at[d], k_sc.at[d], lf_sc.at[d])
            _hgrn_prep(reverse, layer, ph[d], lbl_ref[d], *sc)
            hm = functools.partial(lambda dd, idx: hm_ref[dd, idx], d)
            terms.append(_hgrn_chunk_terms(reverse, n_chunks, ph[d], *sc, b_sc.at[d], ez_sc.at[d], tri_ref[d], hm,
                                           bdm16))

        s = [[s_sc[d, p] for p in range(N_PAIRS)] for d in range(2)]
        for ci in range(n_chunks):
            lanes = [(d, p, ((n_chunks - 1 - ci) if d else ci) * N_PAIRS + p)
                     for d in range(2) for p in range(N_PAIRS)]
            o = [terms[d]["o"][i] + _dot_nt(terms[d]["qd"][i], s[d][p].astype(BF16)) for d, p, i in lanes]
            upd = [_dot(terms[d]["vt"][i], terms[d]["kd"][i]) * bdm for d, p, i in lanes]
            for (d, p, i), x, y in zip(lanes, o, upd):
                c = i // N_PAIRS
                o_refs[d][0, col[d], c * CHUNK:(c + 1) * CHUNK, p * PAIR_W:(p + 1) * PAIR_W] = x
                s[d][p] = s[d][p] * terms[d]["decay"][i] + y
        for d in range(2):
            for p in range(N_PAIRS):
                s_sc[d, p] = s[d][p]
        return carry

    lax.fori_loop(0, n_col, column, 0)

    @pl.when(step == n_steps - 1)
    def _():
        sfin_ref[0] = s_sc[...]


def _hgrn_scan(ph, lb_logits, s0, layer):
    b, n_tiles, tt, _ = ph.shape
    n_col = min(HGRN_NCOL, n_tiles)
    n_steps = n_tiles // n_col
    depth = lb_logits.shape[0]
    consts = [_hgrn_consts(rev) for rev in (False, True)]
    tri = np.stack([c[0] for c in consts])
    hm = np.stack([c[1] for c in consts])
    n_chunks = tt // CHUNK
    n_exp = len(_H_LEVELS) + 2
    const = lambda shape: pl.BlockSpec(shape, lambda bi, i: (0,) * len(shape))
    state_spec = pl.BlockSpec((1, 2, N_PAIRS, PAIR_W, PAIR_W), lambda bi, i: (bi, 0, 0, 0, 0))
    fwd = lambda w: pl.BlockSpec((1, n_col, tt, w), lambda bi, i: (bi, i, 0, 0))
    bwd = lambda w: pl.BlockSpec((1, n_col, tt, w), lambda bi, i: (bi, n_steps - 1 - i, 0, 0))
    return pl.pallas_call(
        functools.partial(_hgrn_kernel, layer, n_steps),
        grid=(b, n_steps),
        in_specs=[fwd(4 * GROUP_W), bwd(4 * GROUP_W), const((2, depth, GROUP_W)), state_spec,
                  const(tri.shape), const(hm.shape), const((PAIR_W, PAIR_W))],
        out_specs=[fwd(GROUP_W), bwd(GROUP_W), state_spec],
        out_shape=[jax.ShapeDtypeStruct((b, n_tiles, tt, GROUP_W), F32),
                   jax.ShapeDtypeStruct((b, n_tiles, tt, GROUP_W), F32),
                   jax.ShapeDtypeStruct((b, 2, N_PAIRS, PAIR_W, PAIR_W), F32)],
        scratch_shapes=[pltpu.VMEM((2, tt, GROUP_W), F32)] * 3
                       + [pltpu.VMEM((2, n_chunks, CHUNK, GROUP_W), F32),
                          pltpu.VMEM((2, n_chunks, n_exp * CHUNK, GROUP_W), F32),
                          pltpu.VMEM((2, N_PAIRS, PAIR_W, PAIR_W), F32)],
        compiler_params=_cparams(("arbitrary", "arbitrary")),
        name="hgrn_scan",
    )(ph, ph, jnp.transpose(lb_logits, (1, 0, 2)), s0, jnp.asarray(tri, BF16), jnp.asarray(hm),
      jnp.asarray(_bd_mask()))


def _mlp_kernel(final, col_major, x_ref, oaf_ref, oab_ref, obf_ref, obb_ref, ga_ref, gb_ref, mod_ref,
                gg_ref, hg_ref, n2g_ref, fg_ref, ones_ref, perm_ref, wo_ref, w1_ref, w2_ref, o_ref):
    ones = ones_ref[...]
    tm = x_ref.shape[1]
    oa = oaf_ref[0] + oab_ref[0]
    ob = (obf_ref[0] + obb_ref[0]).reshape(tm, GROUP_W)
    gb = gb_ref[0].reshape(tm, GROUP_W)
    inv_d = 1.0 / HEAD_DIM
    ya = oa * lax.rsqrt(_head_sums(oa * oa, ones) * inv_d + EPS) * gg_ref[...] * _silu(ga_ref[0])
    yb = ob * lax.rsqrt(_head_sums(ob * ob, ones) * inv_d + EPS) * hg_ref[...] * _sigmoid(gb)
    yb = yb.astype(BF16)
    if col_major:
        yb = _dot(perm_ref[...], yb).astype(BF16)
    y = _dot(ya.astype(BF16), wo_ref[0:GROUP_W, :]) + _dot(yb, wo_ref[GROUP_W:2 * GROUP_W, :])
    x1 = x_ref[0] + mod_ref[0, 2:3, :] * y
    h = _norm_mod(x1, n2g_ref[...], mod_ref[0, 3:4, :], mod_ref[0, 4:5, :]).astype(BF16)
    hid = jnp.maximum(_dot(h, w1_ref[...]), 0.0)
    x2 = x1 + mod_ref[0, 5:6, :] * _dot((hid * hid).astype(BF16), w2_ref[...])
    if final:
        x2 = x2 * lax.rsqrt(jnp.mean(x2 * x2, axis=-1, keepdims=True) + EPS) * fg_ref[...]
    o_ref[0] = x2


def _out_mlp(x, oa_f, oa_b, ob_f, ob_b, ga, gb, mod, gdn_g, hgrn_g, n2g, final_g, w_out, w1, w2,
             shared_mod, final, col_major):
    b, t, d = x.shape
    tm = MLP_TM if col_major else min(MLP_TM, t)
    mod_map = (lambda bi, i: (0, 0, 0)) if shared_mod else (lambda bi, i: (bi, 0, 0))
    tok = lambda w: pl.BlockSpec((1, tm, w), lambda bi, i: (bi, i, 0))
    col = pl.BlockSpec((1, GRID_W, tm // GRID_W, GROUP_W), lambda bi, i: (bi, 0, i, 0))
    hg_spec = col if col_major else tok(GROUP_W)
    const = lambda shape: pl.BlockSpec(shape, lambda bi, i: (0,) * len(shape), pipeline_mode=pl.Buffered(1))
    perm_t = jnp.asarray(_grid_perm(tm).T, BF16)
    return pl.pallas_call(
        functools.partial(_mlp_kernel, final, col_major),
        grid=(b, t // tm),
        in_specs=[tok(d), tok(GROUP_W), tok(GROUP_W), hg_spec, hg_spec, tok(GROUP_W), hg_spec,
                  pl.BlockSpec((1, N_MOD, d), mod_map),
                  const((1, GROUP_W)), const((1, GROUP_W)), const((1, d)), const((1, d)),
                  const((GROUP_W // 2, GROUP_W // 2)), const(perm_t.shape),
                  const(w_out.shape), const(w1.shape), const(w2.shape)],
        out_specs=tok(d),
        out_shape=jax.ShapeDtypeStruct((b, t, d), F32),
        compiler_params=_cparams(("arbitrary", "arbitrary")),
        name="out_mlp",
    )(x, oa_f, oa_b, ob_f, ob_b, ga, gb, mod, gdn_g, hgrn_g, n2g, final_g,
      jnp.asarray(_seg_ones(GROUP_W // 2), BF16), perm_t, w_out, w1, w2)


def _split_w_in(w):
    g = GROUP_W
    qkv, ga, ab = w[:, 0:3 * g], w[:, 3 * g:4 * g], w[:, 4 * g:4 * g + 4 * N_HEADS]
    rest = w[:, 4 * g + 4 * N_HEADS:]
    ph, gb = rest[:, 0:4 * g], rest[:, 4 * g:5 * g]
    ab = jnp.pad(ab, ((0, 0), (0, AB_PAD - 4 * N_HEADS)))
    return jnp.concatenate([qkv, ga, ab, ph, gb], axis=1).astype(BF16)


def _pad_lanes(v):
    flat = v.reshape(1, -1)
    return jnp.pad(flat, ((0, 0), (0, AB_PAD - flat.shape[1])))


def kernel(x, c, ctx, c_ctx, w_mod, b_mod, norm1_g, norm2_g, w_in, conv_w, a_log, dt_bias,
           gdn_norm_g, hgrn_norm_g, lb_logits, w_out, w_mlp1, w_mlp2, final_g):
    depth = w_mod.shape[0]
    b, _, d = x.shape
    cvec = jnp.concatenate([c, c_ctx[None, :], jnp.zeros((8 - b - 1, d), F32)], axis=0)
    mod = _modulation(cvec, w_mod, b_mod).reshape(depth, 8, N_MOD, d)
    zero_state = jnp.zeros((b, 2, N_PAIRS, PAIR_W, PAIR_W), F32)
    fg = final_g.reshape(1, d)

    x_lat, x_ctx = x, ctx
    for l in range(depth):
        need_ctx = l < depth - 1
        mod_lat, mod_ctx = mod[l, 0:b], mod[l, b:b + 1]
        w_cat = _split_w_in(w_in[l])
        n1g = norm1_g[l].reshape(1, d)
        a_pad, dt_pad = _pad_lanes(a_log[l]), _pad_lanes(dt_bias[l])
        gg = jnp.tile(gdn_norm_g[l], N_HEADS).reshape(1, GROUP_W)
        hg = jnp.tile(hgrn_norm_g[l], N_HEADS).reshape(1, GROUP_W)
        wo, w1, w2 = w_out[l].astype(BF16), w_mlp1[l].astype(BF16), w_mlp2[l].astype(BF16)

        qkv_c, ga_c, ab_c, ph_c, gb_c = _projection(x_ctx, mod_ctx, n1g, w_cat, conv_w[l], True, False)
        qkv_l, ga_l, ab_l, ph_l, gb_l = _projection(x_lat, mod_lat, n1g, w_cat, conv_w[l], False, True)

        oa_cf, oa_cb, sa = _gdn_scan(qkv_c, ab_c, a_pad, dt_pad, zero_state)
        oa_lf, oa_lb, _ = _gdn_scan(qkv_l, ab_l, a_pad, dt_pad, sa)

        t_ctx = ph_c.shape[1]
        tt_c = min(HGRN_TT, t_ctx)
        ob_cf, ob_cb, sb = _hgrn_scan(ph_c.reshape(b, t_ctx // tt_c, tt_c, 4 * GROUP_W), lb_logits, zero_state, l)
        ob_lf, ob_lb, _ = _hgrn_scan(ph_l, lb_logits, sb, l)

        n2g = norm2_g[l].reshape(1, d)
        x_lat = _out_mlp(x_lat, oa_lf, oa_lb, ob_lf, ob_lb, ga_l, gb_l, mod_lat, gg, hg, n2g, fg, wo, w1, w2,
                         False, not need_ctx, True)
        if need_ctx:
            x_ctx = _out_mlp(x_ctx, oa_cf, oa_cb, ob_cf.reshape(b, t_ctx, GROUP_W), ob_cb.reshape(b, t_ctx, GROUP_W),
                             ga_c, gb_c, mod_ctx, gg, hg, n2g, fg, wo, w1, w2, True, False, False)
    return x_lat
```

```python
import functools

import numpy as np
import jax
import jax.numpy as jnp
from jax import lax
from jax.experimental import pallas as pl
from jax.experimental.pallas import tpu as pltpu

F32 = jnp.float32
BF16 = jnp.bfloat16

HEAD_DIM = 64
N_HEADS = 8
GROUP_W = N_HEADS * HEAD_DIM
N_PAIRS = N_HEADS // 2
PAIR_W = 2 * HEAD_DIM
CHUNK = 64
GRID_W = 64
CONV_W = 3
N_MOD = 6
EPS = 1e-6
LB_FLOOR = 1e-30
AB_PAD = 128
VMEM_LIMIT = 56 * 1024 * 1024

PROJ_TM = 512
MLP_TM = 512
GDN_TT = 256
GDN_NSUB = 2
HGRN_TT = 128
HGRN_NCOL = 4


def _dot(a, b):
    return jnp.dot(a, b, preferred_element_type=F32)


def _dot_nt(a, b):
    return lax.dot_general(a, b, (((1,), (1,)), ((), ())), preferred_element_type=F32)


def _dot_sel(sel, x):
    hi = x.astype(BF16)
    lo = (x - hi.astype(F32)).astype(BF16)
    return _dot(sel, hi) + _dot(sel, lo)


def _dot_sel_rhs(x, sel):
    hi = x.astype(BF16)
    lo = (x - hi.astype(F32)).astype(BF16)
    return _dot(hi, sel) + _dot(lo, sel)


def _bd(x, bdmask):
    return jnp.concatenate([x, x], axis=0) * bdmask.astype(x.dtype)


def _sigmoid(x):
    return 0.5 * jnp.tanh(0.5 * x) + 0.5


def _silu(x):
    return x * _sigmoid(x)


def _softplus(x):
    return jnp.maximum(x, 0.0) + jnp.log1p(jnp.exp(-jnp.abs(x)))


def _cparams(sem):
    return pltpu.CompilerParams(dimension_semantics=sem, vmem_limit_bytes=VMEM_LIMIT)


def _packed_ij():
    i = np.arange(CHUNK)[:, None]
    j = (np.arange(PAIR_W) % HEAD_DIM)[None, :]
    return i, j


def _bd_mask():
    r = np.arange(PAIR_W)[:, None] // HEAD_DIM
    c = np.arange(PAIR_W)[None, :] // HEAD_DIM
    return (r == c).astype(np.float32)


def _seg_ones(width):
    r = np.arange(width)[:, None] // HEAD_DIM
    c = np.arange(width)[None, :] // HEAD_DIM
    return (r == c).astype(np.float32)


_G_INCL, _G_STRICT, _G_EYE, _G_NB8 = range(4)


def _gdn_consts(reverse):
    i, j = _packed_ij()
    t = np.arange(CHUNK)
    if reverse:
        tri = (t[None, :] >= t[:, None])
        incl = j >= i
        strict = j > i
    else:
        tri = (t[None, :] <= t[:, None])
        incl = j <= i
        strict = j < i
    b8 = (i // 8) == (j // 8)
    stack = np.stack([incl, strict, i == j, b8]).astype(np.float32)
    stack[_G_NB8] *= -1.0
    return tri.astype(np.float32), stack


def _gdn_level_masks():
    i, j = _packed_ij()
    b8 = (i // 8) == (j // 8)
    c16 = ((i // 16) == (j // 16)) & ~b8
    c32 = ((i // 32) == (j // 32)) & ((i // 16) != (j // 16))
    c64 = (i // 32) != (j // 32)
    bd = _bd_mask()
    lv = [-b8.astype(np.float32)] + [c.astype(np.float32) for c in (c16, c32, c64)]
    return np.stack([np.concatenate([x, x], axis=0) * bd for x in lv])


def _gdn_expand(reverse):
    d = 1 if reverse else 0
    col = np.arange(AB_PAD)[:, None]
    head = (np.arange(GROUP_W) // HEAD_DIM)[None, :]
    e_lg = (col == d * N_HEADS + head)
    e_bt = (col == 2 * N_HEADS + d * N_HEADS + head)
    return e_lg.astype(np.float32), e_bt.astype(np.float32)


_H_LEVELS = (1, 2, 4, 8, 16, 32)


def _hgrn_consts(reverse):
    r = np.arange(CHUNK)[:, None]
    t = np.arange(CHUNK)[None, :]
    i, j = _packed_ij()
    q_par, k_par = (0, 1) if reverse else (1, 0)
    masks = [((i // (2 * s)) == (j // (2 * s))) & (((i // s) % 2) == q_par) & (((j // s) % 2) == k_par)
             for s in _H_LEVELS]
    masks.append(i == j)
    tri = (t >= r) if reverse else (t <= r)
    return tri.astype(np.float32), np.stack(masks).astype(np.float32)


def _hgrn_exponents(reverse, b_ref, lf):
    w = lf.shape[1]
    rig = lax.broadcasted_iota(jnp.int32, (8, w), 0)
    grp = lambda g: b_ref[8 * g:8 * g + 8, :]
    row = lambda r: b_ref[r:r + 1, :]
    n_grp = CHUNK // 8
    q_par = 0 if reverse else 1
    sign = lambda s: jnp.where(((rig // s) % 2) == q_par, 1.0, -1.0)
    out = []
    r64 = lax.broadcasted_iota(jnp.int32, lf.shape, 0)
    out.append(jnp.where((r64 % 2) == q_par, lf, 0.0))
    bnd2 = (2, 6) if reverse else (1, 5)
    sg2, sg4 = sign(2), sign(4)
    out.append(jnp.concatenate(
        [(grp(g) - jnp.where(rig < 4, row(8 * g + bnd2[0]), row(8 * g + bnd2[1]))) * sg2 for g in range(n_grp)], axis=0))
    bnd4 = 4 if reverse else 3
    out.append(jnp.concatenate([(grp(g) - row(8 * g + bnd4)) * sg4 for g in range(n_grp)], axis=0))
    for s in (8, 16, 32):
        parts = []
        for g in range(n_grp):
            blk = (8 * g) // s
            bnd = 2 * s * (blk // 2) + (s if reverse else s - 1)
            parts.append(grp(g) - row(bnd) if (blk % 2) == q_par else row(bnd) - grp(g))
        out.append(jnp.concatenate(parts, axis=0))
    out.append(b_ref[...])
    out.append(row(0 if reverse else CHUNK - 1) - b_ref[...])
    return out


def _mod_kernel(c_ref, w_ref, b_ref, o_ref):
    sc = _silu(c_ref[...]).astype(BF16)
    o_ref[0] = _dot(sc, w_ref[0].astype(BF16)) + b_ref[0]


def _modulation(cvec, w_mod, b_mod):
    depth, d, n = w_mod.shape
    tn = 1536
    return pl.pallas_call(
        _mod_kernel,
        grid=(depth, n // tn),
        in_specs=[pl.BlockSpec((8, d), lambda l, j: (0, 0)),
                  pl.BlockSpec((1, d, tn), lambda l, j: (l, 0, j)),
                  pl.BlockSpec((1, 1, tn), lambda l, j: (l, 0, j))],
        out_specs=pl.BlockSpec((1, 8, tn), lambda l, j: (l, 0, j)),
        out_shape=jax.ShapeDtypeStruct((depth, 8, n), F32),
        compiler_params=_cparams(("arbitrary", "arbitrary")),
        name="modulation",
    )(cvec, w_mod, b_mod.reshape(depth, 1, n))


_PROJ_WIDTHS = (3 * GROUP_W, GROUP_W, AB_PAD, 4 * GROUP_W, GROUP_W)
_N_ROW_MAJOR = 3


def _norm_mod(x, g, shift, scale):
    y = x * lax.rsqrt(jnp.mean(x * x, axis=-1, keepdims=True) + EPS) * g
    return y * (1.0 + scale) + shift


def _grid_perm(tm):
    rows = tm // GRID_W
    n = np.arange(tm)
    src = (n % rows) * GRID_W + n // rows
    p = np.zeros((tm, tm), np.float32)
    p[n, src] = 1.0
    return p


def _head_sums(x, ones):
    half = ones.shape[0]
    return jnp.concatenate([_dot(x[:, 0:half].astype(BF16), ones), _dot(x[:, half:2 * half].astype(BF16), ones)],
                           axis=1)


def _proj_kernel(col_major, n_tiles, x_ref, xp_ref, xn_ref, mod_ref, g_ref, w_ref, perm_ref, convw_ref, ones_ref,
                 *out_refs):
    tile = pl.program_id(1)
    g, shift, scale = g_ref[...], mod_ref[0, 0:1, :], mod_ref[0, 1:2, :]
    h = _norm_mod(x_ref[0], g, shift, scale).astype(BF16)
    tm = h.shape[0]

    qkv_ref = out_refs[0]
    halo = jnp.concatenate([xp_ref[0], xn_ref[0]], axis=0)
    h_halo = _norm_mod(halo, g, shift, scale).astype(BF16)
    groups = [slice(n * GROUP_W, (n + 1) * GROUP_W) for n in range(3)]
    p_halo = [_dot(h_halo, w_ref[:, c]) for c in groups]
    p = [_dot(h, w_ref[:, c]) for c in groups]

    row = lax.broadcasted_iota(jnp.int32, (tm, 1), 0)
    y = []
    for c, pc, ph in zip(groups, p, p_halo):
        pv = jnp.where(tile > 0, ph[7:8, :], 0.0)
        nx = jnp.where(tile < n_tiles - 1, ph[8:9, :], 0.0)
        p_prev = jnp.where(row == 0, pv, pltpu.roll(pc, 1, axis=0))
        p_next = jnp.where(row == tm - 1, nx, pltpu.roll(pc, tm - 1, axis=0))
        y.append(_silu(convw_ref[0:1, c] * p_prev + convw_ref[1:2, c] * pc + convw_ref[2:3, c] * p_next))
    off = _PROJ_WIDTHS[0]
    for n, (ref, w) in enumerate(zip(out_refs, _PROJ_WIDTHS)):
        if n == 0:
            continue
        if n == _N_ROW_MAJOR and col_major:
            h = _dot(perm_ref[...], h).astype(BF16)
        val = _dot(h, w_ref[:, off:off + w])
        ref[0] = val.reshape(ref.shape[1:])
        off += w

    ones = ones_ref[...]
    q, k, v = y
    qkv_ref[0, :, groups[2]] = v
    qkv_ref[0, :, groups[0]] = q * lax.rsqrt(_head_sums(q * q, ones) + EPS) * (HEAD_DIM ** -0.5)
    qkv_ref[0, :, groups[1]] = k * lax.rsqrt(_head_sums(k * k, ones) + EPS)


def _projection(x, mod, g, w_cat, conv_w, shared_mod, col_major):
    b, t, d = x.shape
    tm = PROJ_TM if col_major else min(PROJ_TM, t)
    rows_t = tm // GRID_W
    n_tiles = t // tm
    hb = tm // 8
    n_hb = t // 8
    mod_map = (lambda bi, i: (0, 0, 0)) if shared_mod else (lambda bi, i: (bi, 0, 0))
    tok = lambda w: pl.BlockSpec((1, tm, w), lambda bi, i: (bi, i, 0))
    const = lambda shape: pl.BlockSpec(shape, lambda bi, i: (0,) * len(shape), pipeline_mode=pl.Buffered(1))
    out_specs = [tok(w) for w in _PROJ_WIDTHS[:_N_ROW_MAJOR]]
    out_shape = [jax.ShapeDtypeStruct((b, t, w), F32) for w in _PROJ_WIDTHS[:_N_ROW_MAJOR]]
    for w in _PROJ_WIDTHS[_N_ROW_MAJOR:]:
        if col_major:
            out_specs.append(pl.BlockSpec((1, GRID_W, rows_t, w), lambda bi, i: (bi, 0, i, 0)))
            out_shape.append(jax.ShapeDtypeStruct((b, GRID_W, t // GRID_W, w), F32))
        else:
            out_specs.append(tok(w))
            out_shape.append(jax.ShapeDtypeStruct((b, t, w), F32))
    perm = jnp.asarray(_grid_perm(tm), BF16)
    ones = jnp.asarray(_seg_ones(GROUP_W // 2), BF16)
    return pl.pallas_call(
        functools.partial(_proj_kernel, col_major, n_tiles),
        grid=(b, n_tiles),
        in_specs=[tok(d),
                  pl.BlockSpec((1, 8, d), lambda bi, i: (bi, jnp.maximum(i * hb - 1, 0), 0)),
                  pl.BlockSpec((1, 8, d), lambda bi, i: (bi, jnp.minimum((i + 1) * hb, n_hb - 1), 0)),
                  pl.BlockSpec((1, N_MOD, d), mod_map),
                  const((1, d)), const(w_cat.shape), const(perm.shape), const(conv_w.shape), const(ones.shape)],
        out_specs=out_specs,
        out_shape=out_shape,
        compiler_params=_cparams(("arbitrary", "arbitrary")),
        name="projection",
    )(x, x, x, mod, g, w_cat, perm, conv_w, ones)


def _gdn_gates(ab, alog_ref, dtb_ref, elg, ebt, lg_sc, bt_sc):
    lg_c = -jnp.exp(alog_ref[...]) * _softplus(ab + dtb_ref[...])
    lg_sc[...] = _dot_sel_rhs(lg_c, elg)
    bt_sc[...] = _dot_sel_rhs(_sigmoid(ab), ebt)


def _gdn_chunk_terms(reverse, items, qkv_ref, row0, lg_sc, bt_sc, tri, cm, bdl, bdm):
    bdm16 = bdm.astype(BF16)
    ld = lambda ref, base=0, off=0: [ref[pl.ds(base + c * CHUNK, CHUNK), off + p * PAIR_W:off + (p + 1) * PAIR_W]
                                     for c, p in items]
    qp, kp, vp = ld(qkv_ref, row0), ld(qkv_ref, row0, GROUP_W), ld(qkv_ref, row0, 2 * GROUP_W)
    bt = ld(bt_sc)
    bd16 = lambda xs: [_bd(x.astype(BF16), bdm16) for x in xs]
    pmul = lambda xs, ys: [_dot(x.astype(BF16), y) for x, y in zip(xs, bd16(ys))]

    gam_c = {c: _dot_sel(tri, lg_sc[c * CHUNK:(c + 1) * CHUNK, :]) for c in sorted({c for c, _ in items})}
    gam = [gam_c[c][:, p * PAIR_W:(p + 1) * PAIR_W] for c, p in items]
    gam_row = [jnp.sum(x * cm(_G_EYE), axis=0, keepdims=True) for x in gam]
    dincl = [jnp.exp((x - r) * cm(_G_INCL)) * cm(_G_INCL) for x, r in zip(gam, gam_row)]
    kb = [x.astype(BF16) for x in kp]
    k2 = [_bd(x, bdm16) for x in kb]
    kk = [_dot_nt(a, b) for a, b in zip(kb, k2)]
    qk = [_dot_nt(a.astype(BF16), b) for a, b in zip(qp, k2)]
    m = [a * b * d * cm(_G_STRICT) for a, b, d in zip(kk, bt, dincl)]
    m2 = [jnp.concatenate([x, x], axis=0) for x in (y.astype(BF16) for y in m)]
    n1 = [x * cm(_G_NB8) for x in m]
    n2 = [_dot(a.astype(BF16), b * bdl(0)) for a, b in zip(n1, m2)]
    n4 = pmul(n2, n2)
    t_inv = [cm(_G_EYE) + x for x in n1]
    t_inv = [t + d for t, d in zip(t_inv, pmul(t_inv, n2))]
    t_inv = [t + d for t, d in zip(t_inv, pmul(t_inv, n4))]
    for lvl, size in ((1, 8), (2, 16), (3, 32)):
        first = 0 if reverse else size
        picked = [slice(r, r + size) for r in range(first, CHUNK, 2 * size)]
        kept = [slice(r, r + size) for r in range(size - first, CHUNK, 2 * size)]
        rows = [jnp.concatenate([t[sl] for sl in picked], axis=0) for t in t_inv]
        left = [_dot(t.astype(BF16), b * bdl(lvl)) for t, b in zip(rows, m2)]
        rows = [t - d for t, d in zip(rows, pmul(left, t_inv))]
        pieces = lambda t, r: sorted([(sl.start, t[sl]) for sl in kept]
                                     + [(sl.start, r[n * size:(n + 1) * size]) for n, sl in enumerate(picked)],
                                     key=lambda z: z[0])
        t_inv = [jnp.concatenate([x for _, x in pieces(t, r)], axis=0) for t, r in zip(t_inv, rows)]
    eg = [jnp.exp(x) for x in gam]
    rhs = [jnp.concatenate([_bd((v * b).astype(BF16), bdm16), _bd((k * b * e).astype(BF16), bdm16)], axis=1)
           for v, k, b, e in zip(vp, kp, bt, eg)]
    uw = [_dot(t.astype(BF16), r) for t, r in zip(t_inv, rhs)]
    last = 0 if reverse else CHUNK - 1
    g_last = [x[last:last + 1, :] for x in gam]
    return dict(
        u=[x[:, 0:PAIR_W] for x in uw],
        wq=[jnp.concatenate([x[:, PAIR_W:2 * PAIR_W], q * e], axis=0).astype(BF16) for x, q, e in zip(uw, qp, eg)],
        qkd=[(a * d).astype(BF16) for a, d in zip(qk, dincl)],
        kdt=[(k * jnp.exp(g - x)).T.astype(BF16) for k, g, x in zip(kp, g_last, gam)],
        decay=[jnp.exp(g) for g in g_last])


def _gdn_kernel(n_steps, n_sub, *refs):
    (qkv_f, ab_f, qkv_b, ab_b, alog_ref, dtb_ref, s0_ref, tri_ref, cm_ref, bdl_ref, bdm_ref, elg_ref, ebt_ref,
     of_ref, ob_ref, sfin_ref, lg_sc, bt_sc, s_sc) = refs
    step = pl.program_id(1)
    tt = qkv_f.shape[1] // n_sub
    n_chunks = tt // CHUNK

    @pl.when(step == 0)
    def _():
        s_sc[...] = s0_ref[0]

    bdm = bdm_ref[...]
    bdm16 = bdm.astype(BF16)
    tok = ((qkv_f, ab_f), (qkv_b, ab_b))
    o_refs = (of_ref, ob_ref)

    def sub_tile(j, carry):
        row0 = [pl.multiple_of(j * tt, tt), pl.multiple_of((n_sub - 1 - j) * tt, tt)]
        terms = []
        for d, reverse in enumerate((False, True)):
            qkv_ref, ab_ref = tok[d]
            _gdn_gates(ab_ref[0, pl.ds(row0[d], tt), :], alog_ref, dtb_ref, elg_ref[d], ebt_ref[d],
                       lg_sc.at[d], bt_sc.at[d])
            items = [(c, p) for c in range(n_chunks) for p in range(N_PAIRS)]
            cm = functools.partial(lambda dd, idx: cm_ref[dd, idx], d)
            terms.append(_gdn_chunk_terms(reverse, items, qkv_ref.at[0], row0[d], lg_sc.at[d], bt_sc.at[d],
                                          tri_ref[d], cm, lambda idx: bdl_ref[idx], bdm))

        s = [[s_sc[d, p] for p in range(N_PAIRS)] for d in range(2)]
        for ci in range(n_chunks):
            lanes = [(d, p, ((n_chunks - 1 - ci) if d else ci) * N_PAIRS + p)
                     for d in range(2) for p in range(N_PAIRS)]
            sq = [_dot(terms[d]["wq"][i], s[d][p].astype(BF16)) for d, p, i in lanes]
            v_new = [terms[d]["u"][i] - x[0:CHUNK] for (d, p, i), x in zip(lanes, sq)]
            v16 = [x.astype(BF16) for x in v_new]
            o = [x[CHUNK:2 * CHUNK] + _dot(terms[d]["qkd"][i], _bd(v, bdm16))
                 for (d, p, i), x, v in zip(lanes, sq, v16)]
            upd = [_dot(terms[d]["kdt"][i], v) * bdm for (d, p, i), v in zip(lanes, v16)]
            for (d, p, i), x, y in zip(lanes, o, upd):
                c = i // N_PAIRS
                o_refs[d][0, pl.ds(row0[d] + c * CHUNK, CHUNK), p * PAIR_W:(p + 1) * PAIR_W] = x
                s[d][p] = s[d][p] * terms[d]["decay"][i] + y
        for d in range(2):
            for p in range(N_PAIRS):
                s_sc[d, p] = s[d][p]
        return carry

    lax.fori_loop(0, n_sub, sub_tile, 0)

    @pl.when(step == n_steps - 1)
    def _():
        sfin_ref[0] = s_sc[...]


def _gdn_scan(qkv, ab, a_log_pad, dt_bias_pad, s0):
    b, t, _ = qkv.shape
    sub = min(GDN_TT, t)
    n_sub = min(GDN_NSUB, t // sub)
    tt = sub * n_sub
    n_tiles = t // tt
    consts = [_gdn_consts(rev) for rev in (False, True)]
    tri = np.stack([c[0] for c in consts])
    cm = np.stack([c[1] for c in consts])
    bdl = _gdn_level_masks()
    expand = [_gdn_expand(rev) for rev in (False, True)]
    e_lg = np.stack([e[0] for e in expand])
    e_bt = np.stack([e[1] for e in expand])
    const = lambda shape: pl.BlockSpec(shape, lambda bi, i: (0,) * len(shape))
    state_spec = pl.BlockSpec((1, 2, N_PAIRS, PAIR_W, PAIR_W), lambda bi, i: (bi, 0, 0, 0, 0))

    def tok_specs(tile_of):
        return [pl.BlockSpec((1, tt, 3 * GROUP_W), lambda bi, i: (bi, tile_of(i), 0)),
                pl.BlockSpec((1, tt, AB_PAD), lambda bi, i: (bi, tile_of(i), 0))]

    fwd_tile = lambda i: i
    bwd_tile = lambda i: n_tiles - 1 - i
    o_f, o_b, s_fin = pl.pallas_call(
        functools.partial(_gdn_kernel, n_tiles, n_sub),
        grid=(b, n_tiles),
        in_specs=tok_specs(fwd_tile) + tok_specs(bwd_tile)
                 + [const((1, AB_PAD)), const((1, AB_PAD)), state_spec,
                    const(tri.shape), const(cm.shape), const(bdl.shape), const((PAIR_W, PAIR_W)),
                    const(e_lg.shape), const(e_bt.shape)],
        out_specs=[pl.BlockSpec((1, tt, GROUP_W), lambda bi, i: (bi, fwd_tile(i), 0)),
                   pl.BlockSpec((1, tt, GROUP_W), lambda bi, i: (bi, bwd_tile(i), 0)),
                   state_spec],
        out_shape=[jax.ShapeDtypeStruct((b, t, GROUP_W), F32),
                   jax.ShapeDtypeStruct((b, t, GROUP_W), F32),
                   jax.ShapeDtypeStruct((b, 2, N_PAIRS, PAIR_W, PAIR_W), F32)],
        scratch_shapes=[pltpu.VMEM((2, sub, GROUP_W), F32)] * 2 + [pltpu.VMEM((2, N_PAIRS, PAIR_W, PAIR_W), F32)],
        compiler_params=_cparams(("arbitrary", "arbitrary")),
        name="gdn_scan",
    )(qkv, ab, qkv, ab, a_log_pad, dt_bias_pad, s0,
      jnp.asarray(tri, BF16), jnp.asarray(cm), jnp.asarray(bdl, BF16), jnp.asarray(_bd_mask()),
      jnp.asarray(e_lg, BF16), jnp.asarray(e_bt, BF16))
    return o_f, o_b, s_fin


def _hgrn_prep(reverse, layer, ph_ref, logits, q_sc, k_sc, lf_sc):
    e = jnp.exp(logits - jnp.max(logits, axis=0, keepdims=True))
    prob = e / jnp.sum(e, axis=0, keepdims=True)
    lb = jnp.maximum(jnp.sum(prob[0:layer + 1], axis=0, keepdims=True) - prob[0:1], 0.0)
    z_off = 2 * GROUP_W if reverse else GROUP_W
    q_sc[...] = _silu(ph_ref[:, 0:GROUP_W]) * (HEAD_DIM ** -0.5)
    sig = _sigmoid(ph_ref[:, z_off:z_off + GROUP_W])
    lf_sc[...] = jnp.log(jnp.maximum(lb, LB_FLOOR) + (1.0 - lb) * sig)
    k_sc[...] = (1.0 - lb) * (1.0 - sig)


def _hgrn_chunk_terms(reverse, n_chunks, ph_ref, q_sc, k_sc, lf_sc, b_sc, ez_sc, tri, hm, bdm16):
    n_lvl = len(_H_LEVELS)
    q_row0 = n_lvl * CHUNK
    k_row0 = (n_lvl + 1) * CHUNK
    last = 0 if reverse else CHUNK - 1
    rows = lambda c: slice(c * CHUNK, (c + 1) * CHUNK)
    lanes = lambda p: slice(p * PAIR_W, (p + 1) * PAIR_W)
    for c in range(n_chunks):
        lf = lf_sc[rows(c), :]
        b_sc[c] = _dot_sel(tri, lf)
        for n, ex in enumerate(_hgrn_exponents(reverse, b_sc.at[c], lf)):
            ez_sc[c, n * CHUNK:(n + 1) * CHUNK, :] = jnp.exp(ex.astype(BF16)).astype(ez_sc.dtype)
    out = dict(o=[], qd=[], kd=[], vt=[], decay=[])
    groups = [[(c, p) for c in range(n_chunks) for p in range(N_PAIRS)]]
    for items in groups:
        ez = lambda row0: [ez_sc[c, row0:row0 + CHUNK, lanes(p)].astype(BF16) for c, p in items]
        q16 = [q_sc[rows(c), lanes(p)].astype(BF16) for c, p in items]
        k16 = [k_sc[rows(c), lanes(p)].astype(BF16) for c, p in items]
        vp = [ph_ref[rows(c), 3 * GROUP_W + p * PAIR_W:3 * GROUP_W + (p + 1) * PAIR_W] for c, p in items]
        bdk = [_bd(k, bdm16) for k in k16]
        a = [_dot_nt(q, k) * hm(n_lvl) for q, k in zip(q16, bdk)]
        for li in range(n_lvl):
            e16 = ez(li * CHUNK)
            sc = [_dot_nt(q * e, k * jnp.concatenate([e, e], axis=0)) for q, k, e in zip(q16, bdk, e16)]
            a = [x + y * hm(li) for x, y in zip(a, sc)]
        out["o"] += [_dot(x.astype(BF16), _bd(v.astype(BF16), bdm16)) for x, v in zip(a, vp)]
        out["qd"] += [q * e for q, e in zip(q16, ez(q_row0))]
        out["kd"] += [k * e for k, e in zip(k16, ez(k_row0))]
        out["vt"] += [v.T.astype(BF16) for v in vp]
        out["decay"] += [jnp.exp(b_sc[c, last:last + 1, lanes(p)]) for c, p in items]
    return out


def _hgrn_kernel(layer, n_steps, ph_f, ph_b, lbl_ref, s0_ref, tri_ref, hm_ref, bdm_ref,
                 of_ref, ob_ref, sfin_ref, q_sc, k_sc, lf_sc, b_sc, ez_sc, s_sc):
    step = pl.program_id(1)
    n_col, tt = ph_f.shape[1], ph_f.shape[2]
    n_chunks = tt // CHUNK

    @pl.when(step == 0)
    def _():
        s_sc[...] = s0_ref[0]

    bdm = bdm_ref[...]
    bdm16 = bdm.astype(BF16)
    o_refs = (of_ref, ob_ref)

    def column(j, carry):
        col = (j, n_col - 1 - j)
        ph = (ph_f.at[0, col[0]], ph_b.at[0, col[1]])
        terms = []
        for d, reverse in enumerate((False, True)):
            sc = (q_sc.at[d], k_sc.at[d], lf_sc.at[d])
            _hgrn_prep(reverse, layer, ph[d], lbl_ref[d], *sc)
            hm = functools.partial(lambda dd, idx: hm_ref[dd, idx], d)
            terms.append(_hgrn_chunk_terms(reverse, n_chunks, ph[d], *sc, b_sc.at[d], ez_sc.at[d], tri_ref[d], hm,
                                           bdm16))

        s = [[s_sc[d, p] for p in range(N_PAIRS)] for d in range(2)]
        for ci in range(n_chunks):
            lanes = [(d, p, ((n_chunks - 1 - ci) if d else ci) * N_PAIRS + p)
                     for d in range(2) for p in range(N_PAIRS)]
            o = [terms[d]["o"][i] + _dot_nt(terms[d]["qd"][i], s[d][p].astype(BF16)) for d, p, i in lanes]
            upd = [_dot(terms[d]["vt"][i], terms[d]["kd"][i]) * bdm for d, p, i in lanes]
            for (d, p, i), x, y in zip(lanes, o, upd):
                c = i // N_PAIRS
                o_refs[d][0, col[d], c * CHUNK:(c + 1) * CHUNK, p * PAIR_W:(p + 1) * PAIR_W] = x
                s[d][p] = s[d][p] * terms[d]["decay"][i] + y
        for d in range(2):
            for p in range(N_PAIRS):
                s_sc[d, p] = s[d][p]
        return carry

    lax.fori_loop(0, n_col, column, 0)

    @pl.when(step == n_steps - 1)
    def _():
        sfin_ref[0] = s_sc[...]


def _hgrn_scan(ph, lb_logits, s0, layer):
    b, n_tiles, tt, _ = ph.shape
    n_col = min(HGRN_NCOL, n_tiles)
    n_steps = n_tiles // n_col
    depth = lb_logits.shape[0]
    consts = [_hgrn_consts(rev) for rev in (False, True)]
    tri = np.stack([c[0] for c in consts])
    hm = np.stack([c[1] for c in consts])
    n_chunks = tt // CHUNK
    n_exp = len(_H_LEVELS) + 2
    const = lambda shape: pl.BlockSpec(shape, lambda bi, i: (0,) * len(shape))
    state_spec = pl.BlockSpec((1, 2, N_PAIRS, PAIR_W, PAIR_W), lambda bi, i: (bi, 0, 0, 0, 0))
    fwd = lambda w: pl.BlockSpec((1, n_col, tt, w), lambda bi, i: (bi, i, 0, 0))
    bwd = lambda w: pl.BlockSpec((1, n_col, tt, w), lambda bi, i: (bi, n_steps - 1 - i, 0, 0))
    return pl.pallas_call(
        functools.partial(_hgrn_kernel, layer, n_steps),
        grid=(b, n_steps),
        in_specs=[fwd(4 * GROUP_W), bwd(4 * GROUP_W), const((2, depth, GROUP_W)), state_spec,
                  const(tri.shape), const(hm.shape), const((PAIR_W, PAIR_W))],
        out_specs=[fwd(GROUP_W), bwd(GROUP_W), state_spec],
        out_shape=[jax.ShapeDtypeStruct((b, n_tiles, tt, GROUP_W), F32),
                   jax.ShapeDtypeStruct((b, n_tiles, tt, GROUP_W), F32),
                   jax.ShapeDtypeStruct((b, 2, N_PAIRS, PAIR_W, PAIR_W), F32)],
        scratch_shapes=[pltpu.VMEM((2, tt, GROUP_W), F32)] * 3
                       + [pltpu.VMEM((2, n_chunks, CHUNK, GROUP_W), F32),
                          pltpu.VMEM((2, n_chunks, n_exp * CHUNK, GROUP_W), F32),
                          pltpu.VMEM((2, N_PAIRS, PAIR_W, PAIR_W), F32)],
        compiler_params=_cparams(("arbitrary", "arbitrary")),
        name="hgrn_scan",
    )(ph, ph, jnp.transpose(lb_logits, (1, 0, 2)), s0, jnp.asarray(tri, BF16), jnp.asarray(hm),
      jnp.asarray(_bd_mask()))


def _mlp_kernel(final, col_major, x_ref, oaf_ref, oab_ref, obf_ref, obb_ref, ga_ref, gb_ref, mod_ref,
                gg_ref, hg_ref, n2g_ref, fg_ref, ones_ref, perm_ref, wo_ref, w1_ref, w2_ref, o_ref):
    ones = ones_ref[...]
    tm = x_ref.shape[1]
    oa = oaf_ref[0] + oab_ref[0]
    ob = (obf_ref[0] + obb_ref[0]).reshape(tm, GROUP_W)
    gb = gb_ref[0].reshape(tm, GROUP_W)
    inv_d = 1.0 / HEAD_DIM
    ya = oa * lax.rsqrt(_head_sums(oa * oa, ones) * inv_d + EPS) * gg_ref[...] * _silu(ga_ref[0])
    yb = ob * lax.rsqrt(_head_sums(ob * ob, ones) * inv_d + EPS) * hg_ref[...] * _sigmoid(gb)
    yb = yb.astype(BF16)
    if col_major:
        yb = _dot(perm_ref[...], yb).astype(BF16)
    y = _dot(ya.astype(BF16), wo_ref[0:GROUP_W, :]) + _dot(yb, wo_ref[GROUP_W:2 * GROUP_W, :])
    x1 = x_ref[0] + mod_ref[0, 2:3, :] * y
    h = _norm_mod(x1, n2g_ref[...], mod_ref[0, 3:4, :], mod_ref[0, 4:5, :]).astype(BF16)
    hid = jnp.maximum(_dot(h, w1_ref[...]), 0.0)
    x2 = x1 + mod_ref[0, 5:6, :] * _dot((hid * hid).astype(BF16), w2_ref[...])
    if final:
        x2 = x2 * lax.rsqrt(jnp.mean(x2 * x2, axis=-1, keepdims=True) + EPS) * fg_ref[...]
    o_ref[0] = x2


def _out_mlp(x, oa_f, oa_b, ob_f, ob_b, ga, gb, mod, gdn_g, hgrn_g, n2g, final_g, w_out, w1, w2,
             shared_mod, final, col_major):
    b, t, d = x.shape
    tm = MLP_TM if col_major else min(MLP_TM, t)
    mod_map = (lambda bi, i: (0, 0, 0)) if shared_mod else (lambda bi, i: (bi, 0, 0))
    tok = lambda w: pl.BlockSpec((1, tm, w), lambda bi, i: (bi, i, 0))
    col = pl.BlockSpec((1, GRID_W, tm // GRID_W, GROUP_W), lambda bi, i: (bi, 0, i, 0))
    hg_spec = col if col_major else tok(GROUP_W)
    const = lambda shape: pl.BlockSpec(shape, lambda bi, i: (0,) * len(shape), pipeline_mode=pl.Buffered(1))
    perm_t = jnp.asarray(_grid_perm(tm).T, BF16)
    return pl.pallas_call(
        functools.partial(_mlp_kernel, final, col_major),
        grid=(b, t // tm),
        in_specs=[tok(d), tok(GROUP_W), tok(GROUP_W), hg_spec, hg_spec, tok(GROUP_W), hg_spec,
                  pl.BlockSpec((1, N_MOD, d), mod_map),
                  const((1, GROUP_W)), const((1, GROUP_W)), const((1, d)), const((1, d)),
                  const((GROUP_W // 2, GROUP_W // 2)), const(perm_t.shape),
                  const(w_out.shape), const(w1.shape), const(w2.shape)],
        out_specs=tok(d),
        out_shape=jax.ShapeDtypeStruct((b, t, d), F32),
        compiler_params=_cparams(("arbitrary", "arbitrary")),
        name="out_mlp",
    )(x, oa_f, oa_b, ob_f, ob_b, ga, gb, mod, gdn_g, hgrn_g, n2g, final_g,
      jnp.asarray(_seg_ones(GROUP_W // 2), BF16), perm_t, w_out, w1, w2)


def _split_w_in(w):
    g = GROUP_W
    qkv, ga, ab = w[:, 0:3 * g], w[:, 3 * g:4 * g], w[:, 4 * g:4 * g + 4 * N_HEADS]
    rest = w[:, 4 * g + 4 * N_HEADS:]
    ph, gb = rest[:, 0:4 * g], rest[:, 4 * g:5 * g]
    ab = jnp.pad(ab, ((0, 0), (0, AB_PAD - 4 * N_HEADS)))
    return jnp.concatenate([qkv, ga, ab, ph, gb], axis=1).astype(BF16)


def _pad_lanes(v):
    flat = v.reshape(1, -1)
    return jnp.pad(flat, ((0, 0), (0, AB_PAD - flat.shape[1])))


def kernel(x, c, ctx, c_ctx, w_mod, b_mod, norm1_g, norm2_g, w_in, conv_w, a_log, dt_bias,
           gdn_norm_g, hgrn_norm_g, lb_logits, w_out, w_mlp1, w_mlp2, final_g):
    depth = w_mod.shape[0]
    b, _, d = x.shape
    cvec = jnp.concatenate([c, c_ctx[None, :], jnp.zeros((8 - b - 1, d), F32)], axis=0)
    mod = _modulation(cvec, w_mod, b_mod).reshape(depth, 8, N_MOD, d)
    zero_state = jnp.zeros((b, 2, N_PAIRS, PAIR_W, PAIR_W), F32)
    fg = final_g.reshape(1, d)

    x_lat, x_ctx = x, ctx
    for l in range(depth):
        need_ctx = l < depth - 1
        mod_lat, mod_ctx = mod[l, 0:b], mod[l, b:b + 1]
        w_cat = _split_w_in(w_in[l])
        n1g = norm1_g[l].reshape(1, d)
        a_pad, dt_pad = _pad_lanes(a_log[l]), _pad_lanes(dt_bias[l])
        gg = jnp.tile(gdn_norm_g[l], N_HEADS).reshape(1, GROUP_W)
        hg = jnp.tile(hgrn_norm_g[l], N_HEADS).reshape(1, GROUP_W)
        wo, w1, w2 = w_out[l].astype(BF16), w_mlp1[l].astype(BF16), w_mlp2[l].astype(BF16)

        qkv_c, ga_c, ab_c, ph_c, gb_c = _projection(x_ctx, mod_ctx, n1g, w_cat, conv_w[l], True, False)
        qkv_l, ga_l, ab_l, ph_l, gb_l = _projection(x_lat, mod_lat, n1g, w_cat, conv_w[l], False, True)

        oa_cf, oa_cb, sa = _gdn_scan(qkv_c, ab_c, a_pad, dt_pad, zero_state)
        oa_lf, oa_lb, _ = _gdn_scan(qkv_l, ab_l, a_pad, dt_pad, sa)

        t_ctx = ph_c.shape[1]
        tt_c = min(HGRN_TT, t_ctx)
        ob_cf, ob_cb, sb = _hgrn_scan(ph_c.reshape(b, t_ctx // tt_c, tt_c, 4 * GROUP_W), lb_logits, zero_state, l)
        ob_lf, ob_lb, _ = _hgrn_scan(ph_l, lb_logits, sb, l)

        n2g = norm2_g[l].reshape(1, d)
        x_lat = _out_mlp(x_lat, oa_lf, oa_lb, ob_lf, ob_lb, ga_l, gb_l, mod_lat, gg, hg, n2g, fg, wo, w1, w2,
                         False, not need_ctx, True)
        if need_ctx:
            x_ctx = _out_mlp(x_ctx, oa_cf, oa_cb, ob_cf.reshape(b, t_ctx, GROUP_W), ob_cb.reshape(b, t_ctx, GROUP_W),
                             ga_c, gb_c, mod_ctx, gg, hg, n2g, fg, wo, w1, w2, True, False, False)
    return x_lat
```

```python
import functools

import numpy as np
import jax
import jax.numpy as jnp
from jax import lax
from jax.experimental import pallas as pl
from jax.experimental.pallas import tpu as pltpu

F32 = jnp.float32
BF16 = jnp.bfloat16

SUBLANES = 8
HEAD_DIM = 64
N_HEADS = 8
GROUP_W = N_HEADS * HEAD_DIM
N_PAIRS = N_HEADS // 2
PAIR_W = 2 * HEAD_DIM
CHUNK = 64
GRID_W = 64
CONV_W = 3
N_MOD = 6
EPS = 1e-6
LB_FLOOR = 1e-30
AB_PAD = 128
VMEM_LIMIT = 56 * 1024 * 1024

PROJ_TM = 512
MLP_TM = 512
GDN_TT = 256
GDN_NSUB = 4
HGRN_TT = 128
HGRN_NCOL = 8


def _dot(a, b):
    return jnp.dot(a, b, preferred_element_type=F32)


def _dot_nt(a, b):
    return lax.dot_general(a, b, (((1,), (1,)), ((), ())), preferred_element_type=F32)


def _dot_sel(sel, x):
    hi = x.astype(BF16)
    lo = (x - hi.astype(F32)).astype(BF16)
    return _dot(sel, hi) + _dot(sel, lo)


def _dot_sel_rhs(x, sel):
    hi = x.astype(BF16)
    lo = (x - hi.astype(F32)).astype(BF16)
    return _dot(hi, sel) + _dot(lo, sel)


def _bd(x, bdmask):
    return jnp.concatenate([x, x], axis=0) * bdmask.astype(x.dtype)


def _sigmoid(x):
    return 0.5 * jnp.tanh(0.5 * x) + 0.5


def _silu(x):
    return x * _sigmoid(x)


def _softplus(x):
    return jnp.maximum(x, 0.0) + jnp.log1p(jnp.exp(-jnp.abs(x)))


def _cparams(sem):
    return pltpu.CompilerParams(dimension_semantics=sem, vmem_limit_bytes=VMEM_LIMIT)


def _packed_ij():
    i = np.arange(CHUNK)[:, None]
    j = (np.arange(PAIR_W) % HEAD_DIM)[None, :]
    return i, j


def _bd_mask():
    r = np.arange(PAIR_W)[:, None] // HEAD_DIM
    c = np.arange(PAIR_W)[None, :] // HEAD_DIM
    return (r == c).astype(np.float32)


def _seg_ones(width):
    r = np.arange(width)[:, None] // HEAD_DIM
    c = np.arange(width)[None, :] // HEAD_DIM
    return (r == c).astype(np.float32)


_G_INCL, _G_STRICT, _G_EYE, _G_NB8 = range(4)


def _gdn_consts(reverse):
    i, j = _packed_ij()
    t = np.arange(CHUNK)
    if reverse:
        tri = (t[None, :] >= t[:, None])
        incl = j >= i
        strict = j > i
    else:
        tri = (t[None, :] <= t[:, None])
        incl = j <= i
        strict = j < i
    b8 = (i // 8) == (j // 8)
    stack = np.stack([incl, strict, i == j, b8]).astype(np.float32)
    stack[_G_NB8] *= -1.0
    return tri.astype(np.float32), stack


def _gdn_level_masks():
    i, j = _packed_ij()
    b8 = (i // 8) == (j // 8)
    c16 = ((i // 16) == (j // 16)) & ~b8
    c32 = ((i // 32) == (j // 32)) & ((i // 16) != (j // 16))
    c64 = (i // 32) != (j // 32)
    bd = _bd_mask()
    lv = [-b8.astype(np.float32)] + [c.astype(np.float32) for c in (c16, c32, c64)]
    return np.stack([np.concatenate([x, x], axis=0) * bd for x in lv])


def _gdn_expand(reverse):
    d = 1 if reverse else 0
    col = np.arange(AB_PAD)[:, None]
    head = (np.arange(GROUP_W) // HEAD_DIM)[None, :]
    e_lg = (col == d * N_HEADS + head)
    e_bt = (col == 2 * N_HEADS + d * N_HEADS + head)
    return e_lg.astype(np.float32), e_bt.astype(np.float32)


_H_LEVELS = (1, 2, 4, 8, 16, 32)


def _hgrn_consts(reverse):
    r = np.arange(CHUNK)[:, None]
    t = np.arange(CHUNK)[None, :]
    i, j = _packed_ij()
    q_par, k_par = (0, 1) if reverse else (1, 0)
    masks = [((i // (2 * s)) == (j // (2 * s))) & (((i // s) % 2) == q_par) & (((j // s) % 2) == k_par)
             for s in _H_LEVELS]
    masks.append(i == j)
    tri = (t >= r) if reverse else (t <= r)
    return tri.astype(np.float32), np.stack(masks).astype(np.float32)


def _hgrn_exponents(reverse, b_ref, lf):
    w = lf.shape[1]
    gr = SUBLANES
    rig = lax.broadcasted_iota(jnp.int32, (gr, w), 0)
    grp = lambda g: b_ref[gr * g:gr * g + gr, :]
    row = lambda r: b_ref[r:r + 1, :]
    n_grp = CHUNK // gr
    q_par = 0 if reverse else 1
    sign = lambda s: jnp.where(((rig // s) % 2) == q_par, 1.0, -1.0)
    out = []
    r64 = lax.broadcasted_iota(jnp.int32, lf.shape, 0)
    out.append(jnp.where((r64 % 2) == q_par, lf, 0.0))
    bnd2 = (2, 6) if reverse else (1, 5)
    sg2, sg4 = sign(2), sign(4)
    out.append(jnp.concatenate(
        [(grp(g) - jnp.where(rig < 4, row(gr * g + bnd2[0]), row(gr * g + bnd2[1]))) * sg2 for g in range(n_grp)],
        axis=0))
    bnd4 = 4 if reverse else 3
    out.append(jnp.concatenate([(grp(g) - row(gr * g + bnd4)) * sg4 for g in range(n_grp)], axis=0))
    for s in (8, 16, 32):
        parts = []
        for g in range(n_grp):
            blk = (gr * g) // s
            bnd = 2 * s * (blk // 2) + (s if reverse else s - 1)
            parts.append(grp(g) - row(bnd) if (blk % 2) == q_par else row(bnd) - grp(g))
        out.append(jnp.concatenate(parts, axis=0))
    out.append(b_ref[...])
    out.append(row(0 if reverse else CHUNK - 1) - b_ref[...])
    return out


def _mod_kernel(c_ref, w_ref, b_ref, o_ref):
    sc = _silu(c_ref[...]).astype(BF16)
    o_ref[0] = _dot(sc, w_ref[0].astype(BF16)) + b_ref[0]


def _modulation(cvec, w_mod, b_mod):
    depth, d, n = w_mod.shape
    tn = 1536
    return pl.pallas_call(
        _mod_kernel,
        grid=(depth, n // tn),
        in_specs=[pl.BlockSpec((SUBLANES, d), lambda l, j: (0, 0)),
                  pl.BlockSpec((1, d, tn), lambda l, j: (l, 0, j)),
                  pl.BlockSpec((1, 1, tn), lambda l, j: (l, 0, j))],
        out_specs=pl.BlockSpec((1, SUBLANES, tn), lambda l, j: (l, 0, j)),
        out_shape=jax.ShapeDtypeStruct((depth, SUBLANES, n), F32),
        compiler_params=_cparams(("arbitrary", "arbitrary")),
        name="modulation",
    )(cvec, w_mod, b_mod.reshape(depth, 1, n))


_PROJ_WIDTHS = (3 * GROUP_W, GROUP_W, AB_PAD, 4 * GROUP_W, GROUP_W)
_N_ROW_MAJOR = 3


def _norm_mod(x, g, shift, scale):
    y = x * lax.rsqrt(jnp.mean(x * x, axis=-1, keepdims=True) + EPS) * g
    return y * (1.0 + scale) + shift


def _grid_perm(tm):
    rows = tm // GRID_W
    n = np.arange(tm)
    src = (n % rows) * GRID_W + n // rows
    p = np.zeros((tm, tm), np.float32)
    p[n, src] = 1.0
    return p


def _head_sums(x, ones):
    half = ones.shape[0]
    return jnp.concatenate([_dot(x[:, 0:half].astype(BF16), ones), _dot(x[:, half:2 * half].astype(BF16), ones)],
                           axis=1)


def _proj_kernel(col_major, n_tiles, x_ref, xp_ref, xn_ref, mod_ref, g_ref, w_ref, perm_ref, convw_ref, ones_ref,
                 *out_refs):
    tile = pl.program_id(1)
    g, shift, scale = g_ref[...], mod_ref[0, 0:1, :], mod_ref[0, 1:2, :]
    h = _norm_mod(x_ref[0], g, shift, scale).astype(BF16)
    tm = h.shape[0]

    qkv_ref = out_refs[0]
    halo = jnp.concatenate([xp_ref[0], xn_ref[0]], axis=0)
    h_halo = _norm_mod(halo, g, shift, scale).astype(BF16)
    groups = [slice(n * GROUP_W, (n + 1) * GROUP_W) for n in range(3)]
    p_halo = [_dot(h_halo, w_ref[:, c]) for c in groups]
    p = [_dot(h, w_ref[:, c]) for c in groups]

    row = lax.broadcasted_iota(jnp.int32, (tm, 1), 0)
    y = []
    for c, pc, ph in zip(groups, p, p_halo):
        pv = jnp.where(tile > 0, ph[SUBLANES - 1:SUBLANES, :], 0.0)
        nx = jnp.where(tile < n_tiles - 1, ph[SUBLANES:SUBLANES + 1, :], 0.0)
        p_prev = jnp.where(row == 0, pv, pltpu.roll(pc, 1, axis=0))
        p_next = jnp.where(row == tm - 1, nx, pltpu.roll(pc, tm - 1, axis=0))
        y.append(_silu(convw_ref[0:1, c] * p_prev + convw_ref[1:2, c] * pc + convw_ref[2:3, c] * p_next))
    off = _PROJ_WIDTHS[0]
    for n, (ref, w) in enumerate(zip(out_refs, _PROJ_WIDTHS)):
        if n == 0:
            continue
        if n == _N_ROW_MAJOR and col_major:
            h = _dot(perm_ref[...], h).astype(BF16)
        val = _dot(h, w_ref[:, off:off + w])
        ref[0] = val.reshape(ref.shape[1:])
        off += w

    ones = ones_ref[...]
    q, k, v = y
    qkv_ref[0, :, groups[2]] = v
    qkv_ref[0, :, groups[0]] = q * lax.rsqrt(_head_sums(q * q, ones) + EPS) * (HEAD_DIM ** -0.5)
    qkv_ref[0, :, groups[1]] = k * lax.rsqrt(_head_sums(k * k, ones) + EPS)


def _projection(x, mod, g, w_cat, conv_w, shared_mod, col_major):
    b, t, d = x.shape
    tm = PROJ_TM if col_major else min(PROJ_TM, t)
    rows_t = tm // GRID_W
    n_tiles = t // tm
    hb = tm // SUBLANES
    n_hb = t // SUBLANES
    mod_map = (lambda bi, i: (0, 0, 0)) if shared_mod else (lambda bi, i: (bi, 0, 0))
    tok = lambda w: pl.BlockSpec((1, tm, w), lambda bi, i: (bi, i, 0))
    const = lambda shape: pl.BlockSpec(shape, lambda bi, i: (0,) * len(shape), pipeline_mode=pl.Buffered(1))
    out_specs = [tok(w) for w in _PROJ_WIDTHS[:_N_ROW_MAJOR]]
    out_shape = [jax.ShapeDtypeStruct((b, t, w), F32) for w in _PROJ_WIDTHS[:_N_ROW_MAJOR]]
    for w in _PROJ_WIDTHS[_N_ROW_MAJOR:]:
        if col_major:
            out_specs.append(pl.BlockSpec((1, GRID_W, rows_t, w), lambda bi, i: (bi, 0, i, 0)))
            out_shape.append(jax.ShapeDtypeStruct((b, GRID_W, t // GRID_W, w), F32))
        else:
            out_specs.append(tok(w))
            out_shape.append(jax.ShapeDtypeStruct((b, t, w), F32))
    perm = jnp.asarray(_grid_perm(tm), BF16)
    ones = jnp.asarray(_seg_ones(GROUP_W // 2), BF16)
    return pl.pallas_call(
        functools.partial(_proj_kernel, col_major, n_tiles),
        grid=(b, n_tiles),
        in_specs=[tok(d),
                  pl.BlockSpec((1, SUBLANES, d), lambda bi, i: (bi, jnp.maximum(i * hb - 1, 0), 0)),
                  pl.BlockSpec((1, SUBLANES, d), lambda bi, i: (bi, jnp.minimum((i + 1) * hb, n_hb - 1), 0)),
                  pl.BlockSpec((1, N_MOD, d), mod_map),
                  const((1, d)), const(w_cat.shape), const(perm.shape), const(conv_w.shape), const(ones.shape)],
        out_specs=out_specs,
        out_shape=out_shape,
        compiler_params=_cparams(("arbitrary", "arbitrary")),
        name="projection",
    )(x, x, x, mod, g, w_cat, perm, conv_w, ones)


def _gdn_gates(ab, alog_ref, dtb_ref, elg, ebt, lg_sc, bt_sc):
    lg_c = -jnp.exp(alog_ref[...]) * _softplus(ab + dtb_ref[...])
    lg_sc[...] = _dot_sel_rhs(lg_c, elg)
    bt_sc[...] = _dot_sel_rhs(_sigmoid(ab), ebt)


def _gdn_chunk_terms(reverse, items, qkv_ref, row0, lg_sc, bt_sc, tri, cm, bdl, bdm):
    bdm16 = bdm.astype(BF16)
    ld = lambda ref, base=0, off=0: [ref[pl.ds(base + c * CHUNK, CHUNK), off + p * PAIR_W:off + (p + 1) * PAIR_W]
                                     for c, p in items]
    qp, kp, vp = ld(qkv_ref, row0), ld(qkv_ref, row0, GROUP_W), ld(qkv_ref, row0, 2 * GROUP_W)
    bt = ld(bt_sc)
    bd16 = lambda xs: [_bd(x.astype(BF16), bdm16) for x in xs]
    pmul = lambda xs, ys: [_dot(x.astype(BF16), y) for x, y in zip(xs, bd16(ys))]

    gam_c = {c: _dot_sel(tri, lg_sc[c * CHUNK:(c + 1) * CHUNK, :]) for c in sorted({c for c, _ in items})}
    gam = [gam_c[c][:, p * PAIR_W:(p + 1) * PAIR_W] for c, p in items]
    gam_row = [jnp.sum(x * cm(_G_EYE), axis=0, keepdims=True) for x in gam]
    dincl = [jnp.exp((x - r) * cm(_G_INCL)) * cm(_G_INCL) for x, r in zip(gam, gam_row)]
    kb = [x.astype(BF16) for x in kp]
    k2 = [_bd(x, bdm16) for x in kb]
    kk = [_dot_nt(a, b) for a, b in zip(kb, k2)]
    qk = [_dot_nt(a.astype(BF16), b) for a, b in zip(qp, k2)]
    m = [a * b * d * cm(_G_STRICT) for a, b, d in zip(kk, bt, dincl)]
    m2 = [jnp.concatenate([x, x], axis=0) for x in (y.astype(BF16) for y in m)]
    n1 = [x * cm(_G_NB8) for x in m]
    n2 = [_dot(a.astype(BF16), b * bdl(0)) for a, b in zip(n1, m2)]
    n4 = pmul(n2, n2)
    t_inv = [cm(_G_EYE) + x for x in n1]
    t_inv = [t + d for t, d in zip(t_inv, pmul(t_inv, n2))]
    t_inv = [t + d for t, d in zip(t_inv, pmul(t_inv, n4))]
    for lvl in (1, 2, 3):
        left = [_dot(t.astype(BF16), b * bdl(lvl)) for t, b in zip(t_inv, m2)]
        t_inv = [t - d for t, d in zip(t_inv, pmul(left, t_inv))]
    eg = [jnp.exp(x) for x in gam]
    rhs = [jnp.concatenate([_bd((v * b).astype(BF16), bdm16), _bd((k * b * e).astype(BF16), bdm16)], axis=1)
           for v, k, b, e in zip(vp, kp, bt, eg)]
    uw = [_dot(t.astype(BF16), r) for t, r in zip(t_inv, rhs)]
    last = 0 if reverse else CHUNK - 1
    g_last = [x[last:last + 1, :] for x in gam]
    return dict(
        u=[x[:, 0:PAIR_W] for x in uw],
        wq=[jnp.concatenate([x[:, PAIR_W:2 * PAIR_W], q * e], axis=0).astype(BF16) for x, q, e in zip(uw, qp, eg)],
        qkd=[(a * d).astype(BF16) for a, d in zip(qk, dincl)],
        kdt=[(k * jnp.exp(g - x)).T.astype(BF16) for k, g, x in zip(kp, g_last, gam)],
        decay=[jnp.exp(g) for g in g_last])


def _gdn_kernel(n_steps, n_sub, *refs):
    (qkv_f, ab_f, qkv_b, ab_b, alog_ref, dtb_ref, s0_ref, tri_ref, cm_ref, bdl_ref, bdm_ref, elg_ref, ebt_ref,
     of_ref, ob_ref, sfin_ref, lg_sc, bt_sc, s_sc) = refs
    step = pl.program_id(1)
    tt = qkv_f.shape[1] // n_sub
    n_chunks = tt // CHUNK

    @pl.when(step == 0)
    def _():
        s_sc[...] = s0_ref[0]

    bdm = bdm_ref[...]
    bdm16 = bdm.astype(BF16)
    tok = ((qkv_f, ab_f), (qkv_b, ab_b))
    o_refs = (of_ref, ob_ref)

    def sub_tile(j, carry):
        row0 = [pl.multiple_of(j * tt, tt), pl.multiple_of((n_sub - 1 - j) * tt, tt)]
        terms = []
        for d, reverse in enumerate((False, True)):
            qkv_ref, ab_ref = tok[d]
            _gdn_gates(ab_ref[0, pl.ds(row0[d], tt), :], alog_ref, dtb_ref, elg_ref[d], ebt_ref[d],
                       lg_sc.at[d], bt_sc.at[d])
            items = [(c, p) for c in range(n_chunks) for p in range(N_PAIRS)]
            cm = functools.partial(lambda dd, idx: cm_ref[dd, idx], d)
            terms.append(_gdn_chunk_terms(reverse, items, qkv_ref.at[0], row0[d], lg_sc.at[d], bt_sc.at[d],
                                          tri_ref[d], cm, lambda idx: bdl_ref[idx], bdm))

        s = [[s_sc[d, p] for p in range(N_PAIRS)] for d in range(2)]
        for ci in range(n_chunks):
            lanes = [(d, p, ((n_chunks - 1 - ci) if d else ci) * N_PAIRS + p)
                     for d in range(2) for p in range(N_PAIRS)]
            sq = [_dot(terms[d]["wq"][i], s[d][p].astype(BF16)) for d, p, i in lanes]
            v_new = [terms[d]["u"][i] - x[0:CHUNK] for (d, p, i), x in zip(lanes, sq)]
            v16 = [x.astype(BF16) for x in v_new]
            o = [x[CHUNK:2 * CHUNK] + _dot(terms[d]["qkd"][i], _bd(v, bdm16))
                 for (d, p, i), x, v in zip(lanes, sq, v16)]
            upd = [_dot(terms[d]["kdt"][i], v) * bdm for (d, p, i), v in zip(lanes, v16)]
            for (d, p, i), x, y in zip(lanes, o, upd):
                c = i // N_PAIRS
                o_refs[d][0, pl.ds(row0[d] + c * CHUNK, CHUNK), p * PAIR_W:(p + 1) * PAIR_W] = x
                s[d][p] = s[d][p] * terms[d]["decay"][i] + y
        for d in range(2):
            for p in range(N_PAIRS):
                s_sc[d, p] = s[d][p]
        return carry

    lax.fori_loop(0, n_sub, sub_tile, 0)

    @pl.when(step == n_steps - 1)
    def _():
        sfin_ref[0] = s_sc[...]


def _gdn_scan(qkv, ab, a_log_pad, dt_bias_pad, s0):
    b, t, _ = qkv.shape
    sub = min(GDN_TT, t)
    n_sub = min(GDN_NSUB, t // sub)
    tt = sub * n_sub
    n_tiles = t // tt
    consts = [_gdn_consts(rev) for rev in (False, True)]
    tri = np.stack([c[0] for c in consts])
    cm = np.stack([c[1] for c in consts])
    bdl = _gdn_level_masks()
    expand = [_gdn_expand(rev) for rev in (False, True)]
    e_lg = np.stack([e[0] for e in expand])
    e_bt = np.stack([e[1] for e in expand])
    const = lambda shape: pl.BlockSpec(shape, lambda bi, i: (0,) * len(shape))
    state_spec = pl.BlockSpec((1, 2, N_PAIRS, PAIR_W, PAIR_W), lambda bi, i: (bi, 0, 0, 0, 0))

    def tok_specs(tile_of):
        return [pl.BlockSpec((1, tt, 3 * GROUP_W), lambda bi, i: (bi, tile_of(i), 0)),
                pl.BlockSpec((1, tt, AB_PAD), lambda bi, i: (bi, tile_of(i), 0))]

    fwd_tile = lambda i: i
    bwd_tile = lambda i: n_tiles - 1 - i
    o_f, o_b, s_fin = pl.pallas_call(
        functools.partial(_gdn_kernel, n_tiles, n_sub),
        grid=(b, n_tiles),
        in_specs=tok_specs(fwd_tile) + tok_specs(bwd_tile)
                 + [const((1, AB_PAD)), const((1, AB_PAD)), state_spec,
                    const(tri.shape), const(cm.shape), const(bdl.shape), const((PAIR_W, PAIR_W)),
                    const(e_lg.shape), const(e_bt.shape)],
        out_specs=[pl.BlockSpec((1, tt, GROUP_W), lambda bi, i: (bi, fwd_tile(i), 0)),
                   pl.BlockSpec((1, tt, GROUP_W), lambda bi, i: (bi, bwd_tile(i), 0)),
                   state_spec],
        out_shape=[jax.ShapeDtypeStruct((b, t, GROUP_W), F32),
                   jax.ShapeDtypeStruct((b, t, GROUP_W), F32),
                   jax.ShapeDtypeStruct((b, 2, N_PAIRS, PAIR_W, PAIR_W), F32)],
        scratch_shapes=[pltpu.VMEM((2, sub, GROUP_W), F32)] * 2 + [pltpu.VMEM((2, N_PAIRS, PAIR_W, PAIR_W), F32)],
        compiler_params=_cparams(("arbitrary", "arbitrary")),
        name="gdn_scan",
    )(qkv, ab, qkv, ab, a_log_pad, dt_bias_pad, s0,
      jnp.asarray(tri, BF16), jnp.asarray(cm), jnp.asarray(bdl, BF16), jnp.asarray(_bd_mask()),
      jnp.asarray(e_lg, BF16), jnp.asarray(e_bt, BF16))
    return o_f, o_b, s_fin


def _hgrn_prep(reverse, layer, ph_ref, logits, q_sc, k_sc, lf_sc):
    e = jnp.exp(logits - jnp.max(logits, axis=0, keepdims=True))
    prob = e / jnp.sum(e, axis=0, keepdims=True)
    lb = jnp.maximum(jnp.sum(prob[0:layer + 1], axis=0, keepdims=True) - prob[0:1], 0.0)
    z_off = 2 * GROUP_W if reverse else GROUP_W
    q_sc[...] = _silu(ph_ref[:, 0:GROUP_W]) * (HEAD_DIM ** -0.5)
    sig = _sigmoid(ph_ref[:, z_off:z_off + GROUP_W])
    lf_sc[...] = jnp.log(jnp.maximum(lb, LB_FLOOR) + (1.0 - lb) * sig)
    k_sc[...] = (1.0 - lb) * (1.0 - sig)


def _hgrn_chunk_terms(reverse, n_chunks, ph_ref, q_sc, k_sc, lf_sc, b_sc, ez_sc, tri, hm, bdm16):
    n_lvl = len(_H_LEVELS)
    q_row0 = n_lvl * CHUNK
    k_row0 = (n_lvl + 1) * CHUNK
    last = 0 if reverse else CHUNK - 1
    rows = lambda c: slice(c * CHUNK, (c + 1) * CHUNK)
    lanes = lambda p: slice(p * PAIR_W, (p + 1) * PAIR_W)
    for c in range(n_chunks):
        lf = lf_sc[rows(c), :]
        b_sc[c] = _dot_sel(tri, lf)
        for n, ex in enumerate(_hgrn_exponents(reverse, b_sc.at[c], lf)):
            ez_sc[c, n * CHUNK:(n + 1) * CHUNK, :] = jnp.exp(ex.astype(BF16)).astype(ez_sc.dtype)
    items = [(c, p) for c in range(n_chunks) for p in range(N_PAIRS)]
    ez = lambda row0: [ez_sc[c, row0:row0 + CHUNK, lanes(p)].astype(BF16) for c, p in items]
    q16 = [q_sc[rows(c), lanes(p)].astype(BF16) for c, p in items]
    k16 = [k_sc[rows(c), lanes(p)].astype(BF16) for c, p in items]
    vp = [ph_ref[rows(c), 3 * GROUP_W + p * PAIR_W:3 * GROUP_W + (p + 1) * PAIR_W] for c, p in items]
    bdk = [_bd(k, bdm16) for k in k16]
    a = [_dot_nt(q, k) * hm(n_lvl) for q, k in zip(q16, bdk)]
    for li in range(n_lvl):
        e16 = ez(li * CHUNK)
        sc = [_dot_nt(q * e, k * jnp.concatenate([e, e], axis=0)) for q, k, e in zip(q16, bdk, e16)]
        a = [x + y * hm(li) for x, y in zip(a, sc)]
    return dict(
        o=[_dot(x.astype(BF16), _bd(v.astype(BF16), bdm16)) for x, v in zip(a, vp)],
        qd=[q * e for q, e in zip(q16, ez(q_row0))],
        kd=[k * e for k, e in zip(k16, ez(k_row0))],
        vt=[v.T.astype(BF16) for v in vp],
        decay=[jnp.exp(b_sc[c, last:last + 1, lanes(p)]) for c, p in items])


def _hgrn_kernel(layer, n_steps, ph_f, ph_b, lbl_ref, s0_ref, tri_ref, hm_ref, bdm_ref,
                 of_ref, ob_ref, sfin_ref, q_sc, k_sc, lf_sc, b_sc, ez_sc, s_sc):
    step = pl.program_id(1)
    n_col, tt = ph_f.shape[1], ph_f.shape[2]
    n_chunks = tt // CHUNK

    @pl.when(step == 0)
    def _():
        s_sc[...] = s0_ref[0]

    bdm = bdm_ref[...]
    bdm16 = bdm.astype(BF16)
    o_refs = (of_ref, ob_ref)

    def column(j, carry):
        col = (j, n_col - 1 - j)
        ph = (ph_f.at[0, col[0]], ph_b.at[0, col[1]])
        terms = []
        for d, reverse in enumerate((False, True)):
            sc = (q_sc.at[d], k_sc.at[d], lf_sc.at[d])
            _hgrn_prep(reverse, layer, ph[d], lbl_ref[d], *sc)
            hm = functools.partial(lambda dd, idx: hm_ref[dd, idx], d)
            terms.append(_hgrn_chunk_terms(reverse, n_chunks, ph[d], *sc, b_sc.at[d], ez_sc.at[d], tri_ref[d], hm,
                                           bdm16))

        s = [[s_sc[d, p] for p in range(N_PAIRS)] for d in range(2)]
        for ci in range(n_chunks):
            lanes = [(d, p, ((n_chunks - 1 - ci) if d else ci) * N_PAIRS + p)
                     for d in range(2) for p in range(N_PAIRS)]
            o = [terms[d]["o"][i] + _dot_nt(terms[d]["qd"][i], s[d][p].astype(BF16)) for d, p, i in lanes]
            upd = [_dot(terms[d]["vt"][i], terms[d]["kd"][i]) * bdm for d, p, i in lanes]
            for (d, p, i), x, y in zip(lanes, o, upd):
                c = i // N_PAIRS
                o_refs[d][0, col[d], c * CHUNK:(c + 1) * CHUNK, p * PAIR_W:(p + 1) * PAIR_W] = x
                s[d][p] = s[d][p] * terms[d]["decay"][i] + y
        for d in range(2):
            for p in range(N_PAIRS):
                s_sc[d, p] = s[d][p]
        return carry

    lax.fori_loop(0, n_col, column, 0)

    @pl.when(step == n_steps - 1)
    def _():
        sfin_ref[0] = s_sc[...]


def _hgrn_scan(ph, lb_logits, s0, layer):
    b, n_tiles, tt, _ = ph.shape
    n_col = min(HGRN_NCOL, n_tiles)
    n_steps = n_tiles // n_col
    depth = lb_logits.shape[0]
    consts = [_hgrn_consts(rev) for rev in (False, True)]
    tri = np.stack([c[0] for c in consts])
    hm = np.stack([c[1] for c in consts])
    n_chunks = tt // CHUNK
    n_exp = len(_H_LEVELS) + 2
    const = lambda shape: pl.BlockSpec(shape, lambda bi, i: (0,) * len(shape))
    state_spec = pl.BlockSpec((1, 2, N_PAIRS, PAIR_W, PAIR_W), lambda bi, i: (bi, 0, 0, 0, 0))
    fwd = lambda w: pl.BlockSpec((1, n_col, tt, w), lambda bi, i: (bi, i, 0, 0))
    bwd = lambda w: pl.BlockSpec((1, n_col, tt, w), lambda bi, i: (bi, n_steps - 1 - i, 0, 0))
    return pl.pallas_call(
        functools.partial(_hgrn_kernel, layer, n_steps),
        grid=(b, n_steps),
        in_specs=[fwd(4 * GROUP_W), bwd(4 * GROUP_W), const((2, depth, GROUP_W)), state_spec,
                  const(tri.shape), const(hm.shape), const((PAIR_W, PAIR_W))],
        out_specs=[fwd(GROUP_W), bwd(GROUP_W), state_spec],
        out_shape=[jax.ShapeDtypeStruct((b, n_tiles, tt, GROUP_W), F32),
                   jax.ShapeDtypeStruct((b, n_tiles, tt, GROUP_W), F32),
                   jax.ShapeDtypeStruct((b, 2, N_PAIRS, PAIR_W, PAIR_W), F32)],
        scratch_shapes=[pltpu.VMEM((2, tt, GROUP_W), F32)] * 3
                       + [pltpu.VMEM((2, n_chunks, CHUNK, GROUP_W), F32),
                          pltpu.VMEM((2, n_chunks, n_exp * CHUNK, GROUP_W), F32),
                          pltpu.VMEM((2, N_PAIRS, PAIR_W, PAIR_W), F32)],
        compiler_params=_cparams(("arbitrary", "arbitrary")),
        name="hgrn_scan",
    )(ph, ph, jnp.transpose(lb_logits, (1, 0, 2)), s0, jnp.asarray(tri, BF16), jnp.asarray(hm),
      jnp.asarray(_bd_mask()))


def _mlp_kernel(final, col_major, x_ref, oaf_ref, oab_ref, obf_ref, obb_ref, ga_ref, gb_ref, mod_ref,
                gg_ref, hg_ref, n2g_ref, fg_ref, ones_ref, perm_ref, wo_ref, w1_ref, w2_ref, o_ref):
    ones = ones_ref[...]
    tm = x_ref.shape[1]
    oa = oaf_ref[0] + oab_ref[0]
    ob = (obf_ref[0] + obb_ref[0]).reshape(tm, GROUP_W)
    gb = gb_ref[0].reshape(tm, GROUP_W)
    inv_d = 1.0 / HEAD_DIM
    ya = oa * lax.rsqrt(_head_sums(oa * oa, ones) * inv_d + EPS) * gg_ref[...] * _silu(ga_ref[0])
    yb = ob * lax.rsqrt(_head_sums(ob * ob, ones) * inv_d + EPS) * hg_ref[...] * _sigmoid(gb)
    yb = yb.astype(BF16)
    if col_major:
        yb = _dot(perm_ref[...], yb).astype(BF16)
    y = _dot(ya.astype(BF16), wo_ref[0:GROUP_W, :]) + _dot(yb, wo_ref[GROUP_W:2 * GROUP_W, :])
    x1 = x_ref[0] + mod_ref[0, 2:3, :] * y
    h = _norm_mod(x1, n2g_ref[...], mod_ref[0, 3:4, :], mod_ref[0, 4:5, :]).astype(BF16)
    hid = jnp.maximum(_dot(h, w1_ref[...]), 0.0)
    x2 = x1 + mod_ref[0, 5:6, :] * _dot((hid * hid).astype(BF16), w2_ref[...])
    if final:
        x2 = x2 * lax.rsqrt(jnp.mean(x2 * x2, axis=-1, keepdims=True) + EPS) * fg_ref[...]
    o_ref[0] = x2


def _out_mlp(x, oa_f, oa_b, ob_f, ob_b, ga, gb, mod, gdn_g, hgrn_g, n2g, final_g, w_out, w1, w2,
             shared_mod, final, col_major):
    b, t, d = x.shape
    tm = MLP_TM if col_major else min(MLP_TM, t)
    mod_map = (lambda bi, i: (0, 0, 0)) if shared_mod else (lambda bi, i: (bi, 0, 0))
    tok = lambda w: pl.BlockSpec((1, tm, w), lambda bi, i: (bi, i, 0))
    col = pl.BlockSpec((1, GRID_W, tm // GRID_W, GROUP_W), lambda bi, i: (bi, 0, i, 0))
    hg_spec = col if col_major else tok(GROUP_W)
    const = lambda shape: pl.BlockSpec(shape, lambda bi, i: (0,) * len(shape), pipeline_mode=pl.Buffered(1))
    perm_t = jnp.asarray(_grid_perm(tm).T, BF16)
    return pl.pallas_call(
        functools.partial(_mlp_kernel, final, col_major),
        grid=(b, t // tm),
        in_specs=[tok(d), tok(GROUP_W), tok(GROUP_W), hg_spec, hg_spec, tok(GROUP_W), hg_spec,
                  pl.BlockSpec((1, N_MOD, d), mod_map),
                  const((1, GROUP_W)), const((1, GROUP_W)), const((1, d)), const((1, d)),
                  const((GROUP_W // 2, GROUP_W // 2)), const(perm_t.shape),
                  const(w_out.shape), const(w1.shape), const(w2.shape)],
        out_specs=tok(d),
        out_shape=jax.ShapeDtypeStruct((b, t, d), F32),
        compiler_params=_cparams(("arbitrary", "arbitrary")),
        name="out_mlp",
    )(x, oa_f, oa_b, ob_f, ob_b, ga, gb, mod, gdn_g, hgrn_g, n2g, final_g,
      jnp.asarray(_seg_ones(GROUP_W // 2), BF16), perm_t, w_out, w1, w2)


def _split_w_in(w):
    g = GROUP_W
    qkv, ga, ab = w[:, 0:3 * g], w[:, 3 * g:4 * g], w[:, 4 * g:4 * g + 4 * N_HEADS]
    rest = w[:, 4 * g + 4 * N_HEADS:]
    ph, gb = rest[:, 0:4 * g], rest[:, 4 * g:5 * g]
    ab = jnp.pad(ab, ((0, 0), (0, AB_PAD - 4 * N_HEADS)))
    return jnp.concatenate([qkv, ga, ab, ph, gb], axis=1).astype(BF16)


def _pad_lanes(v):
    flat = v.reshape(1, -1)
    return jnp.pad(flat, ((0, 0), (0, AB_PAD - flat.shape[1])))


def kernel(x, c, ctx, c_ctx, w_mod, b_mod, norm1_g, norm2_g, w_in, conv_w, a_log, dt_bias,
           gdn_norm_g, hgrn_norm_g, lb_logits, w_out, w_mlp1, w_mlp2, final_g):
    depth = w_mod.shape[0]
    b, _, d = x.shape
    cvec = jnp.concatenate([c, c_ctx[None, :], jnp.zeros((SUBLANES - b - 1, d), F32)], axis=0)
    mod = _modulation(cvec, w_mod, b_mod).reshape(depth, SUBLANES, N_MOD, d)
    zero_state = jnp.zeros((b, 2, N_PAIRS, PAIR_W, PAIR_W), F32)
    fg = final_g.reshape(1, d)

    x_lat, x_ctx = x, ctx
    for l in range(depth):
        need_ctx = l < depth - 1
        mod_lat, mod_ctx = mod[l, 0:b], mod[l, b:b + 1]
        w_cat = _split_w_in(w_in[l])
        n1g = norm1_g[l].reshape(1, d)
        a_pad, dt_pad = _pad_lanes(a_log[l]), _pad_lanes(dt_bias[l])
        gg = jnp.tile(gdn_norm_g[l], N_HEADS).reshape(1, GROUP_W)
        hg = jnp.tile(hgrn_norm_g[l], N_HEADS).reshape(1, GROUP_W)
        wo, w1, w2 = w_out[l].astype(BF16), w_mlp1[l].astype(BF16), w_mlp2[l].astype(BF16)

        qkv_c, ga_c, ab_c, ph_c, gb_c = _projection(x_ctx, mod_ctx, n1g, w_cat, conv_w[l], True, False)
        qkv_l, ga_l, ab_l, ph_l, gb_l = _projection(x_lat, mod_lat, n1g, w_cat, conv_w[l], False, True)

        oa_cf, oa_cb, sa = _gdn_scan(qkv_c, ab_c, a_pad, dt_pad, zero_state)
        oa_lf, oa_lb, _ = _gdn_scan(qkv_l, ab_l, a_pad, dt_pad, sa)

        t_ctx = ph_c.shape[1]
        tt_c = min(HGRN_TT, t_ctx)
        ob_cf, ob_cb, sb = _hgrn_scan(ph_c.reshape(b, t_ctx // tt_c, tt_c, 4 * GROUP_W), lb_logits, zero_state, l)
        ob_lf, ob_lb, _ = _hgrn_scan(ph_l, lb_logits, sb, l)

        n2g = norm2_g[l].reshape(1, d)
        x_lat = _out_mlp(x_lat, oa_lf, oa_lb, ob_lf, ob_lb, ga_l, gb_l, mod_lat, gg, hg, n2g, fg, wo, w1, w2,
                         False, not need_ctx, True)
        if need_ctx:
            x_ctx = _out_mlp(x_ctx, oa_cf, oa_cb, ob_cf.reshape(b, t_ctx, GROUP_W), ob_cb.reshape(b, t_ctx, GROUP_W),
                             ga_c, gb_c, mod_ctx, gg, hg, n2g, fg, wo, w1, w2, True, False, False)
    return x_lat
```

```python
import functools

import numpy as np
import jax
import jax.numpy as jnp
from jax import lax
from jax.experimental import pallas as pl
from jax.experimental.pallas import tpu as pltpu

F32 = jnp.float32
BF16 = jnp.bfloat16

SUBLANES = 8
HEAD_DIM = 64
N_HEADS = 8
GROUP_W = N_HEADS * HEAD_DIM
N_PAIRS = N_HEADS // 2
PAIR_W = 2 * HEAD_DIM
CHUNK = 64
GRID_W = 64
CONV_W = 3
N_MOD = 6
EPS = 1e-6
LB_FLOOR = 1e-30
AB_PAD = 128
VMEM_LIMIT = 56 * 1024 * 1024

PROJ_TM = 512
MLP_TM = 512
GDN_TT = 256
GDN_NSUB = 4
HGRN_TT = 128
HGRN_NCOL = 8


def _dot(a, b):
    return jnp.dot(a, b, preferred_element_type=F32)


def _dot_nt(a, b):
    return lax.dot_general(a, b, (((1,), (1,)), ((), ())), preferred_element_type=F32)


def _dot_sel(sel, x):
    hi = x.astype(BF16)
    lo = (x - hi.astype(F32)).astype(BF16)
    return _dot(sel, hi) + _dot(sel, lo)


def _dot_sel_rhs(x, sel):
    hi = x.astype(BF16)
    lo = (x - hi.astype(F32)).astype(BF16)
    return _dot(hi, sel) + _dot(lo, sel)


def _bd(x, bdmask):
    return jnp.concatenate([x, x], axis=0) * bdmask.astype(x.dtype)


def _sigmoid(x):
    return 0.5 * jnp.tanh(0.5 * x) + 0.5


def _silu(x):
    return x * _sigmoid(x)


def _softplus(x):
    return jnp.maximum(x, 0.0) + jnp.log1p(jnp.exp(-jnp.abs(x)))


def _cparams(sem):
    return pltpu.CompilerParams(dimension_semantics=sem, vmem_limit_bytes=VMEM_LIMIT)


def _packed_ij():
    i = np.arange(CHUNK)[:, None]
    j = (np.arange(PAIR_W) % HEAD_DIM)[None, :]
    return i, j


def _bd_mask():
    r = np.arange(PAIR_W)[:, None] // HEAD_DIM
    c = np.arange(PAIR_W)[None, :] // HEAD_DIM
    return (r == c).astype(np.float32)


def _seg_ones(width):
    r = np.arange(width)[:, None] // HEAD_DIM
    c = np.arange(width)[None, :] // HEAD_DIM
    return (r == c).astype(np.float32)


_G_INCL, _G_STRICT, _G_EYE, _G_NB8 = range(4)


def _gdn_consts(reverse):
    i, j = _packed_ij()
    t = np.arange(CHUNK)
    if reverse:
        tri = (t[None, :] >= t[:, None])
        incl = j >= i
        strict = j > i
    else:
        tri = (t[None, :] <= t[:, None])
        incl = j <= i
        strict = j < i
    b8 = (i // 8) == (j // 8)
    stack = np.stack([incl, strict, i == j, b8]).astype(np.float32)
    stack[_G_NB8] *= -1.0
    return tri.astype(np.float32), stack


def _gdn_level_masks():
    i, j = _packed_ij()
    b8 = (i // 8) == (j // 8)
    c16 = ((i // 16) == (j // 16)) & ~b8
    c32 = ((i // 32) == (j // 32)) & ((i // 16) != (j // 16))
    c64 = (i // 32) != (j // 32)
    bd = _bd_mask()
    lv = [-b8.astype(np.float32)] + [c.astype(np.float32) for c in (c16, c32, c64)]
    return np.stack([np.concatenate([x, x], axis=0) * bd for x in lv])


def _gdn_expand(reverse):
    d = 1 if reverse else 0
    col = np.arange(AB_PAD)[:, None]
    head = (np.arange(GROUP_W) // HEAD_DIM)[None, :]
    e_lg = (col == d * N_HEADS + head)
    e_bt = (col == 2 * N_HEADS + d * N_HEADS + head)
    return e_lg.astype(np.float32), e_bt.astype(np.float32)


_H_LEVELS = (1, 2, 4, 8, 16, 32)


def _hgrn_consts(reverse):
    r = np.arange(CHUNK)[:, None]
    t = np.arange(CHUNK)[None, :]
    i, j = _packed_ij()
    q_par, k_par = (0, 1) if reverse else (1, 0)
    masks = [((i // (2 * s)) == (j // (2 * s))) & (((i // s) % 2) == q_par) & (((j // s) % 2) == k_par)
             for s in _H_LEVELS]
    masks.append(i == j)
    tri = (t >= r) if reverse else (t <= r)
    return tri.astype(np.float32), np.stack(masks).astype(np.float32)


def _hgrn_exponents(reverse, b_ref, lf):
    w = lf.shape[1]
    gr = SUBLANES
    rig = lax.broadcasted_iota(jnp.int32, (gr, w), 0)
    grp = lambda g: b_ref[gr * g:gr * g + gr, :]
    row = lambda r: b_ref[r:r + 1, :]
    n_grp = CHUNK // gr
    q_par = 0 if reverse else 1
    sign = lambda s: jnp.where(((rig // s) % 2) == q_par, 1.0, -1.0)
    out = []
    r64 = lax.broadcasted_iota(jnp.int32, lf.shape, 0)
    out.append(jnp.where((r64 % 2) == q_par, lf, 0.0))
    bnd2 = (2, 6) if reverse else (1, 5)
    sg2, sg4 = sign(2), sign(4)
    out.append(jnp.concatenate(
        [(grp(g) - jnp.where(rig < 4, row(gr * g + bnd2[0]), row(gr * g + bnd2[1]))) * sg2 for g in range(n_grp)],
        axis=0))
    bnd4 = 4 if reverse else 3
    out.append(jnp.concatenate([(grp(g) - row(gr * g + bnd4)) * sg4 for g in range(n_grp)], axis=0))
    for s in (8, 16, 32):
        parts = []
        for g in range(n_grp):
            blk = (gr * g) // s
            bnd = 2 * s * (blk // 2) + (s if reverse else s - 1)
            parts.append(grp(g) - row(bnd) if (blk % 2) == q_par else row(bnd) - grp(g))
        out.append(jnp.concatenate(parts, axis=0))
    out.append(b_ref[...])
    out.append(row(0 if reverse else CHUNK - 1) - b_ref[...])
    return out


def _mod_kernel(c_ref, w_ref, b_ref, o_ref):
    sc = _silu(c_ref[...]).astype(BF16)
    o_ref[0] = _dot(sc, w_ref[0].astype(BF16)) + b_ref[0]


def _modulation(cvec, w_mod, b_mod):
    depth, d, n = w_mod.shape
    tn = 1536
    return pl.pallas_call(
        _mod_kernel,
        grid=(depth, n // tn),
        in_specs=[pl.BlockSpec((SUBLANES, d), lambda l, j: (0, 0)),
                  pl.BlockSpec((1, d, tn), lambda l, j: (l, 0, j)),
                  pl.BlockSpec((1, 1, tn), lambda l, j: (l, 0, j))],
        out_specs=pl.BlockSpec((1, SUBLANES, tn), lambda l, j: (l, 0, j)),
        out_shape=jax.ShapeDtypeStruct((depth, SUBLANES, n), F32),
        compiler_params=_cparams(("arbitrary", "arbitrary")),
        name="modulation",
    )(cvec, w_mod, b_mod.reshape(depth, 1, n))


_PROJ_WIDTHS = (3 * GROUP_W, GROUP_W, AB_PAD, 4 * GROUP_W, GROUP_W)
_N_ROW_MAJOR = 3


def _norm_mod(x, g, shift, scale):
    y = x * lax.rsqrt(jnp.mean(x * x, axis=-1, keepdims=True) + EPS) * g
    return y * (1.0 + scale) + shift


def _grid_perm(tm):
    rows = tm // GRID_W
    n = np.arange(tm)
    src = (n % rows) * GRID_W + n // rows
    p = np.zeros((tm, tm), np.float32)
    p[n, src] = 1.0
    return p


def _head_sums(x, ones):
    half = ones.shape[0]
    return jnp.concatenate([_dot(x[:, 0:half].astype(BF16), ones), _dot(x[:, half:2 * half].astype(BF16), ones)],
                           axis=1)


def _proj_kernel(col_major, n_tiles, x_ref, xp_ref, xn_ref, mod_ref, g_ref, w_ref, perm_ref, convw_ref, ones_ref,
                 *out_refs):
    tile = pl.program_id(1)
    g, shift, scale = g_ref[...], mod_ref[0, 0:1, :], mod_ref[0, 1:2, :]
    h = _norm_mod(x_ref[0], g, shift, scale).astype(BF16)
    tm = h.shape[0]

    qkv_ref = out_refs[0]
    halo = jnp.concatenate([xp_ref[0], xn_ref[0]], axis=0)
    h_halo = _norm_mod(halo, g, shift, scale).astype(BF16)
    groups = [slice(n * GROUP_W, (n + 1) * GROUP_W) for n in range(3)]
    p_halo = [_dot(h_halo, w_ref[:, c]) for c in groups]
    p = [_dot(h, w_ref[:, c]) for c in groups]

    row = lax.broadcasted_iota(jnp.int32, (tm, 1), 0)
    y = []
    for c, pc, ph in zip(groups, p, p_halo):
        pv = jnp.where(tile > 0, ph[SUBLANES - 1:SUBLANES, :], 0.0)
        nx = jnp.where(tile < n_tiles - 1, ph[SUBLANES:SUBLANES + 1, :], 0.0)
        p_prev = jnp.where(row == 0, pv, pltpu.roll(pc, 1, axis=0))
        p_next = jnp.where(row == tm - 1, nx, pltpu.roll(pc, tm - 1, axis=0))
        y.append(_silu(convw_ref[0:1, c] * p_prev + convw_ref[1:2, c] * pc + convw_ref[2:3, c] * p_next))
    off = _PROJ_WIDTHS[0]
    for n, (ref, w) in enumerate(zip(out_refs, _PROJ_WIDTHS)):
        if n == 0:
            continue
        if n == _N_ROW_MAJOR and col_major:
            h = _dot(perm_ref[...], h).astype(BF16)
        val = _dot(h, w_ref[:, off:off + w])
        ref[0] = val.reshape(ref.shape[1:])
        off += w

    ones = ones_ref[...]
    q, k, v = y
    qkv_ref[0, :, groups[2]] = v
    qkv_ref[0, :, groups[0]] = q * lax.rsqrt(_head_sums(q * q, ones) + EPS) * (HEAD_DIM ** -0.5)
    qkv_ref[0, :, groups[1]] = k * lax.rsqrt(_head_sums(k * k, ones) + EPS)


def _projection(x, mod, g, w_cat, conv_w, shared_mod, col_major):
    b, t, d = x.shape
    tm = PROJ_TM if col_major else min(PROJ_TM, t)
    rows_t = tm // GRID_W
    n_tiles = t // tm
    hb = tm // SUBLANES
    n_hb = t // SUBLANES
    mod_map = (lambda bi, i: (0, 0, 0)) if shared_mod else (lambda bi, i: (bi, 0, 0))
    tok = lambda w: pl.BlockSpec((1, tm, w), lambda bi, i: (bi, i, 0))
    const = lambda shape: pl.BlockSpec(shape, lambda bi, i: (0,) * len(shape), pipeline_mode=pl.Buffered(1))
    out_specs = [tok(w) for w in _PROJ_WIDTHS[:_N_ROW_MAJOR]]
    out_shape = [jax.ShapeDtypeStruct((b, t, w), F32) for w in _PROJ_WIDTHS[:_N_ROW_MAJOR]]
    for w in _PROJ_WIDTHS[_N_ROW_MAJOR:]:
        if col_major:
            out_specs.append(pl.BlockSpec((1, GRID_W, rows_t, w), lambda bi, i: (bi, 0, i, 0)))
            out_shape.append(jax.ShapeDtypeStruct((b, GRID_W, t // GRID_W, w), F32))
        else:
            out_specs.append(tok(w))
            out_shape.append(jax.ShapeDtypeStruct((b, t, w), F32))
    perm = jnp.asarray(_grid_perm(tm), BF16)
    ones = jnp.asarray(_seg_ones(GROUP_W // 2), BF16)
    return pl.pallas_call(
        functools.partial(_proj_kernel, col_major, n_tiles),
        grid=(b, n_tiles),
        in_specs=[tok(d),
                  pl.BlockSpec((1, SUBLANES, d), lambda bi, i: (bi, jnp.maximum(i * hb - 1, 0), 0)),
                  pl.BlockSpec((1, SUBLANES, d), lambda bi, i: (bi, jnp.minimum((i + 1) * hb, n_hb - 1), 0)),
                  pl.BlockSpec((1, N_MOD, d), mod_map),
                  const((1, d)), const(w_cat.shape), const(perm.shape), const(conv_w.shape), const(ones.shape)],
        out_specs=out_specs,
        out_shape=out_shape,
        compiler_params=_cparams(("arbitrary", "arbitrary")),
        name="projection",
    )(x, x, x, mod, g, w_cat, perm, conv_w, ones)


def _gdn_gates(ab, alog_ref, dtb_ref, elg, ebt, lg_sc, bt_sc):
    lg_c = -jnp.exp(alog_ref[...]) * _softplus(ab + dtb_ref[...])
    lg_sc[...] = _dot_sel_rhs(lg_c, elg)
    bt_sc[...] = _dot_sel_rhs(_sigmoid(ab), ebt)


def _gdn_chunk_terms(reverse, items, qkv_ref, row0, lg_sc, bt_sc, tri, cm, bdl, bdm):
    bdm16 = bdm.astype(BF16)
    ld = lambda ref, base=0, off=0: [ref[pl.ds(base + c * CHUNK, CHUNK), off + p * PAIR_W:off + (p + 1) * PAIR_W]
                                     for c, p in items]
    qp, kp, vp = ld(qkv_ref, row0), ld(qkv_ref, row0, GROUP_W), ld(qkv_ref, row0, 2 * GROUP_W)
    bt = ld(bt_sc)
    bd16 = lambda xs: [_bd(x.astype(BF16), bdm16) for x in xs]
    pmul = lambda xs, ys: [_dot(x.astype(BF16), y) for x, y in zip(xs, bd16(ys))]

    gam_c = {c: _dot_sel(tri, lg_sc[c * CHUNK:(c + 1) * CHUNK, :]) for c in sorted({c for c, _ in items})}
    gam = [gam_c[c][:, p * PAIR_W:(p + 1) * PAIR_W] for c, p in items]
    gam_row = [jnp.sum(x * cm(_G_EYE), axis=0, keepdims=True) for x in gam]
    dincl = [jnp.exp((x - r) * cm(_G_INCL)) * cm(_G_INCL) for x, r in zip(gam, gam_row)]
    kb = [x.astype(BF16) for x in kp]
    k2 = [_bd(x, bdm16) for x in kb]
    kq = [_dot_nt(jnp.concatenate([a, b.astype(BF16)], axis=0), c) for a, b, c in zip(kb, qp, k2)]
    kk = [x[0:CHUNK] for x in kq]
    qk = [x[CHUNK:2 * CHUNK] for x in kq]
    m = [a * b * d * cm(_G_STRICT) for a, b, d in zip(kk, bt, dincl)]
    m2 = [jnp.concatenate([x, x], axis=0) for x in (y.astype(BF16) for y in m)]
    n1 = [x * cm(_G_NB8) for x in m]
    n2 = [_dot(a.astype(BF16), b * bdl(0)) for a, b in zip(n1, m2)]
    t_inv = [cm(_G_EYE) + x for x in n1]
    both = pmul([jnp.concatenate([a, b], axis=0) for a, b in zip(n2, t_inv)], n2)
    n4 = [x[0:CHUNK] for x in both]
    t_inv = [t + x[CHUNK:2 * CHUNK] for t, x in zip(t_inv, both)]
    t_inv = [t + d for t, d in zip(t_inv, pmul(t_inv, n4))]
    for lvl in (1, 2, 3):
        left = [_dot(t.astype(BF16), b * bdl(lvl)) for t, b in zip(t_inv, m2)]
        t_inv = [t - d for t, d in zip(t_inv, pmul(left, t_inv))]
    eg = [jnp.exp(x) for x in gam]
    rhs = [jnp.concatenate([_bd((v * b).astype(BF16), bdm16), _bd((k * b * e).astype(BF16), bdm16)], axis=1)
           for v, k, b, e in zip(vp, kp, bt, eg)]
    uw = [_dot(t.astype(BF16), r) for t, r in zip(t_inv, rhs)]
    last = 0 if reverse else CHUNK - 1
    g_last = [x[last:last + 1, :] for x in gam]
    return dict(
        u=[x[:, 0:PAIR_W] for x in uw],
        wq=[jnp.concatenate([x[:, PAIR_W:2 * PAIR_W], q * e], axis=0).astype(BF16) for x, q, e in zip(uw, qp, eg)],
        qkd=[(a * d).astype(BF16) for a, d in zip(qk, dincl)],
        kdt=[(k * jnp.exp(g - x)).T.astype(BF16) for k, g, x in zip(kp, g_last, gam)],
        decay=[jnp.exp(g) for g in g_last])


def _gdn_kernel(n_steps, n_sub, *refs):
    (qkv_f, ab_f, qkv_b, ab_b, alog_ref, dtb_ref, s0_ref, tri_ref, cm_ref, bdl_ref, bdm_ref, elg_ref, ebt_ref,
     of_ref, ob_ref, sfin_ref, lg_sc, bt_sc, s_sc) = refs
    step = pl.program_id(1)
    tt = qkv_f.shape[1] // n_sub
    n_chunks = tt // CHUNK

    @pl.when(step == 0)
    def _():
        s_sc[...] = s0_ref[0]

    bdm = bdm_ref[...]
    bdm16 = bdm.astype(BF16)
    tok = ((qkv_f, ab_f), (qkv_b, ab_b))
    o_refs = (of_ref, ob_ref)

    def sub_tile(j, carry):
        row0 = [pl.multiple_of(j * tt, tt), pl.multiple_of((n_sub - 1 - j) * tt, tt)]
        terms = []
        for d, reverse in enumerate((False, True)):
            qkv_ref, ab_ref = tok[d]
            _gdn_gates(ab_ref[0, pl.ds(row0[d], tt), :], alog_ref, dtb_ref, elg_ref[d], ebt_ref[d],
                       lg_sc.at[d], bt_sc.at[d])
            items = [(c, p) for c in range(n_chunks) for p in range(N_PAIRS)]
            cm = functools.partial(lambda dd, idx: cm_ref[dd, idx], d)
            terms.append(_gdn_chunk_terms(reverse, items, qkv_ref.at[0], row0[d], lg_sc.at[d], bt_sc.at[d],
                                          tri_ref[d], cm, lambda idx: bdl_ref[idx], bdm))

        s = [[s_sc[d, p] for p in range(N_PAIRS)] for d in range(2)]
        for ci in range(n_chunks):
            lanes = [(d, p, ((n_chunks - 1 - ci) if d else ci) * N_PAIRS + p)
                     for d in range(2) for p in range(N_PAIRS)]
            sq = [_dot(terms[d]["wq"][i], s[d][p].astype(BF16)) for d, p, i in lanes]
            v_new = [terms[d]["u"][i] - x[0:CHUNK] for (d, p, i), x in zip(lanes, sq)]
            v16 = [x.astype(BF16) for x in v_new]
            o = [x[CHUNK:2 * CHUNK] + _dot(terms[d]["qkd"][i], _bd(v, bdm16))
                 for (d, p, i), x, v in zip(lanes, sq, v16)]
            upd = [_dot(terms[d]["kdt"][i], v) * bdm for (d, p, i), v in zip(lanes, v16)]
            for (d, p, i), x, y in zip(lanes, o, upd):
                c = i // N_PAIRS
                o_refs[d][0, pl.ds(row0[d] + c * CHUNK, CHUNK), p * PAIR_W:(p + 1) * PAIR_W] = x
                s[d][p] = s[d][p] * terms[d]["decay"][i] + y
        for d in range(2):
            for p in range(N_PAIRS):
                s_sc[d, p] = s[d][p]
        return carry

    lax.fori_loop(0, n_sub, sub_tile, 0)

    @pl.when(step == n_steps - 1)
    def _():
        sfin_ref[0] = s_sc[...]


def _gdn_scan(qkv, ab, a_log_pad, dt_bias_pad, s0):
    b, t, _ = qkv.shape
    sub = min(GDN_TT, t)
    n_sub = min(GDN_NSUB, t // sub)
    tt = sub * n_sub
    n_tiles = t // tt
    consts = [_gdn_consts(rev) for rev in (False, True)]
    tri = np.stack([c[0] for c in consts])
    cm = np.stack([c[1] for c in consts])
    bdl = _gdn_level_masks()
    expand = [_gdn_expand(rev) for rev in (False, True)]
    e_lg = np.stack([e[0] for e in expand])
    e_bt = np.stack([e[1] for e in expand])
    const = lambda shape: pl.BlockSpec(shape, lambda bi, i: (0,) * len(shape))
    state_spec = pl.BlockSpec((1, 2, N_PAIRS, PAIR_W, PAIR_W), lambda bi, i: (bi, 0, 0, 0, 0))

    def tok_specs(tile_of):
        return [pl.BlockSpec((1, tt, 3 * GROUP_W), lambda bi, i: (bi, tile_of(i), 0)),
                pl.BlockSpec((1, tt, AB_PAD), lambda bi, i: (bi, tile_of(i), 0))]

    fwd_tile = lambda i: i
    bwd_tile = lambda i: n_tiles - 1 - i
    o_f, o_b, s_fin = pl.pallas_call(
        functools.partial(_gdn_kernel, n_tiles, n_sub),
        grid=(b, n_tiles),
        in_specs=tok_specs(fwd_tile) + tok_specs(bwd_tile)
                 + [const((1, AB_PAD)), const((1, AB_PAD)), state_spec,
                    const(tri.shape), const(cm.shape), const(bdl.shape), const((PAIR_W, PAIR_W)),
                    const(e_lg.shape), const(e_bt.shape)],
        out_specs=[pl.BlockSpec((1, tt, GROUP_W), lambda bi, i: (bi, fwd_tile(i), 0)),
                   pl.BlockSpec((1, tt, GROUP_W), lambda bi, i: (bi, bwd_tile(i), 0)),
                   state_spec],
        out_shape=[jax.ShapeDtypeStruct((b, t, GROUP_W), F32),
                   jax.ShapeDtypeStruct((b, t, GROUP_W), F32),
                   jax.ShapeDtypeStruct((b, 2, N_PAIRS, PAIR_W, PAIR_W), F32)],
        scratch_shapes=[pltpu.VMEM((2, sub, GROUP_W), F32)] * 2 + [pltpu.VMEM((2, N_PAIRS, PAIR_W, PAIR_W), F32)],
        compiler_params=_cparams(("arbitrary", "arbitrary")),
        name="gdn_scan",
    )(qkv, ab, qkv, ab, a_log_pad, dt_bias_pad, s0,
      jnp.asarray(tri, BF16), jnp.asarray(cm), jnp.asarray(bdl, BF16), jnp.asarray(_bd_mask()),
      jnp.asarray(e_lg, BF16), jnp.asarray(e_bt, BF16))
    return o_f, o_b, s_fin


def _hgrn_prep(reverse, layer, ph_ref, logits, q_sc, k_sc, lf_sc):
    e = jnp.exp(logits - jnp.max(logits, axis=0, keepdims=True))
    prob = e / jnp.sum(e, axis=0, keepdims=True)
    lb = jnp.maximum(jnp.sum(prob[0:layer + 1], axis=0, keepdims=True) - prob[0:1], 0.0)
    z_off = 2 * GROUP_W if reverse else GROUP_W
    q_sc[...] = _silu(ph_ref[:, 0:GROUP_W]) * (HEAD_DIM ** -0.5)
    sig = _sigmoid(ph_ref[:, z_off:z_off + GROUP_W])
    lf_sc[...] = jnp.log(jnp.maximum(lb, LB_FLOOR) + (1.0 - lb) * sig)
    k_sc[...] = (1.0 - lb) * (1.0 - sig)


def _hgrn_chunk_terms(reverse, n_chunks, ph_ref, q_sc, k_sc, lf_sc, b_sc, ez_sc, tri, hm, bdm16):
    n_lvl = len(_H_LEVELS)
    q_row0 = n_lvl * CHUNK
    k_row0 = (n_lvl + 1) * CHUNK
    last = 0 if reverse else CHUNK - 1
    rows = lambda c: slice(c * CHUNK, (c + 1) * CHUNK)
    lanes = lambda p: slice(p * PAIR_W, (p + 1) * PAIR_W)
    for c in range(n_chunks):
        lf = lf_sc[rows(c), :]
        b_sc[c] = _dot_sel(tri, lf)
        for n, ex in enumerate(_hgrn_exponents(reverse, b_sc.at[c], lf)):
            ez_sc[c, n * CHUNK:(n + 1) * CHUNK, :] = jnp.exp(ex.astype(BF16)).astype(ez_sc.dtype)
    items = [(c, p) for c in range(n_chunks) for p in range(N_PAIRS)]
    ez = lambda row0: [ez_sc[c, row0:row0 + CHUNK, lanes(p)].astype(BF16) for c, p in items]
    q16 = [q_sc[rows(c), lanes(p)].astype(BF16) for c, p in items]
    k16 = [k_sc[rows(c), lanes(p)].astype(BF16) for c, p in items]
    vp = [ph_ref[rows(c), 3 * GROUP_W + p * PAIR_W:3 * GROUP_W + (p + 1) * PAIR_W] for c, p in items]
    bdk = [_bd(k, bdm16) for k in k16]
    a = [_dot_nt(q, k) * hm(n_lvl) for q, k in zip(q16, bdk)]
    for li in range(n_lvl):
        e16 = ez(li * CHUNK)
        sc = [_dot_nt(q * e, k * jnp.concatenate([e, e], axis=0)) for q, k, e in zip(q16, bdk, e16)]
        a = [x + y * hm(li) for x, y in zip(a, sc)]
    return dict(
        o=[_dot(x.astype(BF16), _bd(v.astype(BF16), bdm16)) for x, v in zip(a, vp)],
        qd=[q * e for q, e in zip(q16, ez(q_row0))],
        kd=[k * e for k, e in zip(k16, ez(k_row0))],
        vt=[v.T.astype(BF16) for v in vp],
        decay=[jnp.exp(b_sc[c, last:last + 1, lanes(p)]) for c, p in items])


def _hgrn_kernel(layer, n_steps, ph_f, ph_b, lbl_ref, s0_ref, tri_ref, hm_ref, bdm_ref,
                 of_ref, ob_ref, sfin_ref, q_sc, k_sc, lf_sc, b_sc, ez_sc, s_sc):
    step = pl.program_id(1)
    n_col, tt = ph_f.shape[1], ph_f.shape[2]
    n_chunks = tt // CHUNK

    @pl.when(step == 0)
    def _():
        s_sc[...] = s0_ref[0]

    bdm = bdm_ref[...]
    bdm16 = bdm.astype(BF16)
    o_refs = (of_ref, ob_ref)

    def column(j, carry):
        col = (j, n_col - 1 - j)
        ph = (ph_f.at[0, col[0]], ph_b.at[0, col[1]])
        terms = []
        for d, reverse in enumerate((False, True)):
            sc = (q_sc.at[d], k_sc.at[d], lf_sc.at[d])
            _hgrn_prep(reverse, layer, ph[d], lbl_ref[d], *sc)
            hm = functools.partial(lambda dd, idx: hm_ref[dd, idx], d)
            terms.append(_hgrn_chunk_terms(reverse, n_chunks, ph[d], *sc, b_sc.at[d], ez_sc.at[d], tri_ref[d], hm,
                                           bdm16))

        s = [[s_sc[d, p] for p in range(N_PAIRS)] for d in range(2)]
        for ci in range(n_chunks):
            lanes = [(d, p, ((n_chunks - 1 - ci) if d else ci) * N_PAIRS + p)
                     for d in range(2) for p in range(N_PAIRS)]
            o = [terms[d]["o"][i] + _dot_nt(terms[d]["qd"][i], s[d][p].astype(BF16)) for d, p, i in lanes]
            upd = [_dot(terms[d]["vt"][i], terms[d]["kd"][i]) * bdm for d, p, i in lanes]
            for (d, p, i), x, y in zip(lanes, o, upd):
                c = i // N_PAIRS
                o_refs[d][0, col[d], c * CHUNK:(c + 1) * CHUNK, p * PAIR_W:(p + 1) * PAIR_W] = x
                s[d][p] = s[d][p] * terms[d]["decay"][i] + y
        for d in range(2):
            for p in range(N_PAIRS):
                s_sc[d, p] = s[d][p]
        return carry

    lax.fori_loop(0, n_col, column, 0)

    @pl.when(step == n_steps - 1)
    def _():
        sfin_ref[0] = s_sc[...]


def _hgrn_scan(ph, lb_logits, s0, layer):
    b, n_tiles, tt, _ = ph.shape
    n_col = min(HGRN_NCOL, n_tiles)
    n_steps = n_tiles // n_col
    depth = lb_logits.shape[0]
    consts = [_hgrn_consts(rev) for rev in (False, True)]
    tri = np.stack([c[0] for c in consts])
    hm = np.stack([c[1] for c in consts])
    n_chunks = tt // CHUNK
    n_exp = len(_H_LEVELS) + 2
    const = lambda shape: pl.BlockSpec(shape, lambda bi, i: (0,) * len(shape))
    state_spec = pl.BlockSpec((1, 2, N_PAIRS, PAIR_W, PAIR_W), lambda bi, i: (bi, 0, 0, 0, 0))
    fwd = lambda w: pl.BlockSpec((1, n_col, tt, w), lambda bi, i: (bi, i, 0, 0))
    bwd = lambda w: pl.BlockSpec((1, n_col, tt, w), lambda bi, i: (bi, n_steps - 1 - i, 0, 0))
    return pl.pallas_call(
        functools.partial(_hgrn_kernel, layer, n_steps),
        grid=(b, n_steps),
        in_specs=[fwd(4 * GROUP_W), bwd(4 * GROUP_W), const((2, depth, GROUP_W)), state_spec,
                  const(tri.shape), const(hm.shape), const((PAIR_W, PAIR_W))],
        out_specs=[fwd(GROUP_W), bwd(GROUP_W), state_spec],
        out_shape=[jax.ShapeDtypeStruct((b, n_tiles, tt, GROUP_W), F32),
                   jax.ShapeDtypeStruct((b, n_tiles, tt, GROUP_W), F32),
                   jax.ShapeDtypeStruct((b, 2, N_PAIRS, PAIR_W, PAIR_W), F32)],
        scratch_shapes=[pltpu.VMEM((2, tt, GROUP_W), F32)] * 3
                       + [pltpu.VMEM((2, n_chunks, CHUNK, GROUP_W), F32),
                          pltpu.VMEM((2, n_chunks, n_exp * CHUNK, GROUP_W), F32),
                          pltpu.VMEM((2, N_PAIRS, PAIR_W, PAIR_W), F32)],
        compiler_params=_cparams(("arbitrary", "arbitrary")),
        name="hgrn_scan",
    )(ph, ph, jnp.transpose(lb_logits, (1, 0, 2)), s0, jnp.asarray(tri, BF16), jnp.asarray(hm),
      jnp.asarray(_bd_mask()))


def _mlp_kernel(final, col_major, x_ref, oaf_ref, oab_ref, obf_ref, obb_ref, ga_ref, gb_ref, mod_ref,
                gg_ref, hg_ref, n2g_ref, fg_ref, ones_ref, perm_ref, wo_ref, w1_ref, w2_ref, o_ref):
    ones = ones_ref[...]
    tm = x_ref.shape[1]
    oa = oaf_ref[0] + oab_ref[0]
    ob = (obf_ref[0] + obb_ref[0]).reshape(tm, GROUP_W)
    gb = gb_ref[0].reshape(tm, GROUP_W)
    inv_d = 1.0 / HEAD_DIM
    ya = oa * lax.rsqrt(_head_sums(oa * oa, ones) * inv_d + EPS) * gg_ref[...] * _silu(ga_ref[0])
    yb = ob * lax.rsqrt(_head_sums(ob * ob, ones) * inv_d + EPS) * hg_ref[...] * _sigmoid(gb)
    yb = yb.astype(BF16)
    if col_major:
        yb = _dot(perm_ref[...], yb).astype(BF16)
    y = _dot(ya.astype(BF16), wo_ref[0:GROUP_W, :]) + _dot(yb, wo_ref[GROUP_W:2 * GROUP_W, :])
    x1 = x_ref[0] + mod_ref[0, 2:3, :] * y
    h = _norm_mod(x1, n2g_ref[...], mod_ref[0, 3:4, :], mod_ref[0, 4:5, :]).astype(BF16)
    hid = jnp.maximum(_dot(h, w1_ref[...]), 0.0)
    x2 = x1 + mod_ref[0, 5:6, :] * _dot((hid * hid).astype(BF16), w2_ref[...])
    if final:
        x2 = x2 * lax.rsqrt(jnp.mean(x2 * x2, axis=-1, keepdims=True) + EPS) * fg_ref[...]
    o_ref[0] = x2


def _out_mlp(x, oa_f, oa_b, ob_f, ob_b, ga, gb, mod, gdn_g, hgrn_g, n2g, final_g, w_out, w1, w2,
             shared_mod, final, col_major):
    b, t, d = x.shape
    tm = MLP_TM if col_major else min(MLP_TM, t)
    mod_map = (lambda bi, i: (0, 0, 0)) if shared_mod else (lambda bi, i: (bi, 0, 0))
    tok = lambda w: pl.BlockSpec((1, tm, w), lambda bi, i: (bi, i, 0))
    col = pl.BlockSpec((1, GRID_W, tm // GRID_W, GROUP_W), lambda bi, i: (bi, 0, i, 0))
    hg_spec = col if col_major else tok(GROUP_W)
    const = lambda shape: pl.BlockSpec(shape, lambda bi, i: (0,) * len(shape), pipeline_mode=pl.Buffered(1))
    perm_t = jnp.asarray(_grid_perm(tm).T, BF16)
    return pl.pallas_call(
        functools.partial(_mlp_kernel, final, col_major),
        grid=(b, t // tm),
        in_specs=[tok(d), tok(GROUP_W), tok(GROUP_W), hg_spec, hg_spec, tok(GROUP_W), hg_spec,
                  pl.BlockSpec((1, N_MOD, d), mod_map),
                  const((1, GROUP_W)), const((1, GROUP_W)), const((1, d)), const((1, d)),
                  const((GROUP_W // 2, GROUP_W // 2)), const(perm_t.shape),
                  const(w_out.shape), const(w1.shape), const(w2.shape)],
        out_specs=tok(d),
        out_shape=jax.ShapeDtypeStruct((b, t, d), F32),
        compiler_params=_cparams(("arbitrary", "arbitrary")),
        name="out_mlp",
    )(x, oa_f, oa_b, ob_f, ob_b, ga, gb, mod, gdn_g, hgrn_g, n2g, final_g,
      jnp.asarray(_seg_ones(GROUP_W // 2), BF16), perm_t, w_out, w1, w2)


def _split_w_in(w):
    g = GROUP_W
    qkv, ga, ab = w[:, 0:3 * g], w[:, 3 * g:4 * g], w[:, 4 * g:4 * g + 4 * N_HEADS]
    rest = w[:, 4 * g + 4 * N_HEADS:]
    ph, gb = rest[:, 0:4 * g], rest[:, 4 * g:5 * g]
    ab = jnp.pad(ab, ((0, 0), (0, AB_PAD - 4 * N_HEADS)))
    return jnp.concatenate([qkv, ga, ab, ph, gb], axis=1).astype(BF16)


def _pad_lanes(v):
    flat = v.reshape(1, -1)
    return jnp.pad(flat, ((0, 0), (0, AB_PAD - flat.shape[1])))


def kernel(x, c, ctx, c_ctx, w_mod, b_mod, norm1_g, norm2_g, w_in, conv_w, a_log, dt_bias,
           gdn_norm_g, hgrn_norm_g, lb_logits, w_out, w_mlp1, w_mlp2, final_g):
    depth = w_mod.shape[0]
    b, _, d = x.shape
    cvec = jnp.concatenate([c, c_ctx[None, :], jnp.zeros((SUBLANES - b - 1, d), F32)], axis=0)
    mod = _modulation(cvec, w_mod, b_mod).reshape(depth, SUBLANES, N_MOD, d)
    zero_state = jnp.zeros((b, 2, N_PAIRS, PAIR_W, PAIR_W), F32)
    fg = final_g.reshape(1, d)

    x_lat, x_ctx = x, ctx
    for l in range(depth):
        need_ctx = l < depth - 1
        mod_lat, mod_ctx = mod[l, 0:b], mod[l, b:b + 1]
        w_cat = _split_w_in(w_in[l])
        n1g = norm1_g[l].reshape(1, d)
        a_pad, dt_pad = _pad_lanes(a_log[l]), _pad_lanes(dt_bias[l])
        gg = jnp.tile(gdn_norm_g[l], N_HEADS).reshape(1, GROUP_W)
        hg = jnp.tile(hgrn_norm_g[l], N_HEADS).reshape(1, GROUP_W)
        wo, w1, w2 = w_out[l].astype(BF16), w_mlp1[l].astype(BF16), w_mlp2[l].astype(BF16)

        qkv_c, ga_c, ab_c, ph_c, gb_c = _projection(x_ctx, mod_ctx, n1g, w_cat, conv_w[l], True, False)
        qkv_l, ga_l, ab_l, ph_l, gb_l = _projection(x_lat, mod_lat, n1g, w_cat, conv_w[l], False, True)

        oa_cf, oa_cb, sa = _gdn_scan(qkv_c, ab_c, a_pad, dt_pad, zero_state)
        oa_lf, oa_lb, _ = _gdn_scan(qkv_l, ab_l, a_pad, dt_pad, sa)

        t_ctx = ph_c.shape[1]
        tt_c = min(HGRN_TT, t_ctx)
        ob_cf, ob_cb, sb = _hgrn_scan(ph_c.reshape(b, t_ctx // tt_c, tt_c, 4 * GROUP_W), lb_logits, zero_state, l)
        ob_lf, ob_lb, _ = _hgrn_scan(ph_l, lb_logits, sb, l)

        n2g = norm2_g[l].reshape(1, d)
        x_lat = _out_mlp(x_lat, oa_lf, oa_lb, ob_lf, ob_lb, ga_l, gb_l, mod_lat, gg, hg, n2g, fg, wo, w1, w2,
                         False, not need_ctx, True)
        if need_ctx:
            x_ctx = _out_mlp(x_ctx, oa_cf, oa_cb, ob_cf.reshape(b, t_ctx, GROUP_W), ob_cb.reshape(b, t_ctx, GROUP_W),
                             ga_c, gb_c, mod_ctx, gg, hg, n2g, fg, wo, w1, w2, True, False, False)
    return x_lat
```

```python
import functools

import numpy as np
import jax
import jax.numpy as jnp
from jax import lax
from jax.experimental import pallas as pl
from jax.experimental.pallas import tpu as pltpu

F32 = jnp.float32
BF16 = jnp.bfloat16

SUBLANES = 8
HEAD_DIM = 64
N_HEADS = 8
GROUP_W = N_HEADS * HEAD_DIM
N_PAIRS = N_HEADS // 2
PAIR_W = 2 * HEAD_DIM
CHUNK = 64
GRID_W = 64
CONV_W = 3
N_MOD = 6
EPS = 1e-6
LB_FLOOR = 1e-30
AB_PAD = 128
VMEM_LIMIT = 56 * 1024 * 1024

PROJ_TM = 512
MLP_TM = 512
GDN_TT = 256
GDN_NSUB = 4
HGRN_TT = 128
HGRN_NCOL = 8


def _dot(a, b):
    return jnp.dot(a, b, preferred_element_type=F32)


def _dot_nt(a, b):
    return lax.dot_general(a, b, (((1,), (1,)), ((), ())), preferred_element_type=F32)


def _dot_sel(sel, x):
    hi = x.astype(BF16)
    lo = (x - hi.astype(F32)).astype(BF16)
    return _dot(sel, hi) + _dot(sel, lo)


def _dot_sel_rhs(x, sel):
    hi = x.astype(BF16)
    lo = (x - hi.astype(F32)).astype(BF16)
    return _dot(hi, sel) + _dot(lo, sel)


def _bd(x, bdmask):
    return jnp.concatenate([x, x], axis=0) * bdmask.astype(x.dtype)


def _sigmoid(x):
    return 0.5 * jnp.tanh(0.5 * x) + 0.5


def _silu(x):
    return x * _sigmoid(x)


def _softplus(x):
    return jnp.maximum(x, 0.0) + jnp.log1p(jnp.exp(-jnp.abs(x)))


def _cparams(sem):
    return pltpu.CompilerParams(dimension_semantics=sem, vmem_limit_bytes=VMEM_LIMIT)


def _packed_ij():
    i = np.arange(CHUNK)[:, None]
    j = (np.arange(PAIR_W) % HEAD_DIM)[None, :]
    return i, j


def _bd_mask():
    r = np.arange(PAIR_W)[:, None] // HEAD_DIM
    c = np.arange(PAIR_W)[None, :] // HEAD_DIM
    return (r == c).astype(np.float32)


def _seg_ones(width):
    r = np.arange(width)[:, None] // HEAD_DIM
    c = np.arange(width)[None, :] // HEAD_DIM
    return (r == c).astype(np.float32)


_G_INCL, _G_STRICT, _G_EYE, _G_NB8 = range(4)


def _gdn_consts(reverse):
    i, j = _packed_ij()
    t = np.arange(CHUNK)
    if reverse:
        tri = (t[None, :] >= t[:, None])
        incl = j >= i
        strict = j > i
    else:
        tri = (t[None, :] <= t[:, None])
        incl = j <= i
        strict = j < i
    b8 = (i // 8) == (j // 8)
    stack = np.stack([incl, strict, i == j, b8]).astype(np.float32)
    stack[_G_NB8] *= -1.0
    return tri.astype(np.float32), stack


def _gdn_level_masks():
    i, j = _packed_ij()
    b8 = (i // 8) == (j // 8)
    c16 = ((i // 16) == (j // 16)) & ~b8
    c32 = ((i // 32) == (j // 32)) & ((i // 16) != (j // 16))
    c64 = (i // 32) != (j // 32)
    bd = _bd_mask()
    lv = [-b8.astype(np.float32)] + [c.astype(np.float32) for c in (c16, c32, c64)]
    return np.stack([np.concatenate([x, x], axis=0) * bd for x in lv])


def _gdn_expand(reverse):
    d = 1 if reverse else 0
    col = np.arange(AB_PAD)[:, None]
    head = (np.arange(GROUP_W) // HEAD_DIM)[None, :]
    e_lg = (col == d * N_HEADS + head)
    e_bt = (col == 2 * N_HEADS + d * N_HEADS + head)
    return e_lg.astype(np.float32), e_bt.astype(np.float32)


_H_LEVELS = (1, 2, 4, 8, 16, 32)


def _hgrn_consts(reverse):
    r = np.arange(CHUNK)[:, None]
    t = np.arange(CHUNK)[None, :]
    i, j = _packed_ij()
    q_par, k_par = (0, 1) if reverse else (1, 0)
    masks = [((i // (2 * s)) == (j // (2 * s))) & (((i // s) % 2) == q_par) & (((j // s) % 2) == k_par)
             for s in _H_LEVELS]
    masks.append(i == j)
    tri = (t >= r) if reverse else (t <= r)
    return tri.astype(np.float32), np.stack(masks).astype(np.float32)


def _hgrn_exponents(reverse, b_ref, lf):
    w = lf.shape[1]
    gr = SUBLANES
    rig = lax.broadcasted_iota(jnp.int32, (gr, w), 0)
    grp = lambda g: b_ref[gr * g:gr * g + gr, :]
    row = lambda r: b_ref[r:r + 1, :]
    n_grp = CHUNK // gr
    q_par = 0 if reverse else 1
    sign = lambda s: jnp.where(((rig // s) % 2) == q_par, 1.0, -1.0)
    out = []
    r64 = lax.broadcasted_iota(jnp.int32, lf.shape, 0)
    out.append(jnp.where((r64 % 2) == q_par, lf, 0.0))
    bnd2 = (2, 6) if reverse else (1, 5)
    sg2, sg4 = sign(2), sign(4)
    out.append(jnp.concatenate(
        [(grp(g) - jnp.where(rig < 4, row(gr * g + bnd2[0]), row(gr * g + bnd2[1]))) * sg2 for g in range(n_grp)],
        axis=0))
    bnd4 = 4 if reverse else 3
    out.append(jnp.concatenate([(grp(g) - row(gr * g + bnd4)) * sg4 for g in range(n_grp)], axis=0))
    for s in (8, 16, 32):
        parts = []
        for g in range(n_grp):
            blk = (gr * g) // s
            bnd = 2 * s * (blk // 2) + (s if reverse else s - 1)
            parts.append(grp(g) - row(bnd) if (blk % 2) == q_par else row(bnd) - grp(g))
        out.append(jnp.concatenate(parts, axis=0))
    out.append(b_ref[...])
    out.append(row(0 if reverse else CHUNK - 1) - b_ref[...])
    return out


def _mod_kernel(c_ref, w_ref, b_ref, o_ref):
    sc = _silu(c_ref[...]).astype(BF16)
    o_ref[0] = _dot(sc, w_ref[0].astype(BF16)) + b_ref[0]


def _modulation(cvec, w_mod, b_mod):
    depth, d, n = w_mod.shape
    tn = 1536
    return pl.pallas_call(
        _mod_kernel,
        grid=(depth, n // tn),
        in_specs=[pl.BlockSpec((SUBLANES, d), lambda l, j: (0, 0)),
                  pl.BlockSpec((1, d, tn), lambda l, j: (l, 0, j)),
                  pl.BlockSpec((1, 1, tn), lambda l, j: (l, 0, j))],
        out_specs=pl.BlockSpec((1, SUBLANES, tn), lambda l, j: (l, 0, j)),
        out_shape=jax.ShapeDtypeStruct((depth, SUBLANES, n), F32),
        compiler_params=_cparams(("arbitrary", "arbitrary")),
        name="modulation",
    )(cvec, w_mod, b_mod.reshape(depth, 1, n))


_PROJ_WIDTHS = (3 * GROUP_W, GROUP_W, AB_PAD, 4 * GROUP_W, GROUP_W)
_N_ROW_MAJOR = 3


def _norm_mod(x, g, shift, scale):
    y = x * lax.rsqrt(jnp.mean(x * x, axis=-1, keepdims=True) + EPS) * g
    return y * (1.0 + scale) + shift


def _grid_perm(tm):
    rows = tm // GRID_W
    n = np.arange(tm)
    src = (n % rows) * GRID_W + n // rows
    p = np.zeros((tm, tm), np.float32)
    p[n, src] = 1.0
    return p


def _head_sums(x, ones):
    half = ones.shape[0]
    return jnp.concatenate([_dot(x[:, 0:half].astype(BF16), ones), _dot(x[:, half:2 * half].astype(BF16), ones)],
                           axis=1)


def _proj_kernel(col_major, n_tiles, x_ref, xp_ref, xn_ref, mod_ref, g_ref, w_ref, perm_ref, convw_ref, ones_ref,
                 *out_refs):
    tile = pl.program_id(1)
    g, shift, scale = g_ref[...], mod_ref[0, 0:1, :], mod_ref[0, 1:2, :]
    h = _norm_mod(x_ref[0], g, shift, scale).astype(BF16)
    tm = h.shape[0]

    qkv_ref = out_refs[0]
    halo = jnp.concatenate([xp_ref[0], xn_ref[0]], axis=0)
    h_halo = _norm_mod(halo, g, shift, scale).astype(BF16)
    groups = [slice(n * GROUP_W, (n + 1) * GROUP_W) for n in range(3)]
    p_halo = [_dot(h_halo, w_ref[:, c]) for c in groups]
    p = [_dot(h, w_ref[:, c]) for c in groups]

    row = lax.broadcasted_iota(jnp.int32, (tm, 1), 0)
    y = []
    for c, pc, ph in zip(groups, p, p_halo):
        pv = jnp.where(tile > 0, ph[SUBLANES - 1:SUBLANES, :], 0.0)
        nx = jnp.where(tile < n_tiles - 1, ph[SUBLANES:SUBLANES + 1, :], 0.0)
        p_prev = jnp.where(row == 0, pv, pltpu.roll(pc, 1, axis=0))
        p_next = jnp.where(row == tm - 1, nx, pltpu.roll(pc, tm - 1, axis=0))
        y.append(_silu(convw_ref[0:1, c] * p_prev + convw_ref[1:2, c] * pc + convw_ref[2:3, c] * p_next))
    off = _PROJ_WIDTHS[0]
    for n, (ref, w) in enumerate(zip(out_refs, _PROJ_WIDTHS)):
        if n == 0:
            continue
        if n == _N_ROW_MAJOR and col_major:
            h = _dot(perm_ref[...], h).astype(BF16)
        val = _dot(h, w_ref[:, off:off + w])
        ref[0] = val.reshape(ref.shape[1:])
        off += w

    ones = ones_ref[...]
    q, k, v = y
    qkv_ref[0, :, groups[2]] = v
    qkv_ref[0, :, groups[0]] = q * lax.rsqrt(_head_sums(q * q, ones) + EPS) * (HEAD_DIM ** -0.5)
    qkv_ref[0, :, groups[1]] = k * lax.rsqrt(_head_sums(k * k, ones) + EPS)


def _projection(x, mod, g, w_cat, conv_w, shared_mod, col_major):
    b, t, d = x.shape
    tm = PROJ_TM if col_major else min(PROJ_TM, t)
    rows_t = tm // GRID_W
    n_tiles = t // tm
    hb = tm // SUBLANES
    n_hb = t // SUBLANES
    mod_map = (lambda bi, i: (0, 0, 0)) if shared_mod else (lambda bi, i: (bi, 0, 0))
    tok = lambda w: pl.BlockSpec((1, tm, w), lambda bi, i: (bi, i, 0))
    const = lambda shape: pl.BlockSpec(shape, lambda bi, i: (0,) * len(shape), pipeline_mode=pl.Buffered(1))
    out_specs = [tok(w) for w in _PROJ_WIDTHS[:_N_ROW_MAJOR]]
    out_shape = [jax.ShapeDtypeStruct((b, t, w), F32) for w in _PROJ_WIDTHS[:_N_ROW_MAJOR]]
    for w in _PROJ_WIDTHS[_N_ROW_MAJOR:]:
        if col_major:
            out_specs.append(pl.BlockSpec((1, GRID_W, rows_t, w), lambda bi, i: (bi, 0, i, 0)))
            out_shape.append(jax.ShapeDtypeStruct((b, GRID_W, t // GRID_W, w), F32))
        else:
            out_specs.append(tok(w))
            out_shape.append(jax.ShapeDtypeStruct((b, t, w), F32))
    perm = jnp.asarray(_grid_perm(tm), BF16)
    ones = jnp.asarray(_seg_ones(GROUP_W // 2), BF16)
    return pl.pallas_call(
        functools.partial(_proj_kernel, col_major, n_tiles),
        grid=(b, n_tiles),
        in_specs=[tok(d),
                  pl.BlockSpec((1, SUBLANES, d), lambda bi, i: (bi, jnp.maximum(i * hb - 1, 0), 0)),
                  pl.BlockSpec((1, SUBLANES, d), lambda bi, i: (bi, jnp.minimum((i + 1) * hb, n_hb - 1), 0)),
                  pl.BlockSpec((1, N_MOD, d), mod_map),
                  const((1, d)), const(w_cat.shape), const(perm.shape), const(conv_w.shape), const(ones.shape)],
        out_specs=out_specs,
        out_shape=out_shape,
        compiler_params=_cparams(("arbitrary", "arbitrary")),
        name="projection",
    )(x, x, x, mod, g, w_cat, perm, conv_w, ones)


def _gdn_gates(ab, alog_ref, dtb_ref, elg, ebt, lg_sc, bt_sc):
    lg_c = -jnp.exp(alog_ref[...]) * _softplus(ab + dtb_ref[...])
    lg_sc[...] = _dot_sel_rhs(lg_c, elg)
    bt_sc[...] = _dot_sel_rhs(_sigmoid(ab), ebt)


def _gdn_chunk_terms(reverse, items, qkv_ref, row0, lg_sc, bt_sc, tri, cm, bdl, bdm):
    bdm16 = bdm.astype(BF16)
    ld = lambda ref, base=0, off=0: [ref[pl.ds(base + c * CHUNK, CHUNK), off + p * PAIR_W:off + (p + 1) * PAIR_W]
                                     for c, p in items]
    qp, kp, vp = ld(qkv_ref, row0), ld(qkv_ref, row0, GROUP_W), ld(qkv_ref, row0, 2 * GROUP_W)
    bt = ld(bt_sc)
    bd16 = lambda xs: [_bd(x.astype(BF16), bdm16) for x in xs]
    pmul = lambda xs, ys: [_dot(x.astype(BF16), y) for x, y in zip(xs, bd16(ys))]

    gam_c = {c: _dot_sel(tri, lg_sc[c * CHUNK:(c + 1) * CHUNK, :]) for c in sorted({c for c, _ in items})}
    gam = [gam_c[c][:, p * PAIR_W:(p + 1) * PAIR_W] for c, p in items]
    gam_row = [jnp.sum(x * cm(_G_EYE), axis=0, keepdims=True) for x in gam]
    dincl = [jnp.exp((x - r) * cm(_G_INCL)) * cm(_G_INCL) for x, r in zip(gam, gam_row)]
    kb = [x.astype(BF16) for x in kp]
    k2 = [_bd(x, bdm16) for x in kb]
    kq = [_dot_nt(jnp.concatenate([a, b.astype(BF16)], axis=0), c) for a, b, c in zip(kb, qp, k2)]
    kk = [x[0:CHUNK] for x in kq]
    qk = [x[CHUNK:2 * CHUNK] for x in kq]
    m = [a * b * d * cm(_G_STRICT) for a, b, d in zip(kk, bt, dincl)]
    m2 = [jnp.concatenate([x, x], axis=0) for x in (y.astype(BF16) for y in m)]
    n1 = [x * cm(_G_NB8) for x in m]
    n2 = [_dot(a.astype(BF16), b * bdl(0)) for a, b in zip(n1, m2)]
    t_inv = [cm(_G_EYE) + x for x in n1]
    both = pmul([jnp.concatenate([a, b], axis=0) for a, b in zip(n2, t_inv)], n2)
    n4 = [x[0:CHUNK] for x in both]
    t_inv = [t + x[CHUNK:2 * CHUNK] for t, x in zip(t_inv, both)]
    t_inv = [t + d for t, d in zip(t_inv, pmul(t_inv, n4))]
    for lvl in (1, 2, 3):
        left = [_dot(t.astype(BF16), b * bdl(lvl)) for t, b in zip(t_inv, m2)]
        t_inv = [t - d for t, d in zip(t_inv, pmul(left, t_inv))]
    eg = [jnp.exp(x) for x in gam]
    rhs = [jnp.concatenate([_bd((v * b).astype(BF16), bdm16), _bd((k * b * e).astype(BF16), bdm16)], axis=1)
           for v, k, b, e in zip(vp, kp, bt, eg)]
    uw = [_dot(t.astype(BF16), r) for t, r in zip(t_inv, rhs)]
    last = 0 if reverse else CHUNK - 1
    g_last = [x[last:last + 1, :] for x in gam]
    return dict(
        u=[x[:, 0:PAIR_W] for x in uw],
        wq=[jnp.concatenate([x[:, PAIR_W:2 * PAIR_W], q * e], axis=0).astype(BF16) for x, q, e in zip(uw, qp, eg)],
        qkd=[(a * d).astype(BF16) for a, d in zip(qk, dincl)],
        kdt=[(k * jnp.exp(g - x)).T.astype(BF16) for k, g, x in zip(kp, g_last, gam)],
        decay=[jnp.exp(g) for g in g_last])


def _gdn_kernel(n_steps, n_sub, *refs):
    (qkv_f, ab_f, qkv_b, ab_b, alog_ref, dtb_ref, s0_ref, tri_ref, cm_ref, bdl_ref, bdm_ref, elg_ref, ebt_ref,
     of_ref, ob_ref, sfin_ref, lg_sc, bt_sc, s_sc) = refs
    step = pl.program_id(1)
    tt = qkv_f.shape[1] // n_sub
    n_chunks = tt // CHUNK

    @pl.when(step == 0)
    def _():
        s_sc[...] = s0_ref[0]

    bdm = bdm_ref[...]
    bdm16 = bdm.astype(BF16)
    tok = ((qkv_f, ab_f), (qkv_b, ab_b))
    o_refs = (of_ref, ob_ref)

    def sub_tile(j, carry):
        row0 = [pl.multiple_of(j * tt, tt), pl.multiple_of((n_sub - 1 - j) * tt, tt)]
        terms = []
        for d, reverse in enumerate((False, True)):
            qkv_ref, ab_ref = tok[d]
            _gdn_gates(ab_ref[0, pl.ds(row0[d], tt), :], alog_ref, dtb_ref, elg_ref[d], ebt_ref[d],
                       lg_sc.at[d], bt_sc.at[d])
            items = [(c, p) for c in range(n_chunks) for p in range(N_PAIRS)]
            cm = functools.partial(lambda dd, idx: cm_ref[dd, idx], d)
            terms.append(_gdn_chunk_terms(reverse, items, qkv_ref.at[0], row0[d], lg_sc.at[d], bt_sc.at[d],
                                          tri_ref[d], cm, lambda idx: bdl_ref[idx], bdm))

        s = [[s_sc[d, p] for p in range(N_PAIRS)] for d in range(2)]
        for ci in range(n_chunks):
            lanes = [(d, p, ((n_chunks - 1 - ci) if d else ci) * N_PAIRS + p)
                     for d in range(2) for p in range(N_PAIRS)]
            sq = [_dot(terms[d]["wq"][i], s[d][p].astype(BF16)) for d, p, i in lanes]
            v_new = [terms[d]["u"][i] - x[0:CHUNK] for (d, p, i), x in zip(lanes, sq)]
            v16 = [x.astype(BF16) for x in v_new]
            o = [x[CHUNK:2 * CHUNK] + _dot(terms[d]["qkd"][i], _bd(v, bdm16))
                 for (d, p, i), x, v in zip(lanes, sq, v16)]
            upd = [_dot(terms[d]["kdt"][i], v) * bdm for (d, p, i), v in zip(lanes, v16)]
            for (d, p, i), x, y in zip(lanes, o, upd):
                c = i // N_PAIRS
                o_refs[d][0, pl.ds(row0[d] + c * CHUNK, CHUNK), p * PAIR_W:(p + 1) * PAIR_W] = x
                s[d][p] = s[d][p] * terms[d]["decay"][i] + y
        for d in range(2):
            for p in range(N_PAIRS):
                s_sc[d, p] = s[d][p]
        return carry

    lax.fori_loop(0, n_sub, sub_tile, 0)

    @pl.when(step == n_steps - 1)
    def _():
        sfin_ref[0] = s_sc[...]


def _gdn_scan(qkv, ab, a_log_pad, dt_bias_pad, s0):
    b, t, _ = qkv.shape
    sub = min(GDN_TT, t)
    n_sub = min(GDN_NSUB, t // sub)
    tt = sub * n_sub
    n_tiles = t // tt
    consts = [_gdn_consts(rev) for rev in (False, True)]
    tri = np.stack([c[0] for c in consts])
    cm = np.stack([c[1] for c in consts])
    bdl = _gdn_level_masks()
    expand = [_gdn_expand(rev) for rev in (False, True)]
    e_lg = np.stack([e[0] for e in expand])
    e_bt = np.stack([e[1] for e in expand])
    const = lambda shape: pl.BlockSpec(shape, lambda bi, i: (0,) * len(shape))
    state_spec = pl.BlockSpec((1, 2, N_PAIRS, PAIR_W, PAIR_W), lambda bi, i: (bi, 0, 0, 0, 0))

    def tok_specs(tile_of):
        return [pl.BlockSpec((1, tt, 3 * GROUP_W), lambda bi, i: (bi, tile_of(i), 0)),
                pl.BlockSpec((1, tt, AB_PAD), lambda bi, i: (bi, tile_of(i), 0))]

    fwd_tile = lambda i: i
    bwd_tile = lambda i: n_tiles - 1 - i
    o_f, o_b, s_fin = pl.pallas_call(
        functools.partial(_gdn_kernel, n_tiles, n_sub),
        grid=(b, n_tiles),
        in_specs=tok_specs(fwd_tile) + tok_specs(bwd_tile)
                 + [const((1, AB_PAD)), const((1, AB_PAD)), state_spec,
                    const(tri.shape), const(cm.shape), const(bdl.shape), const((PAIR_W, PAIR_W)),
                    const(e_lg.shape), const(e_bt.shape)],
        out_specs=[pl.BlockSpec((1, tt, GROUP_W), lambda bi, i: (bi, fwd_tile(i), 0)),
                   pl.BlockSpec((1, tt, GROUP_W), lambda bi, i: (bi, bwd_tile(i), 0)),
                   state_spec],
        out_shape=[jax.ShapeDtypeStruct((b, t, GROUP_W), F32),
                   jax.ShapeDtypeStruct((b, t, GROUP_W), F32),
                   jax.ShapeDtypeStruct((b, 2, N_PAIRS, PAIR_W, PAIR_W), F32)],
        scratch_shapes=[pltpu.VMEM((2, sub, GROUP_W), F32)] * 2 + [pltpu.VMEM((2, N_PAIRS, PAIR_W, PAIR_W), F32)],
        compiler_params=_cparams(("arbitrary", "arbitrary")),
        name="gdn_scan",
    )(qkv, ab, qkv, ab, a_log_pad, dt_bias_pad, s0,
      jnp.asarray(tri, BF16), jnp.asarray(cm), jnp.asarray(bdl, BF16), jnp.asarray(_bd_mask()),
      jnp.asarray(e_lg, BF16), jnp.asarray(e_bt, BF16))
    return o_f, o_b, s_fin


def _hgrn_prep(reverse, layer, ph_ref, logits, q_sc, k_sc, lf_sc):
    e = jnp.exp(logits - jnp.max(logits, axis=0, keepdims=True))
    prob = e / jnp.sum(e, axis=0, keepdims=True)
    lb = jnp.maximum(jnp.sum(prob[0:layer + 1], axis=0, keepdims=True) - prob[0:1], 0.0)
    z_off = 2 * GROUP_W if reverse else GROUP_W
    q_sc[...] = _silu(ph_ref[:, 0:GROUP_W]) * (HEAD_DIM ** -0.5)
    sig = _sigmoid(ph_ref[:, z_off:z_off + GROUP_W])
    lf_sc[...] = jnp.log(jnp.maximum(lb, LB_FLOOR) + (1.0 - lb) * sig)
    k_sc[...] = (1.0 - lb) * (1.0 - sig)


def _hgrn_chunk_terms(reverse, n_chunks, ph_ref, q_sc, k_sc, lf_sc, b_sc, ez_sc, tri, hm, bdm16):
    n_lvl = len(_H_LEVELS)
    q_row0 = n_lvl * CHUNK
    k_row0 = (n_lvl + 1) * CHUNK
    last = 0 if reverse else CHUNK - 1
    rows = lambda c: slice(c * CHUNK, (c + 1) * CHUNK)
    lanes = lambda p: slice(p * PAIR_W, (p + 1) * PAIR_W)
    for c in range(n_chunks):
        lf = lf_sc[rows(c), :]
        b_sc[c] = _dot_sel(tri, lf)
        for n, ex in enumerate(_hgrn_exponents(reverse, b_sc.at[c], lf)):
            ez_sc[c, n * CHUNK:(n + 1) * CHUNK, :] = jnp.exp(ex.astype(BF16)).astype(ez_sc.dtype)
    items = [(c, p) for c in range(n_chunks) for p in range(N_PAIRS)]
    ez = lambda row0: [ez_sc[c, row0:row0 + CHUNK, lanes(p)].astype(BF16) for c, p in items]
    q16 = [q_sc[rows(c), lanes(p)].astype(BF16) for c, p in items]
    k16 = [k_sc[rows(c), lanes(p)].astype(BF16) for c, p in items]
    vp = [ph_ref[rows(c), 3 * GROUP_W + p * PAIR_W:3 * GROUP_W + (p + 1) * PAIR_W] for c, p in items]
    bdk = [_bd(k, bdm16) for k in k16]
    both = [_dot_nt(jnp.concatenate([q, q * e], axis=0), k) for q, k, e in zip(q16, bdk, ez(0))]
    a = [x[0:CHUNK] * hm(n_lvl) + x[CHUNK:2 * CHUNK] * hm(0) for x in both]
    for li in range(1, n_lvl):
        e16 = ez(li * CHUNK)
        sc = [_dot_nt(q * e, k * jnp.concatenate([e, e], axis=0)) for q, k, e in zip(q16, bdk, e16)]
        a = [x + y * hm(li) for x, y in zip(a, sc)]
    return dict(
        o=[_dot(x.astype(BF16), _bd(v.astype(BF16), bdm16)) for x, v in zip(a, vp)],
        qd=[q * e for q, e in zip(q16, ez(q_row0))],
        kd=[k * e for k, e in zip(k16, ez(k_row0))],
        vt=[v.T.astype(BF16) for v in vp],
        decay=[jnp.exp(b_sc[c, last:last + 1, lanes(p)]) for c, p in items])


def _hgrn_kernel(layer, n_steps, ph_f, ph_b, lbl_ref, s0_ref, tri_ref, hm_ref, bdm_ref,
                 of_ref, ob_ref, sfin_ref, q_sc, k_sc, lf_sc, b_sc, ez_sc, s_sc):
    step = pl.program_id(1)
    n_col, tt = ph_f.shape[1], ph_f.shape[2]
    n_chunks = tt // CHUNK

    @pl.when(step == 0)
    def _():
        s_sc[...] = s0_ref[0]

    bdm = bdm_ref[...]
    bdm16 = bdm.astype(BF16)
    o_refs = (of_ref, ob_ref)

    def column(j, carry):
        col = (j, n_col - 1 - j)
        ph = (ph_f.at[0, col[0]], ph_b.at[0, col[1]])
        terms = []
        for d, reverse in enumerate((False, True)):
            sc = (q_sc.at[d], k_sc.at[d], lf_sc.at[d])
            _hgrn_prep(reverse, layer, ph[d], lbl_ref[d], *sc)
            hm = functools.partial(lambda dd, idx: hm_ref[dd, idx], d)
            terms.append(_hgrn_chunk_terms(reverse, n_chunks, ph[d], *sc, b_sc.at[d], ez_sc.at[d], tri_ref[d], hm,
                                           bdm16))

        s = [[s_sc[d, p] for p in range(N_PAIRS)] for d in range(2)]
        for ci in range(n_chunks):
            lanes = [(d, p, ((n_chunks - 1 - ci) if d else ci) * N_PAIRS + p)
                     for d in range(2) for p in range(N_PAIRS)]
            o = [terms[d]["o"][i] + _dot_nt(terms[d]["qd"][i], s[d][p].astype(BF16)) for d, p, i in lanes]
            upd = [_dot(terms[d]["vt"][i], terms[d]["kd"][i]) * bdm for d, p, i in lanes]
            for (d, p, i), x, y in zip(lanes, o, upd):
                c = i // N_PAIRS
                o_refs[d][0, col[d], c * CHUNK:(c + 1) * CHUNK, p * PAIR_W:(p + 1) * PAIR_W] = x
                s[d][p] = s[d][p] * terms[d]["decay"][i] + y
        for d in range(2):
            for p in range(N_PAIRS):
                s_sc[d, p] = s[d][p]
        return carry

    lax.fori_loop(0, n_col, column, 0)

    @pl.when(step == n_steps - 1)
    def _():
        sfin_ref[0] = s_sc[...]


def _hgrn_scan(ph, lb_logits, s0, layer):
    b, n_tiles, tt, _ = ph.shape
    n_col = min(HGRN_NCOL, n_tiles)
    n_steps = n_tiles // n_col
    depth = lb_logits.shape[0]
    consts = [_hgrn_consts(rev) for rev in (False, True)]
    tri = np.stack([c[0] for c in consts])
    hm = np.stack([c[1] for c in consts])
    n_chunks = tt // CHUNK
    n_exp = len(_H_LEVELS) + 2
    const = lambda shape: pl.BlockSpec(shape, lambda bi, i: (0,) * len(shape))
    state_spec = pl.BlockSpec((1, 2, N_PAIRS, PAIR_W, PAIR_W), lambda bi, i: (bi, 0, 0, 0, 0))
    fwd = lambda w: pl.BlockSpec((1, n_col, tt, w), lambda bi, i: (bi, i, 0, 0))
    bwd = lambda w: pl.BlockSpec((1, n_col, tt, w), lambda bi, i: (bi, n_steps - 1 - i, 0, 0))
    return pl.pallas_call(
        functools.partial(_hgrn_kernel, layer, n_steps),
        grid=(b, n_steps),
        in_specs=[fwd(4 * GROUP_W), bwd(4 * GROUP_W), const((2, depth, GROUP_W)), state_spec,
                  const(tri.shape), const(hm.shape), const((PAIR_W, PAIR_W))],
        out_specs=[fwd(GROUP_W), bwd(GROUP_W), state_spec],
        out_shape=[jax.ShapeDtypeStruct((b, n_tiles, tt, GROUP_W), F32),
                   jax.ShapeDtypeStruct((b, n_tiles, tt, GROUP_W), F32),
                   jax.ShapeDtypeStruct((b, 2, N_PAIRS, PAIR_W, PAIR_W), F32)],
        scratch_shapes=[pltpu.VMEM((2, tt, GROUP_W), F32)] * 3
                       + [pltpu.VMEM((2, n_chunks, CHUNK, GROUP_W), F32),
                          pltpu.VMEM((2, n_chunks, n_exp * CHUNK, GROUP_W), F32),
                          pltpu.VMEM((2, N_PAIRS, PAIR_W, PAIR_W), F32)],
        compiler_params=_cparams(("arbitrary", "arbitrary")),
        name="hgrn_scan",
    )(ph, ph, jnp.transpose(lb_logits, (1, 0, 2)), s0, jnp.asarray(tri, BF16), jnp.asarray(hm),
      jnp.asarray(_bd_mask()))


def _mlp_kernel(final, col_major, x_ref, oaf_ref, oab_ref, obf_ref, obb_ref, ga_ref, gb_ref, mod_ref,
                gg_ref, hg_ref, n2g_ref, fg_ref, ones_ref, perm_ref, wo_ref, w1_ref, w2_ref, o_ref):
    ones = ones_ref[...]
    tm = x_ref.shape[1]
    oa = oaf_ref[0] + oab_ref[0]
    ob = (obf_ref[0] + obb_ref[0]).reshape(tm, GROUP_W)
    gb = gb_ref[0].reshape(tm, GROUP_W)
    inv_d = 1.0 / HEAD_DIM
    ya = oa * lax.rsqrt(_head_sums(oa * oa, ones) * inv_d + EPS) * gg_ref[...] * _silu(ga_ref[0])
    yb = ob * lax.rsqrt(_head_sums(ob * ob, ones) * inv_d + EPS) * hg_ref[...] * _sigmoid(gb)
    yb = yb.astype(BF16)
    if col_major:
        yb = _dot(perm_ref[...], yb).astype(BF16)
    y = _dot(ya.astype(BF16), wo_ref[0:GROUP_W, :]) + _dot(yb, wo_ref[GROUP_W:2 * GROUP_W, :])
    x1 = x_ref[0] + mod_ref[0, 2:3, :] * y
    h = _norm_mod(x1, n2g_ref[...], mod_ref[0, 3:4, :], mod_ref[0, 4:5, :]).astype(BF16)
    hid = jnp.maximum(_dot(h, w1_ref[...]), 0.0)
    x2 = x1 + mod_ref[0, 5:6, :] * _dot((hid * hid).astype(BF16), w2_ref[...])
    if final:
        x2 = x2 * lax.rsqrt(jnp.mean(x2 * x2, axis=-1, keepdims=True) + EPS) * fg_ref[...]
    o_ref[0] = x2


def _out_mlp(x, oa_f, oa_b, ob_f, ob_b, ga, gb, mod, gdn_g, hgrn_g, n2g, final_g, w_out, w1, w2,
             shared_mod, final, col_major):
    b, t, d = x.shape
    tm = MLP_TM if col_major else min(MLP_TM, t)
    mod_map = (lambda bi, i: (0, 0, 0)) if shared_mod else (lambda bi, i: (bi, 0, 0))
    tok = lambda w: pl.BlockSpec((1, tm, w), lambda bi, i: (bi, i, 0))
    col = pl.BlockSpec((1, GRID_W, tm // GRID_W, GROUP_W), lambda bi, i: (bi, 0, i, 0))
    hg_spec = col if col_major else tok(GROUP_W)
    const = lambda shape: pl.BlockSpec(shape, lambda bi, i: (0,) * len(shape), pipeline_mode=pl.Buffered(1))
    perm_t = jnp.asarray(_grid_perm(tm).T, BF16)
    return pl.pallas_call(
        functools.partial(_mlp_kernel, final, col_major),
        grid=(b, t // tm),
        in_specs=[tok(d), tok(GROUP_W), tok(GROUP_W), hg_spec, hg_spec, tok(GROUP_W), hg_spec,
                  pl.BlockSpec((1, N_MOD, d), mod_map),
                  const((1, GROUP_W)), const((1, GROUP_W)), const((1, d)), const((1, d)),
                  const((GROUP_W // 2, GROUP_W // 2)), const(perm_t.shape),
                  const(w_out.shape), const(w1.shape), const(w2.shape)],
        out_specs=tok(d),
        out_shape=jax.ShapeDtypeStruct((b, t, d), F32),
        compiler_params=_cparams(("arbitrary", "arbitrary")),
        name="out_mlp",
    )(x, oa_f, oa_b, ob_f, ob_b, ga, gb, mod, gdn_g, hgrn_g, n2g, final_g,
      jnp.asarray(_seg_ones(GROUP_W // 2), BF16), perm_t, w_out, w1, w2)


def _split_w_in(w):
    g = GROUP_W
    qkv, ga, ab = w[:, 0:3 * g], w[:, 3 * g:4 * g], w[:, 4 * g:4 * g + 4 * N_HEADS]
    rest = w[:, 4 * g + 4 * N_HEADS:]
    ph, gb = rest[:, 0:4 * g], rest[:, 4 * g:5 * g]
    ab = jnp.pad(ab, ((0, 0), (0, AB_PAD - 4 * N_HEADS)))
    return jnp.concatenate([qkv, ga, ab, ph, gb], axis=1).astype(BF16)


def _pad_lanes(v):
    flat = v.reshape(1, -1)
    return jnp.pad(flat, ((0, 0), (0, AB_PAD - flat.shape[1])))


def kernel(x, c, ctx, c_ctx, w_mod, b_mod, norm1_g, norm2_g, w_in, conv_w, a_log, dt_bias,
           gdn_norm_g, hgrn_norm_g, lb_logits, w_out, w_mlp1, w_mlp2, final_g):
    depth = w_mod.shape[0]
    b, _, d = x.shape
    cvec = jnp.concatenate([c, c_ctx[None, :], jnp.zeros((SUBLANES - b - 1, d), F32)], axis=0)
    mod = _modulation(cvec, w_mod, b_mod).reshape(depth, SUBLANES, N_MOD, d)
    zero_state = jnp.zeros((b, 2, N_PAIRS, PAIR_W, PAIR_W), F32)
    fg = final_g.reshape(1, d)

    x_lat, x_ctx = x, ctx
    for l in range(depth):
        need_ctx = l < depth - 1
        mod_lat, mod_ctx = mod[l, 0:b], mod[l, b:b + 1]
        w_cat = _split_w_in(w_in[l])
        n1g = norm1_g[l].reshape(1, d)
        a_pad, dt_pad = _pad_lanes(a_log[l]), _pad_lanes(dt_bias[l])
        gg = jnp.tile(gdn_norm_g[l], N_HEADS).reshape(1, GROUP_W)
        hg = jnp.tile(hgrn_norm_g[l], N_HEADS).reshape(1, GROUP_W)
        wo, w1, w2 = w_out[l].astype(BF16), w_mlp1[l].astype(BF16), w_mlp2[l].astype(BF16)

        qkv_c, ga_c, ab_c, ph_c, gb_c = _projection(x_ctx, mod_ctx, n1g, w_cat, conv_w[l], True, False)
        qkv_l, ga_l, ab_l, ph_l, gb_l = _projection(x_lat, mod_lat, n1g, w_cat, conv_w[l], False, True)

        oa_cf, oa_cb, sa = _gdn_scan(qkv_c, ab_c, a_pad, dt_pad, zero_state)
        oa_lf, oa_lb, _ = _gdn_scan(qkv_l, ab_l, a_pad, dt_pad, sa)

        t_ctx = ph_c.shape[1]
        tt_c = min(HGRN_TT, t_ctx)
        ob_cf, ob_cb, sb = _hgrn_scan(ph_c.reshape(b, t_ctx // tt_c, tt_c, 4 * GROUP_W), lb_logits, zero_state, l)
        ob_lf, ob_lb, _ = _hgrn_scan(ph_l, lb_logits, sb, l)

        n2g = norm2_g[l].reshape(1, d)
        x_lat = _out_mlp(x_lat, oa_lf, oa_lb, ob_lf, ob_lb, ga_l, gb_l, mod_lat, gg, hg, n2g, fg, wo, w1, w2,
                         False, not need_ctx, True)
        if need_ctx:
            x_ctx = _out_mlp(x_ctx, oa_cf, oa_cb, ob_cf.reshape(b, t_ctx, GROUP_W), ob_cb.reshape(b, t_ctx, GROUP_W),
                             ga_c, gb_c, mod_ctx, gg, hg, n2g, fg, wo, w1, w2, True, False, False)
    return x_lat
```

```python
import functools

import numpy as np
import jax
import jax.numpy as jnp
from jax import lax
from jax.experimental import pallas as pl
from jax.experimental.pallas import tpu as pltpu

F32 = jnp.float32
BF16 = jnp.bfloat16

SUBLANES = 8
HEAD_DIM = 64
N_HEADS = 8
GROUP_W = N_HEADS * HEAD_DIM
N_PAIRS = N_HEADS // 2
PAIR_W = 2 * HEAD_DIM
CHUNK = 64
GRID_W = 64
CONV_W = 3
N_MOD = 6
EPS = 1e-6
LB_FLOOR = 1e-30
AB_PAD = 128
VMEM_LIMIT = 56 * 1024 * 1024

PROJ_TM = 512
MLP_TM = 512
GDN_TT = 256
GDN_NSUB = 4
HGRN_TT = 128
HGRN_NCOL = 8


def _dot(a, b):
    return jnp.dot(a, b, preferred_element_type=F32)


def _dot_nt(a, b):
    return lax.dot_general(a, b, (((1,), (1,)), ((), ())), preferred_element_type=F32)


def _dot_sel(sel, x):
    hi = x.astype(BF16)
    lo = (x - hi.astype(F32)).astype(BF16)
    return _dot(sel, hi) + _dot(sel, lo)


def _dot_sel_rhs(x, sel):
    n = x.shape[0]
    hi = x.astype(BF16)
    lo = (x - hi.astype(F32)).astype(BF16)
    y = _dot(jnp.concatenate([hi, lo], axis=0), sel)
    return y[0:n] + y[n:2 * n]


def _bd(x, bdmask):
    return jnp.concatenate([x, x], axis=0) * bdmask.astype(x.dtype)


def _sigmoid(x):
    return 0.5 * jnp.tanh(0.5 * x) + 0.5


def _silu(x):
    return x * _sigmoid(x)


def _softplus(x):
    return jnp.maximum(x, 0.0) + jnp.log1p(jnp.exp(-jnp.abs(x)))


def _cparams(sem):
    return pltpu.CompilerParams(dimension_semantics=sem, vmem_limit_bytes=VMEM_LIMIT)


def _packed_ij():
    i = np.arange(CHUNK)[:, None]
    j = (np.arange(PAIR_W) % HEAD_DIM)[None, :]
    return i, j


def _bd_mask():
    r = np.arange(PAIR_W)[:, None] // HEAD_DIM
    c = np.arange(PAIR_W)[None, :] // HEAD_DIM
    return (r == c).astype(np.float32)


def _seg_ones(width):
    r = np.arange(width)[:, None] // HEAD_DIM
    c = np.arange(width)[None, :] // HEAD_DIM
    return (r == c).astype(np.float32)


_G_INCL, _G_STRICT, _G_EYE, _G_NB8 = range(4)


def _gdn_consts(reverse):
    i, j = _packed_ij()
    t = np.arange(CHUNK)
    if reverse:
        tri = (t[None, :] >= t[:, None])
        incl = j >= i
        strict = j > i
    else:
        tri = (t[None, :] <= t[:, None])
        incl = j <= i
        strict = j < i
    b8 = (i // 8) == (j // 8)
    stack = np.stack([incl, strict, i == j, b8]).astype(np.float32)
    stack[_G_NB8] *= -1.0
    return tri.astype(np.float32), stack


def _gdn_level_masks():
    i, j = _packed_ij()
    b8 = (i // 8) == (j // 8)
    c16 = ((i // 16) == (j // 16)) & ~b8
    c32 = ((i // 32) == (j // 32)) & ((i // 16) != (j // 16))
    c64 = (i // 32) != (j // 32)
    bd = _bd_mask()
    lv = [-b8.astype(np.float32)] + [c.astype(np.float32) for c in (c16, c32, c64)]
    return np.stack([np.concatenate([x, x], axis=0) * bd for x in lv])


def _gdn_expand(reverse):
    d = 1 if reverse else 0
    col = np.arange(AB_PAD)[:, None]
    head = (np.arange(GROUP_W) // HEAD_DIM)[None, :]
    e_lg = (col == d * N_HEADS + head)
    e_bt = (col == 2 * N_HEADS + d * N_HEADS + head)
    return e_lg.astype(np.float32), e_bt.astype(np.float32)


_H_LEVELS = (1, 2, 4, 8, 16, 32)


def _hgrn_consts(reverse):
    r = np.arange(CHUNK)[:, None]
    t = np.arange(CHUNK)[None, :]
    i, j = _packed_ij()
    q_par, k_par = (0, 1) if reverse else (1, 0)
    masks = [((i // (2 * s)) == (j // (2 * s))) & (((i // s) % 2) == q_par) & (((j // s) % 2) == k_par)
             for s in _H_LEVELS]
    masks.append(i == j)
    tri = (t >= r) if reverse else (t <= r)
    return tri.astype(np.float32), np.stack(masks).astype(np.float32)


def _hgrn_exponents(reverse, b_ref, lf):
    w = lf.shape[1]
    gr = SUBLANES
    rig = lax.broadcasted_iota(jnp.int32, (gr, w), 0)
    grp = lambda g: b_ref[gr * g:gr * g + gr, :]
    row = lambda r: b_ref[r:r + 1, :]
    n_grp = CHUNK // gr
    q_par = 0 if reverse else 1
    sign = lambda s: jnp.where(((rig // s) % 2) == q_par, 1.0, -1.0)
    out = []
    r64 = lax.broadcasted_iota(jnp.int32, lf.shape, 0)
    out.append(jnp.where((r64 % 2) == q_par, lf, 0.0))
    bnd2 = (2, 6) if reverse else (1, 5)
    sg2, sg4 = sign(2), sign(4)
    out.append(jnp.concatenate(
        [(grp(g) - jnp.where(rig < 4, row(gr * g + bnd2[0]), row(gr * g + bnd2[1]))) * sg2 for g in range(n_grp)],
        axis=0))
    bnd4 = 4 if reverse else 3
    out.append(jnp.concatenate([(grp(g) - row(gr * g + bnd4)) * sg4 for g in range(n_grp)], axis=0))
    for s in (8, 16, 32):
        parts = []
        for g in range(n_grp):
            blk = (gr * g) // s
            bnd = 2 * s * (blk // 2) + (s if reverse else s - 1)
            parts.append(grp(g) - row(bnd) if (blk % 2) == q_par else row(bnd) - grp(g))
        out.append(jnp.concatenate(parts, axis=0))
    out.append(b_ref[...])
    out.append(row(0 if reverse else CHUNK - 1) - b_ref[...])
    return out


def _mod_kernel(c_ref, w_ref, b_ref, o_ref):
    sc = _silu(c_ref[...]).astype(BF16)
    o_ref[0] = _dot(sc, w_ref[0].astype(BF16)) + b_ref[0]


def _modulation(cvec, w_mod, b_mod):
    depth, d, n = w_mod.shape
    tn = 1536
    return pl.pallas_call(
        _mod_kernel,
        grid=(depth, n // tn),
        in_specs=[pl.BlockSpec((SUBLANES, d), lambda l, j: (0, 0)),
                  pl.BlockSpec((1, d, tn), lambda l, j: (l, 0, j)),
                  pl.BlockSpec((1, 1, tn), lambda l, j: (l, 0, j))],
        out_specs=pl.BlockSpec((1, SUBLANES, tn), lambda l, j: (l, 0, j)),
        out_shape=jax.ShapeDtypeStruct((depth, SUBLANES, n), F32),
        compiler_params=_cparams(("arbitrary", "arbitrary")),
        name="modulation",
    )(cvec, w_mod, b_mod.reshape(depth, 1, n))


_PROJ_WIDTHS = (3 * GROUP_W, GROUP_W, AB_PAD, 4 * GROUP_W, GROUP_W)
_N_ROW_MAJOR = 3


def _norm_mod(x, g, shift, scale):
    y = x * lax.rsqrt(jnp.mean(x * x, axis=-1, keepdims=True) + EPS) * g
    return y * (1.0 + scale) + shift


def _grid_perm(tm):
    rows = tm // GRID_W
    n = np.arange(tm)
    src = (n % rows) * GRID_W + n // rows
    p = np.zeros((tm, tm), np.float32)
    p[n, src] = 1.0
    return p


def _head_sums(x, ones):
    half = ones.shape[0]
    return jnp.concatenate([_dot(x[:, 0:half].astype(BF16), ones), _dot(x[:, half:2 * half].astype(BF16), ones)],
                           axis=1)


def _proj_kernel(col_major, n_tiles, x_ref, xp_ref, xn_ref, mod_ref, g_ref, w_ref, perm_ref, convw_ref, ones_ref,
                 *out_refs):
    tile = pl.program_id(1)
    g, shift, scale = g_ref[...], mod_ref[0, 0:1, :], mod_ref[0, 1:2, :]
    h = _norm_mod(x_ref[0], g, shift, scale).astype(BF16)
    tm = h.shape[0]

    qkv_ref = out_refs[0]
    halo = jnp.concatenate([xp_ref[0], xn_ref[0]], axis=0)
    h_halo = _norm_mod(halo, g, shift, scale).astype(BF16)
    groups = [slice(n * GROUP_W, (n + 1) * GROUP_W) for n in range(3)]
    p_halo = [_dot(h_halo, w_ref[:, c]) for c in groups]
    p = [_dot(h, w_ref[:, c]) for c in groups]

    row = lax.broadcasted_iota(jnp.int32, (tm, 1), 0)
    y = []
    for c, pc, ph in zip(groups, p, p_halo):
        pv = jnp.where(tile > 0, ph[SUBLANES - 1:SUBLANES, :], 0.0)
        nx = jnp.where(tile < n_tiles - 1, ph[SUBLANES:SUBLANES + 1, :], 0.0)
        p_prev = jnp.where(row == 0, pv, pltpu.roll(pc, 1, axis=0))
        p_next = jnp.where(row == tm - 1, nx, pltpu.roll(pc, tm - 1, axis=0))
        y.append(_silu(convw_ref[0:1, c] * p_prev + convw_ref[1:2, c] * pc + convw_ref[2:3, c] * p_next))
    off = _PROJ_WIDTHS[0]
    for n, (ref, w) in enumerate(zip(out_refs, _PROJ_WIDTHS)):
        if n == 0:
            continue
        if n == _N_ROW_MAJOR and col_major:
            h = _dot(perm_ref[...], h).astype(BF16)
        val = _dot(h, w_ref[:, off:off + w])
        ref[0] = val.reshape(ref.shape[1:])
        off += w

    ones = ones_ref[...]
    q, k, v = y
    qkv_ref[0, :, groups[2]] = v
    qkv_ref[0, :, groups[0]] = q * lax.rsqrt(_head_sums(q * q, ones) + EPS) * (HEAD_DIM ** -0.5)
    qkv_ref[0, :, groups[1]] = k * lax.rsqrt(_head_sums(k * k, ones) + EPS)


def _projection(x, mod, g, w_cat, conv_w, shared_mod, col_major):
    b, t, d = x.shape
    tm = PROJ_TM if col_major else min(PROJ_TM, t)
    rows_t = tm // GRID_W
    n_tiles = t // tm
    hb = tm // SUBLANES
    n_hb = t // SUBLANES
    mod_map = (lambda bi, i: (0, 0, 0)) if shared_mod else (lambda bi, i: (bi, 0, 0))
    tok = lambda w: pl.BlockSpec((1, tm, w), lambda bi, i: (bi, i, 0))
    const = lambda shape: pl.BlockSpec(shape, lambda bi, i: (0,) * len(shape), pipeline_mode=pl.Buffered(1))
    out_specs = [tok(w) for w in _PROJ_WIDTHS[:_N_ROW_MAJOR]]
    out_shape = [jax.ShapeDtypeStruct((b, t, w), F32) for w in _PROJ_WIDTHS[:_N_ROW_MAJOR]]
    for w in _PROJ_WIDTHS[_N_ROW_MAJOR:]:
        if col_major:
            out_specs.append(pl.BlockSpec((1, GRID_W, rows_t, w), lambda bi, i: (bi, 0, i, 0)))
            out_shape.append(jax.ShapeDtypeStruct((b, GRID_W, t // GRID_W, w), F32))
        else:
            out_specs.append(tok(w))
            out_shape.append(jax.ShapeDtypeStruct((b, t, w), F32))
    perm = jnp.asarray(_grid_perm(tm), BF16)
    ones = jnp.asarray(_seg_ones(GROUP_W // 2), BF16)
    return pl.pallas_call(
        functools.partial(_proj_kernel, col_major, n_tiles),
        grid=(b, n_tiles),
        in_specs=[tok(d),
                  pl.BlockSpec((1, SUBLANES, d), lambda bi, i: (bi, jnp.maximum(i * hb - 1, 0), 0)),
                  pl.BlockSpec((1, SUBLANES, d), lambda bi, i: (bi, jnp.minimum((i + 1) * hb, n_hb - 1), 0)),
                  pl.BlockSpec((1, N_MOD, d), mod_map),
                  const((1, d)), const(w_cat.shape), const(perm.shape), const(conv_w.shape), const(ones.shape)],
        out_specs=out_specs,
        out_shape=out_shape,
        compiler_params=_cparams(("arbitrary", "arbitrary")),
        name="projection",
    )(x, x, x, mod, g, w_cat, perm, conv_w, ones)


def _gdn_gates(ab, alog_ref, dtb_ref, elg, ebt, lg_sc, bt_sc):
    lg_c = -jnp.exp(alog_ref[...]) * _softplus(ab + dtb_ref[...])
    lg_sc[...] = _dot_sel_rhs(lg_c, elg)
    bt_sc[...] = _dot_sel_rhs(_sigmoid(ab), ebt)


def _gdn_chunk_terms(reverse, items, qkv_ref, row0, lg_sc, bt_sc, tri, cm, bdl, bdm):
    bdm16 = bdm.astype(BF16)
    ld = lambda ref, base=0, off=0: [ref[pl.ds(base + c * CHUNK, CHUNK), off + p * PAIR_W:off + (p + 1) * PAIR_W]
                                     for c, p in items]
    qp, kp, vp = ld(qkv_ref, row0), ld(qkv_ref, row0, GROUP_W), ld(qkv_ref, row0, 2 * GROUP_W)
    bt = ld(bt_sc)
    bd16 = lambda xs: [_bd(x.astype(BF16), bdm16) for x in xs]
    pmul = lambda xs, ys: [_dot(x.astype(BF16), y) for x, y in zip(xs, bd16(ys))]

    gam_c = {c: _dot_sel(tri, lg_sc[c * CHUNK:(c + 1) * CHUNK, :]) for c in sorted({c for c, _ in items})}
    gam = [gam_c[c][:, p * PAIR_W:(p + 1) * PAIR_W] for c, p in items]
    gam_row = [jnp.sum(x * cm(_G_EYE), axis=0, keepdims=True) for x in gam]
    dincl = [jnp.exp((x - r) * cm(_G_INCL)) * cm(_G_INCL) for x, r in zip(gam, gam_row)]
    ktbd = [jnp.concatenate([x, x], axis=0).T.astype(BF16) * bdm16 for x in kp]
    kq = [_dot(jnp.concatenate([a.astype(BF16), b.astype(BF16)], axis=0), c) for a, b, c in zip(kp, qp, ktbd)]
    kk = [x[0:CHUNK] for x in kq]
    qk = [x[CHUNK:2 * CHUNK] for x in kq]
    m = [a * b * d * cm(_G_STRICT) for a, b, d in zip(kk, bt, dincl)]
    m2 = [jnp.concatenate([x, x], axis=0) for x in (y.astype(BF16) for y in m)]
    n1 = [x * cm(_G_NB8) for x in m]
    n2 = [_dot(a.astype(BF16), b * bdl(0)) for a, b in zip(n1, m2)]
    t_inv = [cm(_G_EYE) + x for x in n1]
    both = pmul([jnp.concatenate([a, b], axis=0) for a, b in zip(n2, t_inv)], n2)
    n4 = [x[0:CHUNK] for x in both]
    t_inv = [t + x[CHUNK:2 * CHUNK] for t, x in zip(t_inv, both)]
    t_inv = [t + d for t, d in zip(t_inv, pmul(t_inv, n4))]
    for lvl in (1, 2, 3):
        left = [_dot(t.astype(BF16), b * bdl(lvl)) for t, b in zip(t_inv, m2)]
        t_inv = [t - d for t, d in zip(t_inv, pmul(left, t_inv))]
    eg = [jnp.exp(x) for x in gam]
    rhs = [jnp.concatenate([_bd((v * b).astype(BF16), bdm16), _bd((k * b * e).astype(BF16), bdm16)], axis=1)
           for v, k, b, e in zip(vp, kp, bt, eg)]
    uw = [_dot(t.astype(BF16), r) for t, r in zip(t_inv, rhs)]
    last = 0 if reverse else CHUNK - 1
    g_last = [x[last:last + 1, :] for x in gam]
    return dict(
        u=[x[:, 0:PAIR_W] for x in uw],
        wq=[jnp.concatenate([x[:, PAIR_W:2 * PAIR_W], q * e], axis=0).astype(BF16) for x, q, e in zip(uw, qp, eg)],
        qkd=[(a * d).astype(BF16) for a, d in zip(qk, dincl)],
        kdt=[k * jnp.exp(g - r).astype(BF16) for k, g, r in zip(ktbd, g_last, gam_row)],
        decay=[jnp.exp(g) for g in g_last])


def _gdn_kernel(n_steps, n_sub, *refs):
    (qkv_f, ab_f, qkv_b, ab_b, alog_ref, dtb_ref, s0_ref, tri_ref, cm_ref, bdl_ref, bdm_ref, elg_ref, ebt_ref,
     of_ref, ob_ref, sfin_ref, lg_sc, bt_sc, s_sc) = refs
    step = pl.program_id(1)
    tt = qkv_f.shape[1] // n_sub
    n_chunks = tt // CHUNK

    @pl.when(step == 0)
    def _():
        s_sc[...] = s0_ref[0]

    bdm = bdm_ref[...]
    bdm16 = bdm.astype(BF16)
    tok = ((qkv_f, ab_f), (qkv_b, ab_b))
    o_refs = (of_ref, ob_ref)

    def sub_tile(j, carry):
        row0 = [pl.multiple_of(j * tt, tt), pl.multiple_of((n_sub - 1 - j) * tt, tt)]
        terms = []
        for d, reverse in enumerate((False, True)):
            qkv_ref, ab_ref = tok[d]
            _gdn_gates(ab_ref[0, pl.ds(row0[d], tt), :], alog_ref, dtb_ref, elg_ref[d], ebt_ref[d],
                       lg_sc.at[d], bt_sc.at[d])
            items = [(c, p) for c in range(n_chunks) for p in range(N_PAIRS)]
            cm = functools.partial(lambda dd, idx: cm_ref[dd, idx], d)
            terms.append(_gdn_chunk_terms(reverse, items, qkv_ref.at[0], row0[d], lg_sc.at[d], bt_sc.at[d],
                                          tri_ref[d], cm, lambda idx: bdl_ref[idx], bdm))

        s = [[s_sc[d, p] for p in range(N_PAIRS)] for d in range(2)]
        for ci in range(n_chunks):
            lanes = [(d, p, ((n_chunks - 1 - ci) if d else ci) * N_PAIRS + p)
                     for d in range(2) for p in range(N_PAIRS)]
            sq = [_dot(terms[d]["wq"][i], s[d][p].astype(BF16)) for d, p, i in lanes]
            v_new = [terms[d]["u"][i] - x[0:CHUNK] for (d, p, i), x in zip(lanes, sq)]
            v16 = [x.astype(BF16) for x in v_new]
            ou = [_dot(jnp.concatenate([terms[d]["qkd"][i], terms[d]["kdt"][i]], axis=0), _bd(v, bdm16))
                  for (d, p, i), v in zip(lanes, v16)]
            for (d, p, i), x, y in zip(lanes, sq, ou):
                c = i // N_PAIRS
                o_refs[d][0, pl.ds(row0[d] + c * CHUNK, CHUNK), p * PAIR_W:(p + 1) * PAIR_W] = (
                    x[CHUNK:2 * CHUNK] + y[0:CHUNK])
                s[d][p] = s[d][p] * terms[d]["decay"][i] + y[CHUNK:CHUNK + PAIR_W]
        for d in range(2):
            for p in range(N_PAIRS):
                s_sc[d, p] = s[d][p]
        return carry

    lax.fori_loop(0, n_sub, sub_tile, 0)

    @pl.when(step == n_steps - 1)
    def _():
        sfin_ref[0] = s_sc[...]


def _gdn_scan(qkv, ab, a_log_pad, dt_bias_pad, s0):
    b, t, _ = qkv.shape
    sub = min(GDN_TT, t)
    n_sub = min(GDN_NSUB, t // sub)
    tt = sub * n_sub
    n_tiles = t // tt
    consts = [_gdn_consts(rev) for rev in (False, True)]
    tri = np.stack([c[0] for c in consts])
    cm = np.stack([c[1] for c in consts])
    bdl = _gdn_level_masks()
    expand = [_gdn_expand(rev) for rev in (False, True)]
    e_lg = np.stack([e[0] for e in expand])
    e_bt = np.stack([e[1] for e in expand])
    const = lambda shape: pl.BlockSpec(shape, lambda bi, i: (0,) * len(shape))
    state_spec = pl.BlockSpec((1, 2, N_PAIRS, PAIR_W, PAIR_W), lambda bi, i: (bi, 0, 0, 0, 0))

    def tok_specs(tile_of):
        return [pl.BlockSpec((1, tt, 3 * GROUP_W), lambda bi, i: (bi, tile_of(i), 0)),
                pl.BlockSpec((1, tt, AB_PAD), lambda bi, i: (bi, tile_of(i), 0))]

    fwd_tile = lambda i: i
    bwd_tile = lambda i: n_tiles - 1 - i
    o_f, o_b, s_fin = pl.pallas_call(
        functools.partial(_gdn_kernel, n_tiles, n_sub),
        grid=(b, n_tiles),
        in_specs=tok_specs(fwd_tile) + tok_specs(bwd_tile)
                 + [const((1, AB_PAD)), const((1, AB_PAD)), state_spec,
                    const(tri.shape), const(cm.shape), const(bdl.shape), const((PAIR_W, PAIR_W)),
                    const(e_lg.shape), const(e_bt.shape)],
        out_specs=[pl.BlockSpec((1, tt, GROUP_W), lambda bi, i: (bi, fwd_tile(i), 0)),
                   pl.BlockSpec((1, tt, GROUP_W), lambda bi, i: (bi, bwd_tile(i), 0)),
                   state_spec],
        out_shape=[jax.ShapeDtypeStruct((b, t, GROUP_W), F32),
                   jax.ShapeDtypeStruct((b, t, GROUP_W), F32),
                   jax.ShapeDtypeStruct((b, 2, N_PAIRS, PAIR_W, PAIR_W), F32)],
        scratch_shapes=[pltpu.VMEM((2, sub, GROUP_W), F32)] * 2 + [pltpu.VMEM((2, N_PAIRS, PAIR_W, PAIR_W), F32)],
        compiler_params=_cparams(("arbitrary", "arbitrary")),
        name="gdn_scan",
    )(qkv, ab, qkv, ab, a_log_pad, dt_bias_pad, s0,
      jnp.asarray(tri, BF16), jnp.asarray(cm), jnp.asarray(bdl, BF16), jnp.asarray(_bd_mask()),
      jnp.asarray(e_lg, BF16), jnp.asarray(e_bt, BF16))
    return o_f, o_b, s_fin


def _hgrn_prep(reverse, layer, ph_ref, logits, q_sc, k_sc, lf_sc):
    e = jnp.exp(logits - jnp.max(logits, axis=0, keepdims=True))
    prob = e / jnp.sum(e, axis=0, keepdims=True)
    lb = jnp.maximum(jnp.sum(prob[0:layer + 1], axis=0, keepdims=True) - prob[0:1], 0.0)
    z_off = 2 * GROUP_W if reverse else GROUP_W
    q_sc[...] = _silu(ph_ref[:, 0:GROUP_W]) * (HEAD_DIM ** -0.5)
    sig = _sigmoid(ph_ref[:, z_off:z_off + GROUP_W])
    lf_sc[...] = jnp.log(jnp.maximum(lb, LB_FLOOR) + (1.0 - lb) * sig)
    k_sc[...] = (1.0 - lb) * (1.0 - sig)


def _hgrn_chunk_terms(reverse, n_chunks, ph_ref, q_sc, k_sc, lf_sc, b_sc, ez_sc, tri, hm, bdm16):
    n_lvl = len(_H_LEVELS)
    q_row0 = n_lvl * CHUNK
    k_row0 = (n_lvl + 1) * CHUNK
    last = 0 if reverse else CHUNK - 1
    rows = lambda c: slice(c * CHUNK, (c + 1) * CHUNK)
    lanes = lambda p: slice(p * PAIR_W, (p + 1) * PAIR_W)
    for c in range(n_chunks):
        lf = lf_sc[rows(c), :]
        b_sc[c] = _dot_sel(tri, lf)
        for n, ex in enumerate(_hgrn_exponents(reverse, b_sc.at[c], lf)):
            ez_sc[c, n * CHUNK:(n + 1) * CHUNK, :] = jnp.exp(ex.astype(BF16)).astype(ez_sc.dtype)
    items = [(c, p) for c in range(n_chunks) for p in range(N_PAIRS)]
    ez = lambda row0: [ez_sc[c, row0:row0 + CHUNK, lanes(p)].astype(BF16) for c, p in items]
    q16 = [q_sc[rows(c), lanes(p)].astype(BF16) for c, p in items]
    k16 = [k_sc[rows(c), lanes(p)].astype(BF16) for c, p in items]
    vp = [ph_ref[rows(c), 3 * GROUP_W + p * PAIR_W:3 * GROUP_W + (p + 1) * PAIR_W] for c, p in items]
    bdk = [_bd(k, bdm16) for k in k16]
    both = [_dot_nt(jnp.concatenate([q, q * e], axis=0), k) for q, k, e in zip(q16, bdk, ez(0))]
    a = [x[0:CHUNK] * hm(n_lvl) + x[CHUNK:2 * CHUNK] * hm(0) for x in both]
    for li in range(1, n_lvl):
        e16 = ez(li * CHUNK)
        sc = [_dot_nt(q * e, k * jnp.concatenate([e, e], axis=0)) for q, k, e in zip(q16, bdk, e16)]
        a = [x + y * hm(li) for x, y in zip(a, sc)]
    return dict(
        o=[_dot(x.astype(BF16), _bd(v.astype(BF16), bdm16)) for x, v in zip(a, vp)],
        qd=[q * e for q, e in zip(q16, ez(q_row0))],
        kd=[k * e for k, e in zip(k16, ez(k_row0))],
        vt=[v.T.astype(BF16) for v in vp],
        decay=[jnp.exp(b_sc[c, last:last + 1, lanes(p)]) for c, p in items])


def _hgrn_kernel(layer, n_steps, ph_f, ph_b, lbl_ref, s0_ref, tri_ref, hm_ref, bdm_ref,
                 of_ref, ob_ref, sfin_ref, q_sc, k_sc, lf_sc, b_sc, ez_sc, s_sc):
    step = pl.program_id(1)
    n_col, tt = ph_f.shape[1], ph_f.shape[2]
    n_chunks = tt // CHUNK

    @pl.when(step == 0)
    def _():
        s_sc[...] = s0_ref[0]

    bdm = bdm_ref[...]
    bdm16 = bdm.astype(BF16)
    o_refs = (of_ref, ob_ref)

    def column(j, carry):
        col = (j, n_col - 1 - j)
        ph = (ph_f.at[0, col[0]], ph_b.at[0, col[1]])
        terms = []
        for d, reverse in enumerate((False, True)):
            sc = (q_sc.at[d], k_sc.at[d], lf_sc.at[d])
            _hgrn_prep(reverse, layer, ph[d], lbl_ref[d], *sc)
            hm = functools.partial(lambda dd, idx: hm_ref[dd, idx], d)
            terms.append(_hgrn_chunk_terms(reverse, n_chunks, ph[d], *sc, b_sc.at[d], ez_sc.at[d], tri_ref[d], hm,
                                           bdm16))

        s = [[s_sc[d, p] for p in range(N_PAIRS)] for d in range(2)]
        for ci in range(n_chunks):
            lanes = [(d, p, ((n_chunks - 1 - ci) if d else ci) * N_PAIRS + p)
                     for d in range(2) for p in range(N_PAIRS)]
            o = [terms[d]["o"][i] + _dot_nt(terms[d]["qd"][i], s[d][p].astype(BF16)) for d, p, i in lanes]
            upd = [_dot(terms[d]["vt"][i], terms[d]["kd"][i]) * bdm for d, p, i in lanes]
            for (d, p, i), x, y in zip(lanes, o, upd):
                c = i // N_PAIRS
                o_refs[d][0, col[d], c * CHUNK:(c + 1) * CHUNK, p * PAIR_W:(p + 1) * PAIR_W] = x
                s[d][p] = s[d][p] * terms[d]["decay"][i] + y
        for d in range(2):
            for p in range(N_PAIRS):
                s_sc[d, p] = s[d][p]
        return carry

    lax.fori_loop(0, n_col, column, 0)

    @pl.when(step == n_steps - 1)
    def _():
        sfin_ref[0] = s_sc[...]


def _hgrn_scan(ph, lb_logits, s0, layer):
    b, n_tiles, tt, _ = ph.shape
    n_col = min(HGRN_NCOL, n_tiles)
    n_steps = n_tiles // n_col
    depth = lb_logits.shape[0]
    consts = [_hgrn_consts(rev) for rev in (False, True)]
    tri = np.stack([c[0] for c in consts])
    hm = np.stack([c[1] for c in consts])
    n_chunks = tt // CHUNK
    n_exp = len(_H_LEVELS) + 2
    const = lambda shape: pl.BlockSpec(shape, lambda bi, i: (0,) * len(shape))
    state_spec = pl.BlockSpec((1, 2, N_PAIRS, PAIR_W, PAIR_W), lambda bi, i: (bi, 0, 0, 0, 0))
    fwd = lambda w: pl.BlockSpec((1, n_col, tt, w), lambda bi, i: (bi, i, 0, 0))
    bwd = lambda w: pl.BlockSpec((1, n_col, tt, w), lambda bi, i: (bi, n_steps - 1 - i, 0, 0))
    return pl.pallas_call(
        functools.partial(_hgrn_kernel, layer, n_steps),
        grid=(b, n_steps),
        in_specs=[fwd(4 * GROUP_W), bwd(4 * GROUP_W), const((2, depth, GROUP_W)), state_spec,
                  const(tri.shape), const(hm.shape), const((PAIR_W, PAIR_W))],
        out_specs=[fwd(GROUP_W), bwd(GROUP_W), state_spec],
        out_shape=[jax.ShapeDtypeStruct((b, n_tiles, tt, GROUP_W), F32),
                   jax.ShapeDtypeStruct((b, n_tiles, tt, GROUP_W), F32),
                   jax.ShapeDtypeStruct((b, 2, N_PAIRS, PAIR_W, PAIR_W), F32)],
        scratch_shapes=[pltpu.VMEM((2, tt, GROUP_W), F32)] * 3
                       + [pltpu.VMEM((2, n_chunks, CHUNK, GROUP_W), F32),
                          pltpu.VMEM((2, n_chunks, n_exp * CHUNK, GROUP_W), F32),
                          pltpu.VMEM((2, N_PAIRS, PAIR_W, PAIR_W), F32)],
        compiler_params=_cparams(("arbitrary", "arbitrary")),
        name="hgrn_scan",
    )(ph, ph, jnp.transpose(lb_logits, (1, 0, 2)), s0, jnp.asarray(tri, BF16), jnp.asarray(hm),
      jnp.asarray(_bd_mask()))


def _mlp_kernel(final, col_major, x_ref, oaf_ref, oab_ref, obf_ref, obb_ref, ga_ref, gb_ref, mod_ref,
                gg_ref, hg_ref, n2g_ref, fg_ref, ones_ref, perm_ref, wo_ref, w1_ref, w2_ref, o_ref):
    ones = ones_ref[...]
    tm = x_ref.shape[1]
    oa = oaf_ref[0] + oab_ref[0]
    ob = (obf_ref[0] + obb_ref[0]).reshape(tm, GROUP_W)
    gb = gb_ref[0].reshape(tm, GROUP_W)
    inv_d = 1.0 / HEAD_DIM
    ya = oa * lax.rsqrt(_head_sums(oa * oa, ones) * inv_d + EPS) * gg_ref[...] * _silu(ga_ref[0])
    yb = ob * lax.rsqrt(_head_sums(ob * ob, ones) * inv_d + EPS) * hg_ref[...] * _sigmoid(gb)
    yb = yb.astype(BF16)
    if col_major:
        yb = _dot(perm_ref[...], yb).astype(BF16)
    y = _dot(ya.astype(BF16), wo_ref[0:GROUP_W, :]) + _dot(yb, wo_ref[GROUP_W:2 * GROUP_W, :])
    x1 = x_ref[0] + mod_ref[0, 2:3, :] * y
    h = _norm_mod(x1, n2g_ref[...], mod_ref[0, 3:4, :], mod_ref[0, 4:5, :]).astype(BF16)
    hid = jnp.maximum(_dot(h, w1_ref[...]), 0.0)
    x2 = x1 + mod_ref[0, 5:6, :] * _dot((hid * hid).astype(BF16), w2_ref[...])
    if final:
        x2 = x2 * lax.rsqrt(jnp.mean(x2 * x2, axis=-1, keepdims=True) + EPS) * fg_ref[...]
    o_ref[0] = x2


def _out_mlp(x, oa_f, oa_b, ob_f, ob_b, ga, gb, mod, gdn_g, hgrn_g, n2g, final_g, w_out, w1, w2,
             shared_mod, final, col_major):
    b, t, d = x.shape
    tm = MLP_TM if col_major else min(MLP_TM, t)
    mod_map = (lambda bi, i: (0, 0, 0)) if shared_mod else (lambda bi, i: (bi, 0, 0))
    tok = lambda w: pl.BlockSpec((1, tm, w), lambda bi, i: (bi, i, 0))
    col = pl.BlockSpec((1, GRID_W, tm // GRID_W, GROUP_W), lambda bi, i: (bi, 0, i, 0))
    hg_spec = col if col_major else tok(GROUP_W)
    const = lambda shape: pl.BlockSpec(shape, lambda bi, i: (0,) * len(shape), pipeline_mode=pl.Buffered(1))
    perm_t = jnp.asarray(_grid_perm(tm).T, BF16)
    return pl.pallas_call(
        functools.partial(_mlp_kernel, final, col_major),
        grid=(b, t // tm),
        in_specs=[tok(d), tok(GROUP_W), tok(GROUP_W), hg_spec, hg_spec, tok(GROUP_W), hg_spec,
                  pl.BlockSpec((1, N_MOD, d), mod_map),
                  const((1, GROUP_W)), const((1, GROUP_W)), const((1, d)), const((1, d)),
                  const((GROUP_W // 2, GROUP_W // 2)), const(perm_t.shape),
                  const(w_out.shape), const(w1.shape), const(w2.shape)],
        out_specs=tok(d),
        out_shape=jax.ShapeDtypeStruct((b, t, d), F32),
        compiler_params=_cparams(("arbitrary", "arbitrary")),
        name="out_mlp",
    )(x, oa_f, oa_b, ob_f, ob_b, ga, gb, mod, gdn_g, hgrn_g, n2g, final_g,
      jnp.asarray(_seg_ones(GROUP_W // 2), BF16), perm_t, w_out, w1, w2)


def _split_w_in(w):
    g = GROUP_W
    qkv, ga, ab = w[:, 0:3 * g], w[:, 3 * g:4 * g], w[:, 4 * g:4 * g + 4 * N_HEADS]
    rest = w[:, 4 * g + 4 * N_HEADS:]
    ph, gb = rest[:, 0:4 * g], rest[:, 4 * g:5 * g]
    ab = jnp.pad(ab, ((0, 0), (0, AB_PAD - 4 * N_HEADS)))
    return jnp.concatenate([qkv, ga, ab, ph, gb], axis=1).astype(BF16)


def _pad_lanes(v):
    flat = v.reshape(1, -1)
    return jnp.pad(flat, ((0, 0), (0, AB_PAD - flat.shape[1])))


def kernel(x, c, ctx, c_ctx, w_mod, b_mod, norm1_g, norm2_g, w_in, conv_w, a_log, dt_bias,
           gdn_norm_g, hgrn_norm_g, lb_logits, w_out, w_mlp1, w_mlp2, final_g):
    depth = w_mod.shape[0]
    b, _, d = x.shape
    cvec = jnp.concatenate([c, c_ctx[None, :], jnp.zeros((SUBLANES - b - 1, d), F32)], axis=0)
    mod = _modulation(cvec, w_mod, b_mod).reshape(depth, SUBLANES, N_MOD, d)
    zero_state = jnp.zeros((b, 2, N_PAIRS, PAIR_W, PAIR_W), F32)
    fg = final_g.reshape(1, d)

    x_lat, x_ctx = x, ctx
    for l in range(depth):
        need_ctx = l < depth - 1
        mod_lat, mod_ctx = mod[l, 0:b], mod[l, b:b + 1]
        w_cat = _split_w_in(w_in[l])
        n1g = norm1_g[l].reshape(1, d)
        a_pad, dt_pad = _pad_lanes(a_log[l]), _pad_lanes(dt_bias[l])
        gg = jnp.tile(gdn_norm_g[l], N_HEADS).reshape(1, GROUP_W)
        hg = jnp.tile(hgrn_norm_g[l], N_HEADS).reshape(1, GROUP_W)
        wo, w1, w2 = w_out[l].astype(BF16), w_mlp1[l].astype(BF16), w_mlp2[l].astype(BF16)

        qkv_c, ga_c, ab_c, ph_c, gb_c = _projection(x_ctx, mod_ctx, n1g, w_cat, conv_w[l], True, False)
        qkv_l, ga_l, ab_l, ph_l, gb_l = _projection(x_lat, mod_lat, n1g, w_cat, conv_w[l], False, True)

        oa_cf, oa_cb, sa = _gdn_scan(qkv_c, ab_c, a_pad, dt_pad, zero_state)
        oa_lf, oa_lb, _ = _gdn_scan(qkv_l, ab_l, a_pad, dt_pad, sa)

        t_ctx = ph_c.shape[1]
        tt_c = min(HGRN_TT, t_ctx)
        ob_cf, ob_cb, sb = _hgrn_scan(ph_c.reshape(b, t_ctx // tt_c, tt_c, 4 * GROUP_W), lb_logits, zero_state, l)
        ob_lf, ob_lb, _ = _hgrn_scan(ph_l, lb_logits, sb, l)

        n2g = norm2_g[l].reshape(1, d)
        x_lat = _out_mlp(x_lat, oa_lf, oa_lb, ob_lf, ob_lb, ga_l, gb_l, mod_lat, gg, hg, n2g, fg, wo, w1, w2,
                         False, not need_ctx, True)
        if need_ctx:
            x_ctx = _out_mlp(x_ctx, oa_cf, oa_cb, ob_cf.reshape(b, t_ctx, GROUP_W), ob_cb.reshape(b, t_ctx, GROUP_W),
                             ga_c, gb_c, mod_ctx, gg, hg, n2g, fg, wo, w1, w2, True, False, False)
    return x_lat
```

```python
import functools

import numpy as np
import jax
import jax.numpy as jnp
from jax import lax
from jax.experimental import pallas as pl
from jax.experimental.pallas import tpu as pltpu

F32 = jnp.float32
BF16 = jnp.bfloat16

SUBLANES = 8
HEAD_DIM = 64
N_HEADS = 8
GROUP_W = N_HEADS * HEAD_DIM
N_PAIRS = N_HEADS // 2
PAIR_W = 2 * HEAD_DIM
CHUNK = 64
GRID_W = 64
CONV_W = 3
N_MOD = 6
EPS = 1e-6
LB_FLOOR = 1e-30
AB_PAD = 128
VMEM_LIMIT = 56 * 1024 * 1024

PROJ_TM = 512
MLP_TM = 512
GDN_TT = 256
GDN_NSUB = 4
HGRN_TT = 128
HGRN_NCOL = 8


def _dot(a, b):
    return jnp.dot(a, b, preferred_element_type=F32)


def _dot_nt(a, b):
    return lax.dot_general(a, b, (((1,), (1,)), ((), ())), preferred_element_type=F32)


def _dot_sel(sel, x):
    hi = x.astype(BF16)
    lo = (x - hi.astype(F32)).astype(BF16)
    return _dot(sel, hi) + _dot(sel, lo)


def _dot_sel_rhs(x, sel):
    n = x.shape[0]
    hi = x.astype(BF16)
    lo = (x - hi.astype(F32)).astype(BF16)
    y = _dot(jnp.concatenate([hi, lo], axis=0), sel)
    return y[0:n] + y[n:2 * n]


def _bd(x, bdmask):
    return jnp.concatenate([x, x], axis=0) * bdmask.astype(x.dtype)


def _sigmoid(x):
    return 0.5 * jnp.tanh(0.5 * x) + 0.5


def _silu(x):
    return x * _sigmoid(x)


def _softplus(x):
    return jnp.maximum(x, 0.0) + jnp.log1p(jnp.exp(-jnp.abs(x)))


def _cparams(sem):
    return pltpu.CompilerParams(dimension_semantics=sem, vmem_limit_bytes=VMEM_LIMIT)


def _packed_ij():
    i = np.arange(CHUNK)[:, None]
    j = (np.arange(PAIR_W) % HEAD_DIM)[None, :]
    return i, j


def _bd_mask():
    r = np.arange(PAIR_W)[:, None] // HEAD_DIM
    c = np.arange(PAIR_W)[None, :] // HEAD_DIM
    return (r == c).astype(np.float32)


def _seg_ones(width):
    r = np.arange(width)[:, None] // HEAD_DIM
    c = np.arange(width)[None, :] // HEAD_DIM
    return (r == c).astype(np.float32)


_G_INCL, _G_STRICT, _G_EYE, _G_NB8 = range(4)


def _gdn_consts(reverse):
    i, j = _packed_ij()
    t = np.arange(CHUNK)
    if reverse:
        tri = (t[None, :] >= t[:, None])
        incl = j >= i
        strict = j > i
    else:
        tri = (t[None, :] <= t[:, None])
        incl = j <= i
        strict = j < i
    b8 = (i // 8) == (j // 8)
    stack = np.stack([incl, strict, i == j, b8]).astype(np.float32)
    stack[_G_NB8] *= -1.0
    return tri.astype(np.float32), stack


def _gdn_level_masks():
    i, j = _packed_ij()
    b8 = (i // 8) == (j // 8)
    c16 = ((i // 16) == (j // 16)) & ~b8
    c32 = ((i // 32) == (j // 32)) & ((i // 16) != (j // 16))
    c64 = (i // 32) != (j // 32)
    bd = _bd_mask()
    lv = [-b8.astype(np.float32)] + [c.astype(np.float32) for c in (c16, c32, c64)]
    return np.stack([np.concatenate([x, x], axis=0) * bd for x in lv])


def _gdn_expand(reverse):
    d = 1 if reverse else 0
    col = np.arange(AB_PAD)[:, None]
    head = (np.arange(GROUP_W) // HEAD_DIM)[None, :]
    e_lg = (col == d * N_HEADS + head)
    e_bt = (col == 2 * N_HEADS + d * N_HEADS + head)
    return e_lg.astype(np.float32), e_bt.astype(np.float32)


_H_LEVELS = (1, 2, 4, 8, 16, 32)


def _hgrn_consts(reverse):
    r = np.arange(CHUNK)[:, None]
    t = np.arange(CHUNK)[None, :]
    i, j = _packed_ij()
    q_par, k_par = (0, 1) if reverse else (1, 0)
    masks = [((i // (2 * s)) == (j // (2 * s))) & (((i // s) % 2) == q_par) & (((j // s) % 2) == k_par)
             for s in _H_LEVELS]
    masks.append(i == j)
    tri = (t >= r) if reverse else (t <= r)
    return tri.astype(np.float32), np.stack(masks).astype(np.float32)


def _hgrn_exponents(reverse, b_ref, lf):
    w = lf.shape[1]
    gr = SUBLANES
    rig = lax.broadcasted_iota(jnp.int32, (gr, w), 0)
    grp = lambda g: b_ref[gr * g:gr * g + gr, :]
    row = lambda r: b_ref[r:r + 1, :]
    n_grp = CHUNK // gr
    q_par = 0 if reverse else 1
    sign = lambda s: jnp.where(((rig // s) % 2) == q_par, 1.0, -1.0)
    out = []
    r64 = lax.broadcasted_iota(jnp.int32, lf.shape, 0)
    out.append(jnp.where((r64 % 2) == q_par, lf, 0.0))
    bnd2 = (2, 6) if reverse else (1, 5)
    sg2, sg4 = sign(2), sign(4)
    out.append(jnp.concatenate(
        [(grp(g) - jnp.where(rig < 4, row(gr * g + bnd2[0]), row(gr * g + bnd2[1]))) * sg2 for g in range(n_grp)],
        axis=0))
    bnd4 = 4 if reverse else 3
    out.append(jnp.concatenate([(grp(g) - row(gr * g + bnd4)) * sg4 for g in range(n_grp)], axis=0))
    for s in (8, 16, 32):
        parts = []
        for g in range(n_grp):
            blk = (gr * g) // s
            bnd = 2 * s * (blk // 2) + (s if reverse else s - 1)
            parts.append(grp(g) - row(bnd) if (blk % 2) == q_par else row(bnd) - grp(g))
        out.append(jnp.concatenate(parts, axis=0))
    out.append(b_ref[...])
    out.append(row(0 if reverse else CHUNK - 1) - b_ref[...])
    return out


def _mod_kernel(c_ref, w_ref, b_ref, o_ref):
    sc = _silu(c_ref[...]).astype(BF16)
    o_ref[0] = _dot(sc, w_ref[0].astype(BF16)) + b_ref[0]


def _modulation(cvec, w_mod, b_mod):
    depth, d, n = w_mod.shape
    tn = 1536
    return pl.pallas_call(
        _mod_kernel,
        grid=(depth, n // tn),
        in_specs=[pl.BlockSpec((SUBLANES, d), lambda l, j: (0, 0)),
                  pl.BlockSpec((1, d, tn), lambda l, j: (l, 0, j)),
                  pl.BlockSpec((1, 1, tn), lambda l, j: (l, 0, j))],
        out_specs=pl.BlockSpec((1, SUBLANES, tn), lambda l, j: (l, 0, j)),
        out_shape=jax.ShapeDtypeStruct((depth, SUBLANES, n), F32),
        compiler_params=_cparams(("arbitrary", "arbitrary")),
        name="modulation",
    )(cvec, w_mod, b_mod.reshape(depth, 1, n))


_PROJ_WIDTHS = (3 * GROUP_W, GROUP_W, AB_PAD, 4 * GROUP_W, GROUP_W)
_N_ROW_MAJOR = 3


def _norm_mod(x, g, shift, scale):
    y = x * lax.rsqrt(jnp.mean(x * x, axis=-1, keepdims=True) + EPS) * g
    return y * (1.0 + scale) + shift


def _grid_perm(tm):
    rows = tm // GRID_W
    n = np.arange(tm)
    src = (n % rows) * GRID_W + n // rows
    p = np.zeros((tm, tm), np.float32)
    p[n, src] = 1.0
    return p


def _head_sums(x, ones):
    half = ones.shape[0]
    return jnp.concatenate([_dot(x[:, 0:half].astype(BF16), ones), _dot(x[:, half:2 * half].astype(BF16), ones)],
                           axis=1)


def _proj_kernel(col_major, n_tiles, x_ref, xp_ref, xn_ref, mod_ref, g_ref, w_ref, perm_ref, convw_ref, ones_ref,
                 *out_refs):
    tile = pl.program_id(1)
    g, shift, scale = g_ref[...], mod_ref[0, 0:1, :], mod_ref[0, 1:2, :]
    h = _norm_mod(x_ref[0], g, shift, scale).astype(BF16)
    tm = h.shape[0]

    qkv_ref = out_refs[0]
    halo = jnp.concatenate([xp_ref[0], xn_ref[0]], axis=0)
    h_halo = _norm_mod(halo, g, shift, scale).astype(BF16)
    groups = [slice(n * GROUP_W, (n + 1) * GROUP_W) for n in range(3)]
    p_halo = [_dot(h_halo, w_ref[:, c]) for c in groups]
    p = [_dot(h, w_ref[:, c]) for c in groups]

    row = lax.broadcasted_iota(jnp.int32, (tm, 1), 0)
    y = []
    for c, pc, ph in zip(groups, p, p_halo):
        pv = jnp.where(tile > 0, ph[SUBLANES - 1:SUBLANES, :], 0.0)
        nx = jnp.where(tile < n_tiles - 1, ph[SUBLANES:SUBLANES + 1, :], 0.0)
        p_prev = jnp.where(row == 0, pv, pltpu.roll(pc, 1, axis=0))
        p_next = jnp.where(row == tm - 1, nx, pltpu.roll(pc, tm - 1, axis=0))
        y.append(_silu(convw_ref[0:1, c] * p_prev + convw_ref[1:2, c] * pc + convw_ref[2:3, c] * p_next))
    off = _PROJ_WIDTHS[0]
    for n, (ref, w) in enumerate(zip(out_refs, _PROJ_WIDTHS)):
        if n == 0:
            continue
        if n == _N_ROW_MAJOR and col_major:
            h = _dot(perm_ref[...], h).astype(BF16)
        val = _dot(h, w_ref[:, off:off + w])
        ref[0] = val.reshape(ref.shape[1:])
        off += w

    ones = ones_ref[...]
    q, k, v = y
    qkv_ref[0, :, groups[2]] = v
    qkv_ref[0, :, groups[0]] = q * lax.rsqrt(_head_sums(q * q, ones) + EPS) * (HEAD_DIM ** -0.5)
    qkv_ref[0, :, groups[1]] = k * lax.rsqrt(_head_sums(k * k, ones) + EPS)


def _projection(x, mod, g, w_cat, conv_w, shared_mod, col_major):
    b, t, d = x.shape
    tm = PROJ_TM if col_major else min(PROJ_TM, t)
    rows_t = tm // GRID_W
    n_tiles = t // tm
    hb = tm // SUBLANES
    n_hb = t // SUBLANES
    mod_map = (lambda bi, i: (0, 0, 0)) if shared_mod else (lambda bi, i: (bi, 0, 0))
    tok = lambda w: pl.BlockSpec((1, tm, w), lambda bi, i: (bi, i, 0))
    const = lambda shape: pl.BlockSpec(shape, lambda bi, i: (0,) * len(shape), pipeline_mode=pl.Buffered(1))
    out_specs = [tok(w) for w in _PROJ_WIDTHS[:_N_ROW_MAJOR]]
    out_shape = [jax.ShapeDtypeStruct((b, t, w), F32) for w in _PROJ_WIDTHS[:_N_ROW_MAJOR]]
    for w in _PROJ_WIDTHS[_N_ROW_MAJOR:]:
        if col_major:
            out_specs.append(pl.BlockSpec((1, GRID_W, rows_t, w), lambda bi, i: (bi, 0, i, 0)))
            out_shape.append(jax.ShapeDtypeStruct((b, GRID_W, t // GRID_W, w), F32))
        else:
            out_specs.append(tok(w))
            out_shape.append(jax.ShapeDtypeStruct((b, t, w), F32))
    perm = jnp.asarray(_grid_perm(tm), BF16)
    ones = jnp.asarray(_seg_ones(GROUP_W // 2), BF16)
    return pl.pallas_call(
        functools.partial(_proj_kernel, col_major, n_tiles),
        grid=(b, n_tiles),
        in_specs=[tok(d),
                  pl.BlockSpec((1, SUBLANES, d), lambda bi, i: (bi, jnp.maximum(i * hb - 1, 0), 0)),
                  pl.BlockSpec((1, SUBLANES, d), lambda bi, i: (bi, jnp.minimum((i + 1) * hb, n_hb - 1), 0)),
                  pl.BlockSpec((1, N_MOD, d), mod_map),
                  const((1, d)), const(w_cat.shape), const(perm.shape), const(conv_w.shape), const(ones.shape)],
        out_specs=out_specs,
        out_shape=out_shape,
        compiler_params=_cparams(("arbitrary", "arbitrary")),
        name="projection",
    )(x, x, x, mod, g, w_cat, perm, conv_w, ones)


def _gdn_gates(ab, alog_ref, dtb_ref, elg, ebt, lg_sc, bt_sc):
    lg_c = -jnp.exp(alog_ref[...]) * _softplus(ab + dtb_ref[...])
    lg_sc[...] = _dot_sel_rhs(lg_c, elg)
    bt_sc[...] = _dot_sel_rhs(_sigmoid(ab), ebt)


def _gdn_chunk_terms(reverse, items, qkv_ref, row0, lg_sc, bt_sc, tri, cm, bdl, bdm):
    bdm16 = bdm.astype(BF16)
    ld = lambda ref, base=0, off=0: [ref[pl.ds(base + c * CHUNK, CHUNK), off + p * PAIR_W:off + (p + 1) * PAIR_W]
                                     for c, p in items]
    qp, kp, vp = ld(qkv_ref, row0), ld(qkv_ref, row0, GROUP_W), ld(qkv_ref, row0, 2 * GROUP_W)
    bt = ld(bt_sc)
    bd16 = lambda xs: [_bd(x.astype(BF16), bdm16) for x in xs]
    pmul = lambda xs, ys: [_dot(x.astype(BF16), y) for x, y in zip(xs, bd16(ys))]

    gam_c = {c: _dot_sel(tri, lg_sc[c * CHUNK:(c + 1) * CHUNK, :]) for c in sorted({c for c, _ in items})}
    gam = [gam_c[c][:, p * PAIR_W:(p + 1) * PAIR_W] for c, p in items]
    gam_row = [jnp.sum(x * cm(_G_EYE), axis=0, keepdims=True) for x in gam]
    dincl = [jnp.exp((x - r) * cm(_G_INCL)) * cm(_G_INCL) for x, r in zip(gam, gam_row)]
    ktbd = [jnp.concatenate([x, x], axis=0).T.astype(BF16) * bdm16 for x in kp]
    kq = [_dot(jnp.concatenate([a.astype(BF16), b.astype(BF16)], axis=0), c) for a, b, c in zip(kp, qp, ktbd)]
    kk = [x[0:CHUNK] for x in kq]
    qk = [x[CHUNK:2 * CHUNK] for x in kq]
    m = [a * b * d * cm(_G_STRICT) for a, b, d in zip(kk, bt, dincl)]
    m2 = [jnp.concatenate([x, x], axis=0) for x in (y.astype(BF16) for y in m)]
    n1 = [x * cm(_G_NB8) for x in m]
    n2 = [_dot(a.astype(BF16), b * bdl(0)) for a, b in zip(n1, m2)]
    t_inv = [cm(_G_EYE) + x for x in n1]
    both = pmul([jnp.concatenate([a, b], axis=0) for a, b in zip(n2, t_inv)], n2)
    n4 = [x[0:CHUNK] for x in both]
    t_inv = [t + x[CHUNK:2 * CHUNK] for t, x in zip(t_inv, both)]
    t_inv = [t + d for t, d in zip(t_inv, pmul(t_inv, n4))]
    for lvl in (1, 2, 3):
        left = [_dot(t.astype(BF16), b * bdl(lvl)) for t, b in zip(t_inv, m2)]
        t_inv = [t - d for t, d in zip(t_inv, pmul(left, t_inv))]
    eg = [jnp.exp(x) for x in gam]
    rhs = [jnp.concatenate([_bd((v * b).astype(BF16), bdm16), _bd((k * b * e).astype(BF16), bdm16)], axis=1)
           for v, k, b, e in zip(vp, kp, bt, eg)]
    uw = [_dot(t.astype(BF16), r) for t, r in zip(t_inv, rhs)]
    last = 0 if reverse else CHUNK - 1
    g_last = [x[last:last + 1, :] for x in gam]
    return dict(
        u=[x[:, 0:PAIR_W] for x in uw],
        wq=[jnp.concatenate([x[:, PAIR_W:2 * PAIR_W], q * e], axis=0).astype(BF16) for x, q, e in zip(uw, qp, eg)],
        qkd=[(a * d).astype(BF16) for a, d in zip(qk, dincl)],
        kdt=[k * jnp.exp(g - r).astype(BF16) for k, g, r in zip(ktbd, g_last, gam_row)],
        decay=[jnp.exp(g) for g in g_last])


def _gdn_kernel(n_steps, n_sub, *refs):
    (qkv_f, ab_f, qkv_b, ab_b, alog_ref, dtb_ref, s0_ref, tri_ref, cm_ref, bdl_ref, bdm_ref, elg_ref, ebt_ref,
     of_ref, ob_ref, sfin_ref, lg_sc, bt_sc, s_sc) = refs
    step = pl.program_id(1)
    tt = qkv_f.shape[1] // n_sub
    n_chunks = tt // CHUNK

    @pl.when(step == 0)
    def _():
        s_sc[...] = s0_ref[0]

    bdm = bdm_ref[...]
    bdm16 = bdm.astype(BF16)
    tok = ((qkv_f, ab_f), (qkv_b, ab_b))
    o_refs = (of_ref, ob_ref)

    def sub_tile(j, carry):
        row0 = [pl.multiple_of(j * tt, tt), pl.multiple_of((n_sub - 1 - j) * tt, tt)]
        terms = []
        for d, reverse in enumerate((False, True)):
            qkv_ref, ab_ref = tok[d]
            _gdn_gates(ab_ref[0, pl.ds(row0[d], tt), :], alog_ref, dtb_ref, elg_ref[d], ebt_ref[d],
                       lg_sc.at[d], bt_sc.at[d])
            items = [(c, p) for c in range(n_chunks) for p in range(N_PAIRS)]
            cm = functools.partial(lambda dd, idx: cm_ref[dd, idx], d)
            terms.append(_gdn_chunk_terms(reverse, items, qkv_ref.at[0], row0[d], lg_sc.at[d], bt_sc.at[d],
                                          tri_ref[d], cm, lambda idx: bdl_ref[idx], bdm))

        s = [[s_sc[d, p] for p in range(N_PAIRS)] for d in range(2)]
        for ci in range(n_chunks):
            lanes = [(d, p, ((n_chunks - 1 - ci) if d else ci) * N_PAIRS + p)
                     for d in range(2) for p in range(N_PAIRS)]
            sq = [_dot(terms[d]["wq"][i], s[d][p].astype(BF16)) for d, p, i in lanes]
            v_new = [terms[d]["u"][i] - x[0:CHUNK] for (d, p, i), x in zip(lanes, sq)]
            v16 = [x.astype(BF16) for x in v_new]
            ou = [_dot(jnp.concatenate([terms[d]["qkd"][i], terms[d]["kdt"][i]], axis=0), _bd(v, bdm16))
                  for (d, p, i), v in zip(lanes, v16)]
            for (d, p, i), x, y in zip(lanes, sq, ou):
                c = i // N_PAIRS
                o_refs[d][0, pl.ds(row0[d] + c * CHUNK, CHUNK), p * PAIR_W:(p + 1) * PAIR_W] = (
                    x[CHUNK:2 * CHUNK] + y[0:CHUNK])
                s[d][p] = s[d][p] * terms[d]["decay"][i] + y[CHUNK:CHUNK + PAIR_W]
        for d in range(2):
            for p in range(N_PAIRS):
                s_sc[d, p] = s[d][p]
        return carry

    lax.fori_loop(0, n_sub, sub_tile, 0)

    @pl.when(step == n_steps - 1)
    def _():
        sfin_ref[0] = s_sc[...]


def _gdn_scan(qkv, ab, a_log_pad, dt_bias_pad, s0):
    b, t, _ = qkv.shape
    sub = min(GDN_TT, t)
    n_sub = min(GDN_NSUB, t // sub)
    tt = sub * n_sub
    n_tiles = t // tt
    consts = [_gdn_consts(rev) for rev in (False, True)]
    tri = np.stack([c[0] for c in consts])
    cm = np.stack([c[1] for c in consts])
    bdl = _gdn_level_masks()
    expand = [_gdn_expand(rev) for rev in (False, True)]
    e_lg = np.stack([e[0] for e in expand])
    e_bt = np.stack([e[1] for e in expand])
    const = lambda shape: pl.BlockSpec(shape, lambda bi, i: (0,) * len(shape))
    state_spec = pl.BlockSpec((1, 2, N_PAIRS, PAIR_W, PAIR_W), lambda bi, i: (bi, 0, 0, 0, 0))

    def tok_specs(tile_of):
        return [pl.BlockSpec((1, tt, 3 * GROUP_W), lambda bi, i: (bi, tile_of(i), 0)),
                pl.BlockSpec((1, tt, AB_PAD), lambda bi, i: (bi, tile_of(i), 0))]

    fwd_tile = lambda i: i
    bwd_tile = lambda i: n_tiles - 1 - i
    o_f, o_b, s_fin = pl.pallas_call(
        functools.partial(_gdn_kernel, n_tiles, n_sub),
        grid=(b, n_tiles),
        in_specs=tok_specs(fwd_tile) + tok_specs(bwd_tile)
                 + [const((1, AB_PAD)), const((1, AB_PAD)), state_spec,
                    const(tri.shape), const(cm.shape), const(bdl.shape), const((PAIR_W, PAIR_W)),
                    const(e_lg.shape), const(e_bt.shape)],
        out_specs=[pl.BlockSpec((1, tt, GROUP_W), lambda bi, i: (bi, fwd_tile(i), 0)),
                   pl.BlockSpec((1, tt, GROUP_W), lambda bi, i: (bi, bwd_tile(i), 0)),
                   state_spec],
        out_shape=[jax.ShapeDtypeStruct((b, t, GROUP_W), F32),
                   jax.ShapeDtypeStruct((b, t, GROUP_W), F32),
                   jax.ShapeDtypeStruct((b, 2, N_PAIRS, PAIR_W, PAIR_W), F32)],
        scratch_shapes=[pltpu.VMEM((2, sub, GROUP_W), F32)] * 2 + [pltpu.VMEM((2, N_PAIRS, PAIR_W, PAIR_W), F32)],
        compiler_params=_cparams(("arbitrary", "arbitrary")),
        name="gdn_scan",
    )(qkv, ab, qkv, ab, a_log_pad, dt_bias_pad, s0,
      jnp.asarray(tri, BF16), jnp.asarray(cm), jnp.asarray(bdl, BF16), jnp.asarray(_bd_mask()),
      jnp.asarray(e_lg, BF16), jnp.asarray(e_bt, BF16))
    return o_f, o_b, s_fin


def _hgrn_prep(reverse, layer, ph_ref, logits, q_sc, k_sc, lf_sc):
    e = jnp.exp(logits - jnp.max(logits, axis=0, keepdims=True))
    prob = e / jnp.sum(e, axis=0, keepdims=True)
    lb = jnp.maximum(jnp.sum(prob[0:layer + 1], axis=0, keepdims=True) - prob[0:1], 0.0)
    z_off = 2 * GROUP_W if reverse else GROUP_W
    q_sc[...] = _silu(ph_ref[:, 0:GROUP_W]) * (HEAD_DIM ** -0.5)
    sig = _sigmoid(ph_ref[:, z_off:z_off + GROUP_W])
    lf_sc[...] = jnp.log(jnp.maximum(lb, LB_FLOOR) + (1.0 - lb) * sig)
    k_sc[...] = (1.0 - lb) * (1.0 - sig)


def _hgrn_chunk_terms(reverse, n_chunks, ph_ref, q_sc, k_sc, lf_sc, b_sc, ez_sc, tri, hm, bdm16):
    n_lvl = len(_H_LEVELS)
    q_row0 = n_lvl * CHUNK
    k_row0 = (n_lvl + 1) * CHUNK
    last = 0 if reverse else CHUNK - 1
    rows = lambda c: slice(c * CHUNK, (c + 1) * CHUNK)
    lanes = lambda p: slice(p * PAIR_W, (p + 1) * PAIR_W)
    for c in range(n_chunks):
        lf = lf_sc[rows(c), :]
        b_sc[c] = _dot_sel(tri, lf)
        for n, ex in enumerate(_hgrn_exponents(reverse, b_sc.at[c], lf)):
            ez_sc[c, n * CHUNK:(n + 1) * CHUNK, :] = jnp.exp(ex.astype(BF16)).astype(ez_sc.dtype)
    items = [(c, p) for c in range(n_chunks) for p in range(N_PAIRS)]
    ez = lambda row0: [ez_sc[c, row0:row0 + CHUNK, lanes(p)].astype(BF16) for c, p in items]
    q16 = [q_sc[rows(c), lanes(p)].astype(BF16) for c, p in items]
    k16 = [k_sc[rows(c), lanes(p)].astype(BF16) for c, p in items]
    vp = [ph_ref[rows(c), 3 * GROUP_W + p * PAIR_W:3 * GROUP_W + (p + 1) * PAIR_W] for c, p in items]
    twice_t = lambda x: jnp.concatenate([x, x], axis=0).T.astype(BF16)
    ktbd = [twice_t(k_sc[rows(c), lanes(p)]) * bdm16 for c, p in items]
    both = [_dot(jnp.concatenate([q, q * e], axis=0), k) for q, k, e in zip(q16, ktbd, ez(0))]
    a = [x[0:CHUNK] * hm(n_lvl) + x[CHUNK:2 * CHUNK] * hm(0) for x in both]
    for li in range(1, n_lvl):
        e16 = ez(li * CHUNK)
        e_t = [twice_t(ez_sc[c, li * CHUNK:(li + 1) * CHUNK, lanes(p)]) for c, p in items]
        sc = [_dot(q * e, k * et) for q, k, e, et in zip(q16, ktbd, e16, e_t)]
        a = [x + y * hm(li) for x, y in zip(a, sc)]
    return dict(
        o=[_dot(x.astype(BF16), _bd(v.astype(BF16), bdm16)) for x, v in zip(a, vp)],
        qd=[q * e for q, e in zip(q16, ez(q_row0))],
        kd=[k * e for k, e in zip(k16, ez(k_row0))],
        vt=[v.T.astype(BF16) for v in vp],
        decay=[jnp.exp(b_sc[c, last:last + 1, lanes(p)]) for c, p in items])


def _hgrn_kernel(layer, n_steps, ph_f, ph_b, lbl_ref, s0_ref, tri_ref, hm_ref, bdm_ref,
                 of_ref, ob_ref, sfin_ref, q_sc, k_sc, lf_sc, b_sc, ez_sc, s_sc):
    step = pl.program_id(1)
    n_col, tt = ph_f.shape[1], ph_f.shape[2]
    n_chunks = tt // CHUNK

    @pl.when(step == 0)
    def _():
        s_sc[...] = s0_ref[0]

    bdm = bdm_ref[...]
    bdm16 = bdm.astype(BF16)
    o_refs = (of_ref, ob_ref)

    def column(j, carry):
        col = (j, n_col - 1 - j)
        ph = (ph_f.at[0, col[0]], ph_b.at[0, col[1]])
        terms = []
        for d, reverse in enumerate((False, True)):
            sc = (q_sc.at[d], k_sc.at[d], lf_sc.at[d])
            _hgrn_prep(reverse, layer, ph[d], lbl_ref[d], *sc)
            hm = functools.partial(lambda dd, idx: hm_ref[dd, idx], d)
            terms.append(_hgrn_chunk_terms(reverse, n_chunks, ph[d], *sc, b_sc.at[d], ez_sc.at[d], tri_ref[d], hm,
                                           bdm16))

        s = [[s_sc[d, p] for p in range(N_PAIRS)] for d in range(2)]
        for ci in range(n_chunks):
            lanes = [(d, p, ((n_chunks - 1 - ci) if d else ci) * N_PAIRS + p)
                     for d in range(2) for p in range(N_PAIRS)]
            o = [terms[d]["o"][i] + _dot(terms[d]["qd"][i], s[d][p].T.astype(BF16)) for d, p, i in lanes]
            upd = [_dot(terms[d]["vt"][i], terms[d]["kd"][i]) * bdm for d, p, i in lanes]
            for (d, p, i), x, y in zip(lanes, o, upd):
                c = i // N_PAIRS
                o_refs[d][0, col[d], c * CHUNK:(c + 1) * CHUNK, p * PAIR_W:(p + 1) * PAIR_W] = x
                s[d][p] = s[d][p] * terms[d]["decay"][i] + y
        for d in range(2):
            for p in range(N_PAIRS):
                s_sc[d, p] = s[d][p]
        return carry

    lax.fori_loop(0, n_col, column, 0)

    @pl.when(step == n_steps - 1)
    def _():
        sfin_ref[0] = s_sc[...]


def _hgrn_scan(ph, lb_logits, s0, layer):
    b, n_tiles, tt, _ = ph.shape
    n_col = min(HGRN_NCOL, n_tiles)
    n_steps = n_tiles // n_col
    depth = lb_logits.shape[0]
    consts = [_hgrn_consts(rev) for rev in (False, True)]
    tri = np.stack([c[0] for c in consts])
    hm = np.stack([c[1] for c in consts])
    n_chunks = tt // CHUNK
    n_exp = len(_H_LEVELS) + 2
    const = lambda shape: pl.BlockSpec(shape, lambda bi, i: (0,) * len(shape))
    state_spec = pl.BlockSpec((1, 2, N_PAIRS, PAIR_W, PAIR_W), lambda bi, i: (bi, 0, 0, 0, 0))
    fwd = lambda w: pl.BlockSpec((1, n_col, tt, w), lambda bi, i: (bi, i, 0, 0))
    bwd = lambda w: pl.BlockSpec((1, n_col, tt, w), lambda bi, i: (bi, n_steps - 1 - i, 0, 0))
    return pl.pallas_call(
        functools.partial(_hgrn_kernel, layer, n_steps),
        grid=(b, n_steps),
        in_specs=[fwd(4 * GROUP_W), bwd(4 * GROUP_W), const((2, depth, GROUP_W)), state_spec,
                  const(tri.shape), const(hm.shape), const((PAIR_W, PAIR_W))],
        out_specs=[fwd(GROUP_W), bwd(GROUP_W), state_spec],
        out_shape=[jax.ShapeDtypeStruct((b, n_tiles, tt, GROUP_W), F32),
                   jax.ShapeDtypeStruct((b, n_tiles, tt, GROUP_W), F32),
                   jax.ShapeDtypeStruct((b, 2, N_PAIRS, PAIR_W, PAIR_W), F32)],
        scratch_shapes=[pltpu.VMEM((2, tt, GROUP_W), F32)] * 3
                       + [pltpu.VMEM((2, n_chunks, CHUNK, GROUP_W), F32),
                          pltpu.VMEM((2, n_chunks, n_exp * CHUNK, GROUP_W), F32),
                          pltpu.VMEM((2, N_PAIRS, PAIR_W, PAIR_W), F32)],
        compiler_params=_cparams(("arbitrary", "arbitrary")),
        name="hgrn_scan",
    )(ph, ph, jnp.transpose(lb_logits, (1, 0, 2)), s0, jnp.asarray(tri, BF16), jnp.asarray(hm),
      jnp.asarray(_bd_mask()))


def _mlp_kernel(final, col_major, x_ref, oaf_ref, oab_ref, obf_ref, obb_ref, ga_ref, gb_ref, mod_ref,
                gg_ref, hg_ref, n2g_ref, fg_ref, ones_ref, perm_ref, wo_ref, w1_ref, w2_ref, o_ref):
    ones = ones_ref[...]
    tm = x_ref.shape[1]
    oa = oaf_ref[0] + oab_ref[0]
    ob = (obf_ref[0] + obb_ref[0]).reshape(tm, GROUP_W)
    gb = gb_ref[0].reshape(tm, GROUP_W)
    inv_d = 1.0 / HEAD_DIM
    ya = oa * lax.rsqrt(_head_sums(oa * oa, ones) * inv_d + EPS) * gg_ref[...] * _silu(ga_ref[0])
    yb = ob * lax.rsqrt(_head_sums(ob * ob, ones) * inv_d + EPS) * hg_ref[...] * _sigmoid(gb)
    yb = yb.astype(BF16)
    if col_major:
        yb = _dot(perm_ref[...], yb).astype(BF16)
    y = _dot(ya.astype(BF16), wo_ref[0:GROUP_W, :]) + _dot(yb, wo_ref[GROUP_W:2 * GROUP_W, :])
    x1 = x_ref[0] + mod_ref[0, 2:3, :] * y
    h = _norm_mod(x1, n2g_ref[...], mod_ref[0, 3:4, :], mod_ref[0, 4:5, :]).astype(BF16)
    hid = jnp.maximum(_dot(h, w1_ref[...]), 0.0)
    x2 = x1 + mod_ref[0, 5:6, :] * _dot((hid * hid).astype(BF16), w2_ref[...])
    if final:
        x2 = x2 * lax.rsqrt(jnp.mean(x2 * x2, axis=-1, keepdims=True) + EPS) * fg_ref[...]
    o_ref[0] = x2


def _out_mlp(x, oa_f, oa_b, ob_f, ob_b, ga, gb, mod, gdn_g, hgrn_g, n2g, final_g, w_out, w1, w2,
             shared_mod, final, col_major):
    b, t, d = x.shape
    tm = MLP_TM if col_major else min(MLP_TM, t)
    mod_map = (lambda bi, i: (0, 0, 0)) if shared_mod else (lambda bi, i: (bi, 0, 0))
    tok = lambda w: pl.BlockSpec((1, tm, w), lambda bi, i: (bi, i, 0))
    col = pl.BlockSpec((1, GRID_W, tm // GRID_W, GROUP_W), lambda bi, i: (bi, 0, i, 0))
    hg_spec = col if col_major else tok(GROUP_W)
    const = lambda shape: pl.BlockSpec(shape, lambda bi, i: (0,) * len(shape), pipeline_mode=pl.Buffered(1))
    perm_t = jnp.asarray(_grid_perm(tm).T, BF16)
    return pl.pallas_call(
        functools.partial(_mlp_kernel, final, col_major),
        grid=(b, t // tm),
        in_specs=[tok(d), tok(GROUP_W), tok(GROUP_W), hg_spec, hg_spec, tok(GROUP_W), hg_spec,
                  pl.BlockSpec((1, N_MOD, d), mod_map),
                  const((1, GROUP_W)), const((1, GROUP_W)), const((1, d)), const((1, d)),
                  const((GROUP_W // 2, GROUP_W // 2)), const(perm_t.shape),
                  const(w_out.shape), const(w1.shape), const(w2.shape)],
        out_specs=tok(d),
        out_shape=jax.ShapeDtypeStruct((b, t, d), F32),
        compiler_params=_cparams(("arbitrary", "arbitrary")),
        name="out_mlp",
    )(x, oa_f, oa_b, ob_f, ob_b, ga, gb, mod, gdn_g, hgrn_g, n2g, final_g,
      jnp.asarray(_seg_ones(GROUP_W // 2), BF16), perm_t, w_out, w1, w2)


def _split_w_in(w):
    g = GROUP_W
    qkv, ga, ab = w[:, 0:3 * g], w[:, 3 * g:4 * g], w[:, 4 * g:4 * g + 4 * N_HEADS]
    rest = w[:, 4 * g + 4 * N_HEADS:]
    ph, gb = rest[:, 0:4 * g], rest[:, 4 * g:5 * g]
    ab = jnp.pad(ab, ((0, 0), (0, AB_PAD - 4 * N_HEADS)))
    return jnp.concatenate([qkv, ga, ab, ph, gb], axis=1).astype(BF16)


def _pad_lanes(v):
    flat = v.reshape(1, -1)
    return jnp.pad(flat, ((0, 0), (0, AB_PAD - flat.shape[1])))


def kernel(x, c, ctx, c_ctx, w_mod, b_mod, norm1_g, norm2_g, w_in, conv_w, a_log, dt_bias,
           gdn_norm_g, hgrn_norm_g, lb_logits, w_out, w_mlp1, w_mlp2, final_g):
    depth = w_mod.shape[0]
    b, _, d = x.shape
    cvec = jnp.concatenate([c, c_ctx[None, :], jnp.zeros((SUBLANES - b - 1, d), F32)], axis=0)
    mod = _modulation(cvec, w_mod, b_mod).reshape(depth, SUBLANES, N_MOD, d)
    zero_state = jnp.zeros((b, 2, N_PAIRS, PAIR_W, PAIR_W), F32)
    fg = final_g.reshape(1, d)

    x_lat, x_ctx = x, ctx
    for l in range(depth):
        need_ctx = l < depth - 1
        mod_lat, mod_ctx = mod[l, 0:b], mod[l, b:b + 1]
        w_cat = _split_w_in(w_in[l])
        n1g = norm1_g[l].reshape(1, d)
        a_pad, dt_pad = _pad_lanes(a_log[l]), _pad_lanes(dt_bias[l])
        gg = jnp.tile(gdn_norm_g[l], N_HEADS).reshape(1, GROUP_W)
        hg = jnp.tile(hgrn_norm_g[l], N_HEADS).reshape(1, GROUP_W)
        wo, w1, w2 = w_out[l].astype(BF16), w_mlp1[l].astype(BF16), w_mlp2[l].astype(BF16)

        qkv_c, ga_c, ab_c, ph_c, gb_c = _projection(x_ctx, mod_ctx, n1g, w_cat, conv_w[l], True, False)
        qkv_l, ga_l, ab_l, ph_l, gb_l = _projection(x_lat, mod_lat, n1g, w_cat, conv_w[l], False, True)

        oa_cf, oa_cb, sa = _gdn_scan(qkv_c, ab_c, a_pad, dt_pad, zero_state)
        oa_lf, oa_lb, _ = _gdn_scan(qkv_l, ab_l, a_pad, dt_pad, sa)

        t_ctx = ph_c.shape[1]
        tt_c = min(HGRN_TT, t_ctx)
        ob_cf, ob_cb, sb = _hgrn_scan(ph_c.reshape(b, t_ctx // tt_c, tt_c, 4 * GROUP_W), lb_logits, zero_state, l)
        ob_lf, ob_lb, _ = _hgrn_scan(ph_l, lb_logits, sb, l)

        n2g = norm2_g[l].reshape(1, d)
        x_lat = _out_mlp(x_lat, oa_lf, oa_lb, ob_lf, ob_lb, ga_l, gb_l, mod_lat, gg, hg, n2g, fg, wo, w1, w2,
                         False, not need_ctx, True)
        if need_ctx:
            x_ctx = _out_mlp(x_ctx, oa_cf, oa_cb, ob_cf.reshape(b, t_ctx, GROUP_W), ob_cb.reshape(b, t_ctx, GROUP_W),
                             ga_c, gb_c, mod_ctx, gg, hg, n2g, fg, wo, w1, w2, True, False, False)
    return x_lat
```

```python
import functools

import numpy as np
import jax
import jax.numpy as jnp
from jax import lax
from jax.experimental import pallas as pl
from jax.experimental.pallas import tpu as pltpu

F32 = jnp.float32
BF16 = jnp.bfloat16

SUBLANES = 8
HEAD_DIM = 64
N_HEADS = 8
GROUP_W = N_HEADS * HEAD_DIM
N_PAIRS = N_HEADS // 2
PAIR_W = 2 * HEAD_DIM
CHUNK = 64
GRID_W = 64
CONV_W = 3
N_MOD = 6
EPS = 1e-6
LB_FLOOR = 1e-30
AB_PAD = 128
VMEM_LIMIT = 56 * 1024 * 1024

PROJ_TM = 512
MLP_TM = 512
GDN_TT = 256
GDN_NSUB = 4
HGRN_TT = 128
HGRN_NCOL = 8


def _dot(a, b):
    return jnp.dot(a, b, preferred_element_type=F32)


def _dot_nt(a, b):
    return lax.dot_general(a, b, (((1,), (1,)), ((), ())), preferred_element_type=F32)


def _dot_sel(sel, x):
    hi = x.astype(BF16)
    lo = (x - hi.astype(F32)).astype(BF16)
    return _dot(sel, hi) + _dot(sel, lo)


def _dot_sel_rhs(x, sel):
    n = x.shape[0]
    hi = x.astype(BF16)
    lo = (x - hi.astype(F32)).astype(BF16)
    y = _dot(jnp.concatenate([hi, lo], axis=0), sel)
    return y[0:n] + y[n:2 * n]


def _bd(x, bdmask):
    return jnp.concatenate([x, x], axis=0) * bdmask.astype(x.dtype)


def _sigmoid(x):
    return 0.5 * jnp.tanh(0.5 * x) + 0.5


def _silu(x):
    return x * _sigmoid(x)


def _softplus(x):
    return jnp.maximum(x, 0.0) + jnp.log1p(jnp.exp(-jnp.abs(x)))


def _cparams(sem):
    return pltpu.CompilerParams(dimension_semantics=sem, vmem_limit_bytes=VMEM_LIMIT)


def _packed_ij():
    i = np.arange(CHUNK)[:, None]
    j = (np.arange(PAIR_W) % HEAD_DIM)[None, :]
    return i, j


def _bd_mask():
    r = np.arange(PAIR_W)[:, None] // HEAD_DIM
    c = np.arange(PAIR_W)[None, :] // HEAD_DIM
    return (r == c).astype(np.float32)


def _seg_ones(width):
    r = np.arange(width)[:, None] // HEAD_DIM
    c = np.arange(width)[None, :] // HEAD_DIM
    return (r == c).astype(np.float32)


_G_INCL, _G_STRICT, _G_EYE, _G_NB8 = range(4)


def _gdn_consts(reverse):
    i, j = _packed_ij()
    t = np.arange(CHUNK)
    if reverse:
        tri = (t[None, :] >= t[:, None])
        incl = j >= i
        strict = j > i
    else:
        tri = (t[None, :] <= t[:, None])
        incl = j <= i
        strict = j < i
    b8 = (i // 8) == (j // 8)
    stack = np.stack([incl, strict, i == j, b8]).astype(np.float32)
    stack[_G_NB8] *= -1.0
    return tri.astype(np.float32), stack


def _gdn_level_masks():
    i, j = _packed_ij()
    b8 = (i // 8) == (j // 8)
    c16 = ((i // 16) == (j // 16)) & ~b8
    c32 = ((i // 32) == (j // 32)) & ((i // 16) != (j // 16))
    c64 = (i // 32) != (j // 32)
    bd = _bd_mask()
    lv = [-b8.astype(np.float32)] + [c.astype(np.float32) for c in (c16, c32, c64)]
    return np.stack([np.concatenate([x, x], axis=0) * bd for x in lv])


def _gdn_expand(reverse):
    d = 1 if reverse else 0
    col = np.arange(AB_PAD)[:, None]
    head = (np.arange(GROUP_W) // HEAD_DIM)[None, :]
    e_lg = (col == d * N_HEADS + head)
    e_bt = (col == 2 * N_HEADS + d * N_HEADS + head)
    return e_lg.astype(np.float32), e_bt.astype(np.float32)


_H_LEVELS = (1, 2, 4, 8, 16, 32)


def _hgrn_consts(reverse):
    r = np.arange(CHUNK)[:, None]
    t = np.arange(CHUNK)[None, :]
    i, j = _packed_ij()
    q_par, k_par = (0, 1) if reverse else (1, 0)
    masks = [((i // (2 * s)) == (j // (2 * s))) & (((i // s) % 2) == q_par) & (((j // s) % 2) == k_par)
             for s in _H_LEVELS]
    masks.append(i == j)
    tri = (t >= r) if reverse else (t <= r)
    return tri.astype(np.float32), np.stack(masks).astype(np.float32)


def _hgrn_exponents(reverse, b_ref, lf):
    w = lf.shape[1]
    gr = SUBLANES
    rig = lax.broadcasted_iota(jnp.int32, (gr, w), 0)
    grp = lambda g: b_ref[gr * g:gr * g + gr, :]
    row = lambda r: b_ref[r:r + 1, :]
    n_grp = CHUNK // gr
    q_par = 0 if reverse else 1
    sign = lambda s: jnp.where(((rig // s) % 2) == q_par, 1.0, -1.0)
    out = []
    r64 = lax.broadcasted_iota(jnp.int32, lf.shape, 0)
    out.append(jnp.where((r64 % 2) == q_par, lf, 0.0))
    bnd2 = (2, 6) if reverse else (1, 5)
    sg2, sg4 = sign(2), sign(4)
    out.append(jnp.concatenate(
        [(grp(g) - jnp.where(rig < 4, row(gr * g + bnd2[0]), row(gr * g + bnd2[1]))) * sg2 for g in range(n_grp)],
        axis=0))
    bnd4 = 4 if reverse else 3
    out.append(jnp.concatenate([(grp(g) - row(gr * g + bnd4)) * sg4 for g in range(n_grp)], axis=0))
    for s in (8, 16, 32):
        parts = []
        for g in range(n_grp):
            blk = (gr * g) // s
            bnd = 2 * s * (blk // 2) + (s if reverse else s - 1)
            parts.append(grp(g) - row(bnd) if (blk % 2) == q_par else row(bnd) - grp(g))
        out.append(jnp.concatenate(parts, axis=0))
    out.append(b_ref[...])
    out.append(row(0 if reverse else CHUNK - 1) - b_ref[...])
    return out


def _mod_kernel(c_ref, w_ref, b_ref, o_ref):
    sc = _silu(c_ref[...]).astype(BF16)
    o_ref[0] = _dot(sc, w_ref[0].astype(BF16)) + b_ref[0]


def _modulation(cvec, w_mod, b_mod):
    depth, d, n = w_mod.shape
    tn = 1536
    return pl.pallas_call(
        _mod_kernel,
        grid=(depth, n // tn),
        in_specs=[pl.BlockSpec((SUBLANES, d), lambda l, j: (0, 0)),
                  pl.BlockSpec((1, d, tn), lambda l, j: (l, 0, j)),
                  pl.BlockSpec((1, 1, tn), lambda l, j: (l, 0, j))],
        out_specs=pl.BlockSpec((1, SUBLANES, tn), lambda l, j: (l, 0, j)),
        out_shape=jax.ShapeDtypeStruct((depth, SUBLANES, n), F32),
        compiler_params=_cparams(("arbitrary", "arbitrary")),
        name="modulation",
    )(cvec, w_mod, b_mod.reshape(depth, 1, n))


_PROJ_WIDTHS = (3 * GROUP_W, GROUP_W, AB_PAD, 4 * GROUP_W, GROUP_W)
_N_ROW_MAJOR = 3


def _norm_mod(x, g, shift, scale):
    y = x * lax.rsqrt(jnp.mean(x * x, axis=-1, keepdims=True) + EPS) * g
    return y * (1.0 + scale) + shift


def _grid_perm(tm):
    rows = tm // GRID_W
    n = np.arange(tm)
    src = (n % rows) * GRID_W + n // rows
    p = np.zeros((tm, tm), np.float32)
    p[n, src] = 1.0
    return p


def _head_sums(x, ones):
    half = ones.shape[0]
    return jnp.concatenate([_dot(x[:, 0:half].astype(BF16), ones), _dot(x[:, half:2 * half].astype(BF16), ones)],
                           axis=1)


def _proj_kernel(col_major, n_tiles, x_ref, xp_ref, xn_ref, mod_ref, g_ref, w_ref, perm_ref, convw_ref, ones_ref,
                 *out_refs):
    tile = pl.program_id(1)
    g, shift, scale = g_ref[...], mod_ref[0, 0:1, :], mod_ref[0, 1:2, :]
    h = _norm_mod(x_ref[0], g, shift, scale).astype(BF16)
    tm = h.shape[0]

    qkv_ref = out_refs[0]
    halo = jnp.concatenate([xp_ref[0], xn_ref[0]], axis=0)
    h_halo = _norm_mod(halo, g, shift, scale).astype(BF16)
    groups = [slice(n * GROUP_W, (n + 1) * GROUP_W) for n in range(3)]
    p_halo = [_dot(h_halo, w_ref[:, c]) for c in groups]
    p = [_dot(h, w_ref[:, c]) for c in groups]

    row = lax.broadcasted_iota(jnp.int32, (tm, 1), 0)
    y = []
    for c, pc, ph in zip(groups, p, p_halo):
        pv = jnp.where(tile > 0, ph[SUBLANES - 1:SUBLANES, :], 0.0)
        nx = jnp.where(tile < n_tiles - 1, ph[SUBLANES:SUBLANES + 1, :], 0.0)
        p_prev = jnp.where(row == 0, pv, pltpu.roll(pc, 1, axis=0))
        p_next = jnp.where(row == tm - 1, nx, pltpu.roll(pc, tm - 1, axis=0))
        y.append(_silu(convw_ref[0:1, c] * p_prev + convw_ref[1:2, c] * pc + convw_ref[2:3, c] * p_next))
    off = _PROJ_WIDTHS[0]
    for n, (ref, w) in enumerate(zip(out_refs, _PROJ_WIDTHS)):
        if n == 0:
            continue
        if n == _N_ROW_MAJOR and col_major:
            h = _dot(perm_ref[...], h).astype(BF16)
        val = _dot(h, w_ref[:, off:off + w])
        ref[0] = val.reshape(ref.shape[1:])
        off += w

    ones = ones_ref[...]
    q, k, v = y
    qkv_ref[0, :, groups[2]] = v
    qkv_ref[0, :, groups[0]] = q * lax.rsqrt(_head_sums(q * q, ones) + EPS) * (HEAD_DIM ** -0.5)
    qkv_ref[0, :, groups[1]] = k * lax.rsqrt(_head_sums(k * k, ones) + EPS)


def _projection(x, mod, g, w_cat, conv_w, shared_mod, col_major):
    b, t, d = x.shape
    tm = PROJ_TM if col_major else min(PROJ_TM, t)
    rows_t = tm // GRID_W
    n_tiles = t // tm
    hb = tm // SUBLANES
    n_hb = t // SUBLANES
    mod_map = (lambda bi, i: (0, 0, 0)) if shared_mod else (lambda bi, i: (bi, 0, 0))
    tok = lambda w: pl.BlockSpec((1, tm, w), lambda bi, i: (bi, i, 0))
    const = lambda shape: pl.BlockSpec(shape, lambda bi, i: (0,) * len(shape), pipeline_mode=pl.Buffered(1))
    out_specs = [tok(w) for w in _PROJ_WIDTHS[:_N_ROW_MAJOR]]
    out_shape = [jax.ShapeDtypeStruct((b, t, w), F32) for w in _PROJ_WIDTHS[:_N_ROW_MAJOR]]
    for w in _PROJ_WIDTHS[_N_ROW_MAJOR:]:
        if col_major:
            out_specs.append(pl.BlockSpec((1, GRID_W, rows_t, w), lambda bi, i: (bi, 0, i, 0)))
            out_shape.append(jax.ShapeDtypeStruct((b, GRID_W, t // GRID_W, w), F32))
        else:
            out_specs.append(tok(w))
            out_shape.append(jax.ShapeDtypeStruct((b, t, w), F32))
    perm = jnp.asarray(_grid_perm(tm), BF16)
    ones = jnp.asarray(_seg_ones(GROUP_W // 2), BF16)
    return pl.pallas_call(
        functools.partial(_proj_kernel, col_major, n_tiles),
        grid=(b, n_tiles),
        in_specs=[tok(d),
                  pl.BlockSpec((1, SUBLANES, d), lambda bi, i: (bi, jnp.maximum(i * hb - 1, 0), 0)),
                  pl.BlockSpec((1, SUBLANES, d), lambda bi, i: (bi, jnp.minimum((i + 1) * hb, n_hb - 1), 0)),
                  pl.BlockSpec((1, N_MOD, d), mod_map),
                  const((1, d)), const(w_cat.shape), const(perm.shape), const(conv_w.shape), const(ones.shape)],
        out_specs=out_specs,
        out_shape=out_shape,
        compiler_params=_cparams(("arbitrary", "arbitrary")),
        name="projection",
    )(x, x, x, mod, g, w_cat, perm, conv_w, ones)


def _gdn_gates(ab, alog_ref, dtb_ref, elg, ebt, lg_sc, bt_sc):
    lg_c = -jnp.exp(alog_ref[...]) * _softplus(ab + dtb_ref[...])
    lg_sc[...] = _dot_sel_rhs(lg_c, elg)
    bt_sc[...] = _dot_sel_rhs(_sigmoid(ab), ebt)


def _gdn_chunk_terms(reverse, items, qkv_ref, row0, lg_sc, bt_sc, tri, cm, bdl, bdm):
    bdm16 = bdm.astype(BF16)
    ld = lambda ref, base=0, off=0: [ref[pl.ds(base + c * CHUNK, CHUNK), off + p * PAIR_W:off + (p + 1) * PAIR_W]
                                     for c, p in items]
    qp, kp, vp = ld(qkv_ref, row0), ld(qkv_ref, row0, GROUP_W), ld(qkv_ref, row0, 2 * GROUP_W)
    bt = ld(bt_sc)
    bd16 = lambda xs: [_bd(x.astype(BF16), bdm16) for x in xs]
    pmul = lambda xs, ys: [_dot(x.astype(BF16), y) for x, y in zip(xs, bd16(ys))]

    gam_c = {c: _dot_sel(tri, lg_sc[c * CHUNK:(c + 1) * CHUNK, :]) for c in sorted({c for c, _ in items})}
    gam = [gam_c[c][:, p * PAIR_W:(p + 1) * PAIR_W] for c, p in items]
    gam_row = [jnp.sum(x * cm(_G_EYE), axis=0, keepdims=True) for x in gam]
    dincl = [jnp.exp((x - r) * cm(_G_INCL)) * cm(_G_INCL) for x, r in zip(gam, gam_row)]
    ktbd = [jnp.concatenate([x, x], axis=0).T.astype(BF16) * bdm16 for x in kp]
    kq = [_dot(jnp.concatenate([a.astype(BF16), b.astype(BF16)], axis=0), c) for a, b, c in zip(kp, qp, ktbd)]
    kk = [x[0:CHUNK] for x in kq]
    qk = [x[CHUNK:2 * CHUNK] for x in kq]
    m = [a * b * d * cm(_G_STRICT) for a, b, d in zip(kk, bt, dincl)]
    m2 = [jnp.concatenate([x, x], axis=0) for x in (y.astype(BF16) for y in m)]
    n1 = [x * cm(_G_NB8) for x in m]
    n2 = [_dot(a.astype(BF16), b * bdl(0)) for a, b in zip(n1, m2)]
    t_inv = [cm(_G_EYE) + x for x in n1]
    both = pmul([jnp.concatenate([a, b], axis=0) for a, b in zip(n2, t_inv)], n2)
    n4 = [x[0:CHUNK] for x in both]
    t_inv = [t + x[CHUNK:2 * CHUNK] for t, x in zip(t_inv, both)]
    t_inv = [t + d for t, d in zip(t_inv, pmul(t_inv, n4))]
    for lvl in (1, 2, 3):
        left = [_dot(t.astype(BF16), b * bdl(lvl)) for t, b in zip(t_inv, m2)]
        t_inv = [t - d for t, d in zip(t_inv, pmul(left, t_inv))]
    eg = [jnp.exp(x) for x in gam]
    rhs = [jnp.concatenate([_bd((v * b).astype(BF16), bdm16), _bd((k * b * e).astype(BF16), bdm16)], axis=1)
           for v, k, b, e in zip(vp, kp, bt, eg)]
    uw = [_dot(t.astype(BF16), r) for t, r in zip(t_inv, rhs)]
    last = 0 if reverse else CHUNK - 1
    g_last = [x[last:last + 1, :] for x in gam]
    return dict(
        u=[x[:, 0:PAIR_W] for x in uw],
        wq=[jnp.concatenate([x[:, PAIR_W:2 * PAIR_W], q * e], axis=0).astype(BF16) for x, q, e in zip(uw, qp, eg)],
        qkd=[(a * d).astype(BF16) for a, d in zip(qk, dincl)],
        kdt=[k * jnp.exp(g - r).astype(BF16) for k, g, r in zip(ktbd, g_last, gam_row)],
        decay=[jnp.exp(g) for g in g_last])


def _gdn_kernel(n_steps, n_sub, *refs):
    (qkv_f, ab_f, qkv_b, ab_b, alog_ref, dtb_ref, s0_ref, tri_ref, cm_ref, bdl_ref, bdm_ref, elg_ref, ebt_ref,
     of_ref, ob_ref, sfin_ref, lg_sc, bt_sc, s_sc) = refs
    step = pl.program_id(1)
    tt = qkv_f.shape[1] // n_sub
    n_chunks = tt // CHUNK

    @pl.when(step == 0)
    def _():
        s_sc[...] = s0_ref[0]

    bdm = bdm_ref[...]
    bdm16 = bdm.astype(BF16)
    tok = ((qkv_f, ab_f), (qkv_b, ab_b))
    o_refs = (of_ref, ob_ref)

    def sub_tile(j, carry):
        row0 = [pl.multiple_of(j * tt, tt), pl.multiple_of((n_sub - 1 - j) * tt, tt)]
        terms = []
        for d, reverse in enumerate((False, True)):
            qkv_ref, ab_ref = tok[d]
            _gdn_gates(ab_ref[0, pl.ds(row0[d], tt), :], alog_ref, dtb_ref, elg_ref[d], ebt_ref[d],
                       lg_sc.at[d], bt_sc.at[d])
            items = [(c, p) for c in range(n_chunks) for p in range(N_PAIRS)]
            cm = functools.partial(lambda dd, idx: cm_ref[dd, idx], d)
            terms.append(_gdn_chunk_terms(reverse, items, qkv_ref.at[0], row0[d], lg_sc.at[d], bt_sc.at[d],
                                          tri_ref[d], cm, lambda idx: bdl_ref[idx], bdm))

        s = [[s_sc[d, p] for p in range(N_PAIRS)] for d in range(2)]
        for ci in range(n_chunks):
            lanes = [(d, p, ((n_chunks - 1 - ci) if d else ci) * N_PAIRS + p)
                     for d in range(2) for p in range(N_PAIRS)]
            sq = [_dot(terms[d]["wq"][i], s[d][p].astype(BF16)) for d, p, i in lanes]
            v_new = [terms[d]["u"][i] - x[0:CHUNK] for (d, p, i), x in zip(lanes, sq)]
            v16 = [x.astype(BF16) for x in v_new]
            ou = [_dot(jnp.concatenate([terms[d]["qkd"][i], terms[d]["kdt"][i]], axis=0), _bd(v, bdm16))
                  for (d, p, i), v in zip(lanes, v16)]
            for (d, p, i), x, y in zip(lanes, sq, ou):
                c = i // N_PAIRS
                o_refs[d][0, pl.ds(row0[d] + c * CHUNK, CHUNK), p * PAIR_W:(p + 1) * PAIR_W] = (
                    x[CHUNK:2 * CHUNK] + y[0:CHUNK])
                s[d][p] = s[d][p] * terms[d]["decay"][i] + y[CHUNK:CHUNK + PAIR_W]
        for d in range(2):
            for p in range(N_PAIRS):
                s_sc[d, p] = s[d][p]
        return carry

    lax.fori_loop(0, n_sub, sub_tile, 0)

    @pl.when(step == n_steps - 1)
    def _():
        sfin_ref[0] = s_sc[...]


def _gdn_scan(qkv, ab, a_log_pad, dt_bias_pad, s0):
    b, t, _ = qkv.shape
    sub = min(GDN_TT, t)
    n_sub = min(GDN_NSUB, t // sub)
    tt = sub * n_sub
    n_tiles = t // tt
    consts = [_gdn_consts(rev) for rev in (False, True)]
    tri = np.stack([c[0] for c in consts])
    cm = np.stack([c[1] for c in consts])
    bdl = _gdn_level_masks()
    expand = [_gdn_expand(rev) for rev in (False, True)]
    e_lg = np.stack([e[0] for e in expand])
    e_bt = np.stack([e[1] for e in expand])
    const = lambda shape: pl.BlockSpec(shape, lambda bi, i: (0,) * len(shape))
    state_spec = pl.BlockSpec((1, 2, N_PAIRS, PAIR_W, PAIR_W), lambda bi, i: (bi, 0, 0, 0, 0))

    def tok_specs(tile_of):
        return [pl.BlockSpec((1, tt, 3 * GROUP_W), lambda bi, i: (bi, tile_of(i), 0)),
                pl.BlockSpec((1, tt, AB_PAD), lambda bi, i: (bi, tile_of(i), 0))]

    fwd_tile = lambda i: i
    bwd_tile = lambda i: n_tiles - 1 - i
    o_f, o_b, s_fin = pl.pallas_call(
        functools.partial(_gdn_kernel, n_tiles, n_sub),
        grid=(b, n_tiles),
        in_specs=tok_specs(fwd_tile) + tok_specs(bwd_tile)
                 + [const((1, AB_PAD)), const((1, AB_PAD)), state_spec,
                    const(tri.shape), const(cm.shape), const(bdl.shape), const((PAIR_W, PAIR_W)),
                    const(e_lg.shape), const(e_bt.shape)],
        out_specs=[pl.BlockSpec((1, tt, GROUP_W), lambda bi, i: (bi, fwd_tile(i), 0)),
                   pl.BlockSpec((1, tt, GROUP_W), lambda bi, i: (bi, bwd_tile(i), 0)),
                   state_spec],
        out_shape=[jax.ShapeDtypeStruct((b, t, GROUP_W), F32),
                   jax.ShapeDtypeStruct((b, t, GROUP_W), F32),
                   jax.ShapeDtypeStruct((b, 2, N_PAIRS, PAIR_W, PAIR_W), F32)],
        scratch_shapes=[pltpu.VMEM((2, sub, GROUP_W), F32)] * 2 + [pltpu.VMEM((2, N_PAIRS, PAIR_W, PAIR_W), F32)],
        compiler_params=_cparams(("arbitrary", "arbitrary")),
        name="gdn_scan",
    )(qkv, ab, qkv, ab, a_log_pad, dt_bias_pad, s0,
      jnp.asarray(tri, BF16), jnp.asarray(cm), jnp.asarray(bdl, BF16), jnp.asarray(_bd_mask()),
      jnp.asarray(e_lg, BF16), jnp.asarray(e_bt, BF16))
    return o_f, o_b, s_fin


def _hgrn_prep(reverse, layer, ph_ref, logits, q_sc, k_sc, lf_sc):
    e = jnp.exp(logits - jnp.max(logits, axis=0, keepdims=True))
    prob = e / jnp.sum(e, axis=0, keepdims=True)
    lb = jnp.maximum(jnp.sum(prob[0:layer + 1], axis=0, keepdims=True) - prob[0:1], 0.0)
    z_off = 2 * GROUP_W if reverse else GROUP_W
    q_sc[...] = _silu(ph_ref[:, 0:GROUP_W]) * (HEAD_DIM ** -0.5)
    sig = _sigmoid(ph_ref[:, z_off:z_off + GROUP_W])
    lf_sc[...] = jnp.log(jnp.maximum(lb, LB_FLOOR) + (1.0 - lb) * sig)
    k_sc[...] = (1.0 - lb) * (1.0 - sig)


def _hgrn_chunk_terms(reverse, n_chunks, ph_ref, q_sc, k_sc, lf_sc, b_sc, ez_sc, tri, hm, bdm16):
    n_lvl = len(_H_LEVELS)
    q_row0 = n_lvl * CHUNK
    k_row0 = (n_lvl + 1) * CHUNK
    last = 0 if reverse else CHUNK - 1
    rows = lambda c: slice(c * CHUNK, (c + 1) * CHUNK)
    lanes = lambda p: slice(p * PAIR_W, (p + 1) * PAIR_W)
    for c in range(n_chunks):
        lf = lf_sc[rows(c), :]
        b_sc[c] = _dot_sel(tri, lf)
        for n, ex in enumerate(_hgrn_exponents(reverse, b_sc.at[c], lf)):
            ez_sc[c, n * CHUNK:(n + 1) * CHUNK, :] = jnp.exp(ex.astype(BF16)).astype(ez_sc.dtype)
    items = [(c, p) for c in range(n_chunks) for p in range(N_PAIRS)]
    ez = lambda row0: [ez_sc[c, row0:row0 + CHUNK, lanes(p)].astype(BF16) for c, p in items]
    q16 = [q_sc[rows(c), lanes(p)].astype(BF16) for c, p in items]
    k16 = [k_sc[rows(c), lanes(p)].astype(BF16) for c, p in items]
    vp = [ph_ref[rows(c), 3 * GROUP_W + p * PAIR_W:3 * GROUP_W + (p + 1) * PAIR_W] for c, p in items]
    twice_t = lambda x: jnp.concatenate([x, x], axis=0).T.astype(BF16)
    ktbd = [twice_t(k_sc[rows(c), lanes(p)]) * bdm16 for c, p in items]
    both = [_dot(jnp.concatenate([q, q * e], axis=0), k) for q, k, e in zip(q16, ktbd, ez(0))]
    a = [x[0:CHUNK] * hm(n_lvl) + x[CHUNK:2 * CHUNK] * hm(0) for x in both]
    for li in range(1, n_lvl):
        e16 = ez(li * CHUNK)
        e_t = [twice_t(ez_sc[c, li * CHUNK:(li + 1) * CHUNK, lanes(p)]) for c, p in items]
        sc = [_dot(q * e, k * et) for q, k, e, et in zip(q16, ktbd, e16, e_t)]
        a = [x + y * hm(li) for x, y in zip(a, sc)]
    return dict(
        a=[x.astype(BF16) for x in a],
        bdv=[_bd(v.astype(BF16), bdm16) for v in vp],
        qd=[q * e for q, e in zip(q16, ez(q_row0))],
        kd=[k * e for k, e in zip(k16, ez(k_row0))],
        vt=[v.T.astype(BF16) for v in vp],
        decay=[jnp.exp(b_sc[c, last:last + 1, lanes(p)]) for c, p in items])


def _hgrn_kernel(layer, n_steps, ph_f, ph_b, lbl_ref, s0_ref, tri_ref, hm_ref, bdm_ref,
                 of_ref, ob_ref, sfin_ref, q_sc, k_sc, lf_sc, b_sc, ez_sc, s_sc):
    step = pl.program_id(1)
    n_col, tt = ph_f.shape[1], ph_f.shape[2]
    n_chunks = tt // CHUNK

    @pl.when(step == 0)
    def _():
        s_sc[...] = s0_ref[0]

    bdm = bdm_ref[...]
    bdm16 = bdm.astype(BF16)
    o_refs = (of_ref, ob_ref)

    def column(j, carry):
        col = (j, n_col - 1 - j)
        ph = (ph_f.at[0, col[0]], ph_b.at[0, col[1]])
        terms = []
        for d, reverse in enumerate((False, True)):
            sc = (q_sc.at[d], k_sc.at[d], lf_sc.at[d])
            _hgrn_prep(reverse, layer, ph[d], lbl_ref[d], *sc)
            hm = functools.partial(lambda dd, idx: hm_ref[dd, idx], d)
            terms.append(_hgrn_chunk_terms(reverse, n_chunks, ph[d], *sc, b_sc.at[d], ez_sc.at[d], tri_ref[d], hm,
                                           bdm16))

        s = [[s_sc[d, p] for p in range(N_PAIRS)] for d in range(2)]
        for ci in range(n_chunks):
            lanes = [(d, p, ((n_chunks - 1 - ci) if d else ci) * N_PAIRS + p)
                     for d in range(2) for p in range(N_PAIRS)]
            o = [_dot(jnp.concatenate([terms[d]["a"][i], terms[d]["qd"][i]], axis=1),
                      jnp.concatenate([terms[d]["bdv"][i], s[d][p].T.astype(BF16)], axis=0)) for d, p, i in lanes]
            upd = [_dot(terms[d]["vt"][i], terms[d]["kd"][i]) * bdm for d, p, i in lanes]
            for (d, p, i), x, y in zip(lanes, o, upd):
                c = i // N_PAIRS
                o_refs[d][0, col[d], c * CHUNK:(c + 1) * CHUNK, p * PAIR_W:(p + 1) * PAIR_W] = x
                s[d][p] = s[d][p] * terms[d]["decay"][i] + y
        for d in range(2):
            for p in range(N_PAIRS):
                s_sc[d, p] = s[d][p]
        return carry

    lax.fori_loop(0, n_col, column, 0)

    @pl.when(step == n_steps - 1)
    def _():
        sfin_ref[0] = s_sc[...]


def _hgrn_scan(ph, lb_logits, s0, layer):
    b, n_tiles, tt, _ = ph.shape
    n_col = min(HGRN_NCOL, n_tiles)
    n_steps = n_tiles // n_col
    depth = lb_logits.shape[0]
    consts = [_hgrn_consts(rev) for rev in (False, True)]
    tri = np.stack([c[0] for c in consts])
    hm = np.stack([c[1] for c in consts])
    n_chunks = tt // CHUNK
    n_exp = len(_H_LEVELS) + 2
    const = lambda shape: pl.BlockSpec(shape, lambda bi, i: (0,) * len(shape))
    state_spec = pl.BlockSpec((1, 2, N_PAIRS, PAIR_W, PAIR_W), lambda bi, i: (bi, 0, 0, 0, 0))
    fwd = lambda w: pl.BlockSpec((1, n_col, tt, w), lambda bi, i: (bi, i, 0, 0))
    bwd = lambda w: pl.BlockSpec((1, n_col, tt, w), lambda bi, i: (bi, n_steps - 1 - i, 0, 0))
    return pl.pallas_call(
        functools.partial(_hgrn_kernel, layer, n_steps),
        grid=(b, n_steps),
        in_specs=[fwd(4 * GROUP_W), bwd(4 * GROUP_W), const((2, depth, GROUP_W)), state_spec,
                  const(tri.shape), const(hm.shape), const((PAIR_W, PAIR_W))],
        out_specs=[fwd(GROUP_W), bwd(GROUP_W), state_spec],
        out_shape=[jax.ShapeDtypeStruct((b, n_tiles, tt, GROUP_W), F32),
                   jax.ShapeDtypeStruct((b, n_tiles, tt, GROUP_W), F32),
                   jax.ShapeDtypeStruct((b, 2, N_PAIRS, PAIR_W, PAIR_W), F32)],
        scratch_shapes=[pltpu.VMEM((2, tt, GROUP_W), F32)] * 3
                       + [pltpu.VMEM((2, n_chunks, CHUNK, GROUP_W), F32),
                          pltpu.VMEM((2, n_chunks, n_exp * CHUNK, GROUP_W), F32),
                          pltpu.VMEM((2, N_PAIRS, PAIR_W, PAIR_W), F32)],
        compiler_params=_cparams(("arbitrary", "arbitrary")),
        name="hgrn_scan",
    )(ph, ph, jnp.transpose(lb_logits, (1, 0, 2)), s0, jnp.asarray(tri, BF16), jnp.asarray(hm),
      jnp.asarray(_bd_mask()))


def _mlp_kernel(final, col_major, x_ref, oaf_ref, oab_ref, obf_ref, obb_ref, ga_ref, gb_ref, mod_ref,
                gg_ref, hg_ref, n2g_ref, fg_ref, ones_ref, perm_ref, wo_ref, w1_ref, w2_ref, o_ref):
    ones = ones_ref[...]
    tm = x_ref.shape[1]
    oa = oaf_ref[0] + oab_ref[0]
    ob = (obf_ref[0] + obb_ref[0]).reshape(tm, GROUP_W)
    gb = gb_ref[0].reshape(tm, GROUP_W)
    inv_d = 1.0 / HEAD_DIM
    ya = oa * lax.rsqrt(_head_sums(oa * oa, ones) * inv_d + EPS) * gg_ref[...] * _silu(ga_ref[0])
    yb = ob * lax.rsqrt(_head_sums(ob * ob, ones) * inv_d + EPS) * hg_ref[...] * _sigmoid(gb)
    yb = yb.astype(BF16)
    if col_major:
        yb = _dot(perm_ref[...], yb).astype(BF16)
    y = _dot(ya.astype(BF16), wo_ref[0:GROUP_W, :]) + _dot(yb, wo_ref[GROUP_W:2 * GROUP_W, :])
    x1 = x_ref[0] + mod_ref[0, 2:3, :] * y
    h = _norm_mod(x1, n2g_ref[...], mod_ref[0, 3:4, :], mod_ref[0, 4:5, :]).astype(BF16)
    hid = jnp.maximum(_dot(h, w1_ref[...]), 0.0)
    x2 = x1 + mod_ref[0, 5:6, :] * _dot((hid * hid).astype(BF16), w2_ref[...])
    if final:
        x2 = x2 * lax.rsqrt(jnp.mean(x2 * x2, axis=-1, keepdims=True) + EPS) * fg_ref[...]
    o_ref[0] = x2


def _out_mlp(x, oa_f, oa_b, ob_f, ob_b, ga, gb, mod, gdn_g, hgrn_g, n2g, final_g, w_out, w1, w2,
             shared_mod, final, col_major):
    b, t, d = x.shape
    tm = MLP_TM if col_major else min(MLP_TM, t)
    mod_map = (lambda bi, i: (0, 0, 0)) if shared_mod else (lambda bi, i: (bi, 0, 0))
    tok = lambda w: pl.BlockSpec((1, tm, w), lambda bi, i: (bi, i, 0))
    col = pl.BlockSpec((1, GRID_W, tm // GRID_W, GROUP_W), lambda bi, i: (bi, 0, i, 0))
    hg_spec = col if col_major else tok(GROUP_W)
    const = lambda shape: pl.BlockSpec(shape, lambda bi, i: (0,) * len(shape), pipeline_mode=pl.Buffered(1))
    perm_t = jnp.asarray(_grid_perm(tm).T, BF16)
    return pl.pallas_call(
        functools.partial(_mlp_kernel, final, col_major),
        grid=(b, t // tm),
        in_specs=[tok(d), tok(GROUP_W), tok(GROUP_W), hg_spec, hg_spec, tok(GROUP_W), hg_spec,
                  pl.BlockSpec((1, N_MOD, d), mod_map),
                  const((1, GROUP_W)), const((1, GROUP_W)), const((1, d)), const((1, d)),
                  const((GROUP_W // 2, GROUP_W // 2)), const(perm_t.shape),
                  const(w_out.shape), const(w1.shape), const(w2.shape)],
        out_specs=tok(d),
        out_shape=jax.ShapeDtypeStruct((b, t, d), F32),
        compiler_params=_cparams(("arbitrary", "arbitrary")),
        name="out_mlp",
    )(x, oa_f, oa_b, ob_f, ob_b, ga, gb, mod, gdn_g, hgrn_g, n2g, final_g,
      jnp.asarray(_seg_ones(GROUP_W // 2), BF16), perm_t, w_out, w1, w2)


def _split_w_in(w):
    g = GROUP_W
    qkv, ga, ab = w[:, 0:3 * g], w[:, 3 * g:4 * g], w[:, 4 * g:4 * g + 4 * N_HEADS]
    rest = w[:, 4 * g + 4 * N_HEADS:]
    ph, gb = rest[:, 0:4 * g], rest[:, 4 * g:5 * g]
    ab = jnp.pad(ab, ((0, 0), (0, AB_PAD - 4 * N_HEADS)))
    return jnp.concatenate([qkv, ga, ab, ph, gb], axis=1).astype(BF16)


def _pad_lanes(v):
    flat = v.reshape(1, -1)
    return jnp.pad(flat, ((0, 0), (0, AB_PAD - flat.shape[1])))


def kernel(x, c, ctx, c_ctx, w_mod, b_mod, norm1_g, norm2_g, w_in, conv_w, a_log, dt_bias,
           gdn_norm_g, hgrn_norm_g, lb_logits, w_out, w_mlp1, w_mlp2, final_g):
    depth = w_mod.shape[0]
    b, _, d = x.shape
    cvec = jnp.concatenate([c, c_ctx[None, :], jnp.zeros((SUBLANES - b - 1, d), F32)], axis=0)
    mod = _modulation(cvec, w_mod, b_mod).reshape(depth, SUBLANES, N_MOD, d)
    zero_state = jnp.zeros((b, 2, N_PAIRS, PAIR_W, PAIR_W), F32)
    fg = final_g.reshape(1, d)

    x_lat, x_ctx = x, ctx
    for l in range(depth):
        need_ctx = l < depth - 1
        mod_lat, mod_ctx = mod[l, 0:b], mod[l, b:b + 1]
        w_cat = _split_w_in(w_in[l])
        n1g = norm1_g[l].reshape(1, d)
        a_pad, dt_pad = _pad_lanes(a_log[l]), _pad_lanes(dt_bias[l])
        gg = jnp.tile(gdn_norm_g[l], N_HEADS).reshape(1, GROUP_W)
        hg = jnp.tile(hgrn_norm_g[l], N_HEADS).reshape(1, GROUP_W)
        wo, w1, w2 = w_out[l].astype(BF16), w_mlp1[l].astype(BF16), w_mlp2[l].astype(BF16)

        qkv_c, ga_c, ab_c, ph_c, gb_c = _projection(x_ctx, mod_ctx, n1g, w_cat, conv_w[l], True, False)
        qkv_l, ga_l, ab_l, ph_l, gb_l = _projection(x_lat, mod_lat, n1g, w_cat, conv_w[l], False, True)

        oa_cf, oa_cb, sa = _gdn_scan(qkv_c, ab_c, a_pad, dt_pad, zero_state)
        oa_lf, oa_lb, _ = _gdn_scan(qkv_l, ab_l, a_pad, dt_pad, sa)

        t_ctx = ph_c.shape[1]
        tt_c = min(HGRN_TT, t_ctx)
        ob_cf, ob_cb, sb = _hgrn_scan(ph_c.reshape(b, t_ctx // tt_c, tt_c, 4 * GROUP_W), lb_logits, zero_state, l)
        ob_lf, ob_lb, _ = _hgrn_scan(ph_l, lb_logits, sb, l)

        n2g = norm2_g[l].reshape(1, d)
        x_lat = _out_mlp(x_lat, oa_lf, oa_lb, ob_lf, ob_lb, ga_l, gb_l, mod_lat, gg, hg, n2g, fg, wo, w1, w2,
                         False, not need_ctx, True)
        if need_ctx:
            x_ctx = _out_mlp(x_ctx, oa_cf, oa_cb, ob_cf.reshape(b, t_ctx, GROUP_W), ob_cb.reshape(b, t_ctx, GROUP_W),
                             ga_c, gb_c, mod_ctx, gg, hg, n2g, fg, wo, w1, w2, True, False, False)
    return x_lat
```
